```python
import jax
import jax.numpy as jnp
from jax import lax
import numpy as np

D_MODEL = 1024
BATCH = 16
SEQ = 4096
DEPTH = 1

MEM_LEN = 256
NORM_EPS = 1e-6

MLA_HEADS = 8
MLA_NOPE_DIM = 64
MLA_ROPE_DIM = 32
MLA_QK_DIM = MLA_NOPE_DIM + MLA_ROPE_DIM
MLA_V_DIM = 64
MLA_Q_RANK = 192
MLA_KV_RANK = 128
ROPE_BASE = 10000.0
Q_BLOCK = 128

HG_HEADS = 4
HG_KEY_DIM = 128
HG_VAL_DIM = 128
HG_CHUNK = 64

MLA_WIDTH = MLA_HEADS * MLA_V_DIM
HG_FWIDTH = HG_HEADS * HG_KEY_DIM
HG_WIDTH = HG_HEADS * HG_VAL_DIM
MIX_WIDTH = MLA_WIDTH + HG_WIDTH
IN_SIZES = (MLA_Q_RANK, MLA_KV_RANK, MLA_ROPE_DIM, HG_FWIDTH, HG_FWIDTH, HG_FWIDTH, HG_WIDTH, HG_WIDTH)
IN_WIDTH = sum(IN_SIZES)

X_HEADS = 4
X_HEAD_DIM = D_MODEL // X_HEADS

MOE_GROUPS = 8
MOE_PER_GROUP = 8
MOE_EXPERTS = MOE_GROUPS * MOE_PER_GROUP
MOE_TOP_K = 2
MOE_HIDDEN = 256
MOE_BLOCK = 128

kernel_name = 'hybrid_mla_hgrn2_hmoe_encoder_layer'


def rms_norm(x, w):
    xf = x.astype(jnp.float32)
    y = xf * lax.rsqrt(jnp.mean(xf * xf, axis=-1, keepdims=True) + NORM_EPS)
    return (y * w.astype(jnp.float32)).astype(x.dtype)


def split_columns(p):
    out, start = [], 0
    for size in IN_SIZES:
        out.append(p[..., start:start + size])
        start += size
    return out


def rope_tables(positions):
    half = MLA_ROPE_DIM // 2
    inv_freq = 1.0 / (ROPE_BASE ** (jnp.arange(half, dtype=jnp.float32) / half))
    ang = positions.astype(jnp.float32)[..., None] * inv_freq
    return jnp.cos(ang)[:, :, None, :], jnp.sin(ang)[:, :, None, :]


def apply_rope(x, cos, sin):
    x1, x2 = jnp.split(x.astype(jnp.float32), 2, axis=-1)
    return jnp.concatenate([x1 * cos - x2 * sin, x2 * cos + x1 * sin], axis=-1).astype(x.dtype)


def blocked_attention(q, k, v, scale):
    B, S, H, Dq = q.shape
    nb = S // Q_BLOCK
    qb = q.reshape(B, nb, Q_BLOCK, H, Dq).transpose(1, 0, 2, 3, 4)

    def one_block(qblk):
        s = jnp.einsum('bqhd,bkhd->bhqk', qblk, k).astype(jnp.float32) * scale
        p = jax.nn.softmax(s, axis=-1).astype(v.dtype)
        return jnp.einsum('bhqk,bkhd->bqhd', p, v)

    o = lax.map(one_block, qb)
    return o.transpose(1, 0, 2, 3, 4).reshape(B, S, H, v.shape[-1])


def mla_mixer(c_q, c_kv, k_rope, cos, sin, q_a_norm, w_q_up, kv_a_norm, w_kv_up, q_norm, k_norm):
    B, S, _ = c_q.shape
    q = (rms_norm(c_q, q_a_norm) @ w_q_up).reshape(B, S, MLA_HEADS, MLA_QK_DIM)
    kv = (rms_norm(c_kv, kv_a_norm) @ w_kv_up).reshape(B, S, MLA_HEADS, MLA_NOPE_DIM + MLA_V_DIM)
    k_nope, v = kv[..., :MLA_NOPE_DIM], kv[..., MLA_NOPE_DIM:]
    k_r = jnp.broadcast_to(k_rope[:, :, None, :], (B, S, MLA_HEADS, MLA_ROPE_DIM))
    k = jnp.concatenate([k_nope, k_r], axis=-1)
    q = rms_norm(q, q_norm)
    k = rms_norm(k, k_norm)
    q = jnp.concatenate([q[..., :MLA_NOPE_DIM], apply_rope(q[..., MLA_NOPE_DIM:], cos, sin)], axis=-1)
    k = jnp.concatenate([k[..., :MLA_NOPE_DIM], apply_rope(k[..., MLA_NOPE_DIM:], cos, sin)], axis=-1)
    o = blocked_attention(q, k, v, MLA_QK_DIM ** -0.5)
    return o.reshape(B, S, MLA_WIDTH)


def gla_chunk_scan(q, k, g, v):
    B, S, H, DK = q.shape
    DV = v.shape[-1]
    n = S // HG_CHUNK

    def to_chunks(t):
        return t.astype(jnp.float32).reshape(B, n, HG_CHUNK, H, t.shape[-1]).transpose(1, 0, 3, 2, 4)

    qc, kc, gc, vc = to_chunks(q), to_chunks(k), to_chunks(g), to_chunks(v)
    mask = jnp.tril(jnp.ones((HG_CHUNK, HG_CHUNK), dtype=bool))

    def step(state, inp):
        qi, ki, gi, vi = inp
        b = jnp.cumsum(gi, axis=2)
        o_inter = jnp.einsum('bhtd,bhdv->bhtv', qi * jnp.exp(b), state)
        diff = b[:, :, :, None, :] - b[:, :, None, :, :]
        decay = jnp.exp(jnp.where(mask[:, :, None], diff, -jnp.inf))
        attn = jnp.einsum('bhtd,bhsd,bhtsd->bhts', qi, ki, decay)
        o_intra = jnp.einsum('bhts,bhsv->bhtv', attn, vi)
        b_last = b[:, :, -1:, :]
        new_state = jnp.exp(b_last[:, :, 0, :])[..., None] * state + jnp.einsum(
            'bhsd,bhsv->bhdv', ki * jnp.exp(b_last - b), vi)
        return new_state, o_inter + o_intra

    s0 = jnp.zeros((B, H, DK, DV), jnp.float32)
    _, o = lax.scan(step, s0, (qc, kc, gc, vc))
    return o.transpose(1, 0, 3, 2, 4).reshape(B, S, H, DV)


def hgrn2_mixer(hq, hf_fwd, hf_bwd, hi, hg, lb, o_norm):
    B, S, _ = hq.shape
    q = jax.nn.silu(hq).reshape(B, S, HG_HEADS, HG_KEY_DIM)
    i = hi.reshape(B, S, HG_HEADS, HG_VAL_DIM)

    def direction(f_pre, lb_dir, reverse):
        z = f_pre.astype(jnp.float32)
        log_f = jnp.logaddexp(jnp.log(lb_dir), jnp.log1p(-lb_dir) + jax.nn.log_sigmoid(z))
        log_f = log_f.reshape(B, S, HG_HEADS, HG_KEY_DIM)
        k = -jnp.expm1(log_f)
        args = (q, k, log_f, i)
        if reverse:
            args = tuple(jnp.flip(a, axis=1) for a in args)
        o = gla_chunk_scan(*args)
        return jnp.flip(o, axis=1) if reverse else o

    o = direction(hf_fwd, lb[0], False) + direction(hf_bwd, lb[1], True)
    o = rms_norm(o, o_norm).astype(hq.dtype) * jax.nn.silu(hg.reshape(B, S, HG_HEADS, HG_VAL_DIM))
    return o.reshape(B, S, HG_WIDTH)


def cross_attention(h, m, w_q, w_kv, q_norm, k_norm, w_o):
    B, S, D = h.shape
    M = m.shape[1]
    q = rms_norm((h @ w_q).reshape(B, S, X_HEADS, X_HEAD_DIM), q_norm)
    kv = (m @ w_kv).reshape(B, M, 2, X_HEADS, X_HEAD_DIM)
    k = rms_norm(kv[:, :, 0], k_norm)
    v = kv[:, :, 1]
    s = jnp.einsum('bqhd,bkhd->bhqk', q, k).astype(jnp.float32) * (X_HEAD_DIM ** -0.5)
    p = jax.nn.softmax(s, axis=-1).astype(v.dtype)
    o = jnp.einsum('bhqk,bkhd->bqhd', p, v).reshape(B, S, X_HEADS * X_HEAD_DIM)
    return o @ w_o


def dropless_experts(t, expert_id, weight, w_gate, w_up, w_down):
    T, D = t.shape
    A = T * MOE_TOP_K
    e_flat = expert_id.reshape(A)
    w_flat = weight.reshape(A)
    tok_flat = jnp.arange(A, dtype=jnp.int32) // MOE_TOP_K
    order = jnp.argsort(e_flat)
    e_sorted, tok_sorted, w_sorted = e_flat[order], tok_flat[order], w_flat[order]
    counts = jnp.bincount(e_flat, length=MOE_EXPERTS)
    padded = (counts + MOE_BLOCK - 1) // MOE_BLOCK * MOE_BLOCK
    start = jnp.cumsum(counts) - counts
    pend = jnp.cumsum(padded)
    pstart = pend - padded
    dest = pstart[e_sorted] + jnp.arange(A, dtype=jnp.int32) - start[e_sorted]
    n_blocks = -(-A // MOE_BLOCK) + MOE_EXPERTS
    rows = jnp.zeros((n_blocks * MOE_BLOCK, D), t.dtype).at[dest].set(t[tok_sorted])
    block_e = jnp.searchsorted(pend, jnp.arange(n_blocks, dtype=jnp.int32) * MOE_BLOCK, side='right')
    block_e = jnp.minimum(block_e, MOE_EXPERTS - 1)

    def expert_block(args):
        xb, e = args
        hb = jax.nn.silu(xb @ w_gate[e]) * (xb @ w_up[e])
        return hb @ w_down[e]

    out = lax.map(expert_block, (rows.reshape(n_blocks, MOE_BLOCK, D), block_e)).reshape(-1, D)
    contrib = out[dest] * w_sorted[:, None].astype(out.dtype)
    return jnp.zeros((T, D), out.dtype).at[tok_sorted].add(contrib)


def hier_moe(h, w_group, b_group, w_expert, b_expert, w_gate, w_up, w_down):
    B, S, D = h.shape
    t = h.reshape(B * S, D)
    g_logits = (t @ w_group).astype(jnp.float32) + b_group.astype(jnp.float32)
    g_prob = jax.nn.softmax(g_logits, axis=-1)
    g_sel = jnp.argmax(g_logits, axis=-1).astype(jnp.int32)
    g_weight = jnp.take_along_axis(g_prob, g_sel[:, None], axis=-1)
    e_logits = ((t @ w_expert).astype(jnp.float32) + b_expert.astype(jnp.float32)).reshape(
        -1, MOE_GROUPS, MOE_PER_GROUP)
    e_logits = jnp.take_along_axis(e_logits, g_sel[:, None, None], axis=1)[:, 0]
    top_logit, top_local = lax.top_k(e_logits, MOE_TOP_K)
    top_w = jax.nn.softmax(top_logit, axis=-1) * g_weight
    expert_id = g_sel[:, None] * MOE_PER_GROUP + top_local.astype(jnp.int32)
    y = dropless_experts(t, expert_id, top_w, w_gate, w_up, w_down)
    return y.reshape(B, S, D)


def setup_inputs(seed: int = 0) -> dict:
    key = jax.random.key(seed)
    keys = jax.random.split(key, 40)
    counter = [0]

    def nk():
        k = keys[counter[0]]
        counter[0] += 1
        return k

    def nrm(shape, fan_in):
        return jax.random.normal(nk(), shape, jnp.float32) * (fan_in ** -0.5)

    def gain(shape):
        return 1.0 + 0.02 * jax.random.normal(nk(), shape, jnp.float32)

    L = DEPTH
    x = jax.random.normal(nk(), (BATCH, SEQ, D_MODEL), jnp.float32)
    mem = jax.random.normal(nk(), (BATCH, MEM_LEN, D_MODEL), jnp.float32)
    positions = jnp.arange(SEQ, dtype=jnp.int32)[None, :] + jax.random.randint(
        nk(), (BATCH, 1), 0, SEQ, dtype=jnp.int32)
    return {
        'x': x,
        'mem': mem,
        'positions': positions,
        'norm_mix': gain((L, D_MODEL)),
        'w_in': nrm((L, D_MODEL, IN_WIDTH), D_MODEL),
        'mla_q_a_norm': gain((L, MLA_Q_RANK)),
        'mla_w_q_up': nrm((L, MLA_Q_RANK, MLA_HEADS * MLA_QK_DIM), MLA_Q_RANK),
        'mla_kv_a_norm': gain((L, MLA_KV_RANK)),
        'mla_w_kv_up': nrm((L, MLA_KV_RANK, MLA_HEADS * (MLA_NOPE_DIM + MLA_V_DIM)), MLA_KV_RANK),
        'mla_q_norm': gain((L, MLA_QK_DIM)),
        'mla_k_norm': gain((L, MLA_QK_DIM)),
        'hg_lb_logits': 0.1 * jax.random.normal(nk(), (L + 1, 2, HG_FWIDTH), jnp.float32),
        'hg_o_norm': gain((L, HG_VAL_DIM)),
        'w_out': nrm((L, MIX_WIDTH, D_MODEL), MIX_WIDTH),
        'norm_cross': gain((L, D_MODEL)),
        'norm_mem': gain((L, D_MODEL)),
        'x_w_q': nrm((L, D_MODEL, X_HEADS * X_HEAD_DIM), D_MODEL),
        'x_w_kv': nrm((L, D_MODEL, 2 * X_HEADS * X_HEAD_DIM), D_MODEL),
        'x_q_norm': gain((L, X_HEAD_DIM)),
        'x_k_norm': gain((L, X_HEAD_DIM)),
        'x_w_o': nrm((L, X_HEADS * X_HEAD_DIM, D_MODEL), X_HEADS * X_HEAD_DIM),
        'norm_ffn': gain((L, D_MODEL)),
        'moe_w_group': nrm((L, D_MODEL, MOE_GROUPS), D_MODEL),
        'moe_b_group': 0.01 * jax.random.normal(nk(), (L, MOE_GROUPS), jnp.float32),
        'moe_w_expert': nrm((L, D_MODEL, MOE_EXPERTS), D_MODEL),
        'moe_b_expert': 0.01 * jax.random.normal(nk(), (L, MOE_EXPERTS), jnp.float32),
        'moe_w_gate': nrm((L, MOE_EXPERTS, D_MODEL, MOE_HIDDEN), D_MODEL),
        'moe_w_up': nrm((L, MOE_EXPERTS, D_MODEL, MOE_HIDDEN), D_MODEL),
        'moe_w_down': nrm((L, MOE_EXPERTS, MOE_HIDDEN, D_MODEL), MOE_HIDDEN),
    }


def reference(x, mem, positions, norm_mix, w_in, mla_q_a_norm, mla_w_q_up, mla_kv_a_norm, mla_w_kv_up,
              mla_q_norm, mla_k_norm, hg_lb_logits, hg_o_norm, w_out, norm_cross, norm_mem, x_w_q, x_w_kv,
              x_q_norm, x_k_norm, x_w_o, norm_ffn, moe_w_group, moe_b_group, moe_w_expert, moe_b_expert,
              moe_w_gate, moe_w_up, moe_w_down):
    cos, sin = rope_tables(positions)
    lb_table = jnp.cumsum(jax.nn.softmax(hg_lb_logits.astype(jnp.float32), axis=0), axis=0)
    for l in range(DEPTH):
        h = rms_norm(x, norm_mix[l])
        c_q, c_kv, k_rope, hq, hf_fwd, hf_bwd, hi, hg = split_columns(h @ w_in[l])
        a = mla_mixer(c_q, c_kv, k_rope, cos, sin, mla_q_a_norm[l], mla_w_q_up[l], mla_kv_a_norm[l],
                      mla_w_kv_up[l], mla_q_norm[l], mla_k_norm[l])
        r = hgrn2_mixer(hq, hf_fwd, hf_bwd, hi, hg, lb_table[l], hg_o_norm[l])
        x = x + jnp.concatenate([a, r], axis=-1) @ w_out[l]
        x = x + cross_attention(rms_norm(x, norm_cross[l]), rms_norm(mem, norm_mem[l]), x_w_q[l], x_w_kv[l],
                                x_q_norm[l], x_k_norm[l], x_w_o[l])
        x = x + hier_moe(rms_norm(x, norm_ffn[l]), moe_w_group[l], moe_b_group[l], moe_w_expert[l],
                         moe_b_expert[l], moe_w_gate[l], moe_w_up[l], moe_w_down[l])
    return x
```

```python
import functools
import math

import jax
import jax.numpy as jnp
from jax import lax
from jax.experimental import pallas as pl
from jax.experimental.pallas import tpu as pltpu

F32 = jnp.float32
BF16 = jnp.bfloat16

LANES = 128
SUBLANES = 8
VMEM_LIMIT_BYTES = 56 * 1024 * 1024

NORM_EPS = 1e-6
LOG2E = math.log2(math.e)

D_MODEL = 1024
MLA_HEADS = 8
MLA_NOPE = 64
MLA_ROPE = 32
MLA_QK = MLA_NOPE + MLA_ROPE
MLA_V = 64
MLA_Q_RANK = 192
MLA_Q_RANK_PAD = 256
MLA_KV_RANK = 128
ROPE_BASE = 10000.0
HG_HEADS = 4
HG_DIM = 128
HG_WIDTH = HG_HEADS * HG_DIM
HG_CHUNK = 128
HG_BAND = SUBLANES
X_HEADS = 4
X_HEAD_DIM = D_MODEL // X_HEADS
MOE_GROUPS = 8
MOE_PER_GROUP = 8
MOE_HIDDEN = 256
MOE_GROUP_HIDDEN = MOE_PER_GROUP * MOE_HIDDEN
IN_SIZES = (MLA_Q_RANK, MLA_KV_RANK, MLA_ROPE, HG_WIDTH, HG_WIDTH, HG_WIDTH, HG_WIDTH, HG_WIDTH)

NT_DIMS = (((1,), (1,)), ((), ()))
TN_DIMS = (((0,), (0,)), ((), ()))


def _rms(x, n):
    return x * lax.rsqrt(jnp.sum(x * x, axis=-1, keepdims=True) * (1.0 / n) + NORM_EPS)


def _silu(x):
    return x / (1.0 + jnp.exp(-x))


def _split3(x):
    a = x.astype(BF16)
    r = x - a.astype(F32)
    b = r.astype(BF16)
    c = (r - b.astype(F32)).astype(BF16)
    return a, b, c


def _dot(a, b):
    return jnp.dot(a, b, preferred_element_type=F32)


def _params(sem):
    return pltpu.CompilerParams(dimension_semantics=sem, vmem_limit_bytes=VMEM_LIMIT_BYTES)


def _inproj_kernel(x_ref, pos_ref, nmix_ref, w_ref, qan_ref, wq_ref, kvan_ref, wk_ref, wv_ref,
                   qn_ref, kn_ref, invf_ref,
                   q_out, k_out, v_out, hq_out, hff_out, hfb_out, hi_out, hg_out):
    x = x_ref[...]
    h = (_rms(x, D_MODEL) * nmix_ref[...]).astype(BF16)
    p = _dot(h, w_ref[...])
    c_q = p[:, 0:MLA_Q_RANK_PAD]
    cqn = (_rms(c_q, MLA_Q_RANK) * qan_ref[...]).astype(BF16)
    q = _dot(cqn, wq_ref[...])
    c_kv = p[:, 256:384]
    ckvn = (_rms(c_kv, MLA_KV_RANK) * kvan_ref[...]).astype(BF16)
    k_nope = _dot(ckvn, wk_ref[...])
    v_out[...] = _dot(ckvn, wv_ref[...]).astype(BF16)
    k_rope = p[:, 384:512]

    ang = pos_ref[...].astype(F32) * invf_ref[...]
    cos = jnp.cos(ang)
    sin = jnp.sin(ang)
    lane = lax.broadcasted_iota(jnp.int32, (1, LANES), 1)
    half = MLA_ROPE // 2
    sin_lo = jnp.where(lane < MLA_NOPE + half, -sin, 0.0)
    sin_hi = jnp.where(lane >= MLA_NOPE + half, sin, 0.0)

    def rope(t):
        return t * cos + pltpu.roll(t, LANES - half, 1) * sin_lo + pltpu.roll(t, half, 1) * sin_hi

    q_scale = MLA_QK ** -0.5 * LOG2E
    for hd in range(MLA_HEADS):
        sl = slice(hd * LANES, (hd + 1) * LANES)
        qh = _rms(q[:, sl], MLA_QK) * qn_ref[...]
        q_out[:, sl] = (rope(qh) * q_scale).astype(BF16)
        kh = _rms(k_nope[:, sl] + k_rope, MLA_QK) * kn_ref[...]
        k_out[:, sl] = rope(kh).astype(BF16)

    hq_out[...] = p[:, 512:1024].astype(BF16)
    hff_out[...] = p[:, 1024:1536]
    hfb_out[...] = p[:, 1536:2048]
    hi_out[...] = p[:, 2048:2560].astype(BF16)
    hg_out[...] = p[:, 2560:3072].astype(BF16)


def _inproj(x2d, pos2d, norm_mix, w_in, q_a_norm, w_q_up, kv_a_norm, w_kv_up, q_norm, k_norm, tm):
    T = x2d.shape[0]
    c0 = 0
    cols = []
    for size in IN_SIZES:
        cols.append(w_in[:, c0:c0 + size])
        c0 += size
    w_cq, w_ckv, w_kr, w_hq, w_hff, w_hfb, w_hi, w_hg = cols
    w_cq = jnp.pad(w_cq, ((0, 0), (0, MLA_Q_RANK_PAD - MLA_Q_RANK)))
    w_kr = jnp.pad(w_kr, ((0, 0), (MLA_NOPE, LANES - MLA_QK)))
    w_big = jnp.concatenate([w_cq, w_ckv, w_kr, w_hq, w_hff, w_hfb, w_hi, w_hg], axis=1).astype(BF16)
    n_big = w_big.shape[1]

    qan = jnp.pad(q_a_norm, (0, MLA_Q_RANK_PAD - MLA_Q_RANK)).reshape(1, -1)
    wq = w_q_up.reshape(MLA_Q_RANK, MLA_HEADS, MLA_QK)
    wq = jnp.pad(wq, ((0, MLA_Q_RANK_PAD - MLA_Q_RANK), (0, 0), (0, LANES - MLA_QK)))
    wq = wq.reshape(MLA_Q_RANK_PAD, MLA_HEADS * LANES).astype(BF16)
    wkv = w_kv_up.reshape(MLA_KV_RANK, MLA_HEADS, MLA_NOPE + MLA_V)
    wk = jnp.pad(wkv[:, :, :MLA_NOPE], ((0, 0), (0, 0), (0, LANES - MLA_NOPE)))
    wk = wk.reshape(MLA_KV_RANK, MLA_HEADS * LANES).astype(BF16)
    wv = wkv[:, :, MLA_NOPE:].reshape(MLA_KV_RANK, MLA_HEADS * MLA_V).astype(BF16)
    qn = jnp.pad(q_norm, (0, LANES - MLA_QK)).reshape(1, LANES)
    kn = jnp.pad(k_norm, (0, LANES - MLA_QK)).reshape(1, LANES)
    half = MLA_ROPE // 2
    inv_freq = 1.0 / (ROPE_BASE ** (jnp.arange(half, dtype=F32) / half))
    invf = jnp.concatenate([jnp.zeros((MLA_NOPE,), F32), inv_freq, inv_freq,
                            jnp.zeros((LANES - MLA_QK,), F32)]).reshape(1, LANES)

    def full(a):
        return pl.BlockSpec(a.shape, lambda i: (0,) * a.ndim)

    def rows(width):
        return pl.BlockSpec((tm, width), lambda i: (i, 0))

    nmix = norm_mix.reshape(1, -1)
    kvan = kv_a_norm.reshape(1, -1)
    qk_w = MLA_HEADS * LANES
    v_w = MLA_HEADS * MLA_V
    out_shape = (
        jax.ShapeDtypeStruct((T, qk_w), BF16), jax.ShapeDtypeStruct((T, qk_w), BF16),
        jax.ShapeDtypeStruct((T, v_w), BF16),
        jax.ShapeDtypeStruct((T, HG_WIDTH), BF16), jax.ShapeDtypeStruct((T, HG_WIDTH), F32),
        jax.ShapeDtypeStruct((T, HG_WIDTH), F32), jax.ShapeDtypeStruct((T, HG_WIDTH), BF16),
        jax.ShapeDtypeStruct((T, HG_WIDTH), BF16))
    return pl.pallas_call(
        _inproj_kernel,
        grid=(T // tm,),
        in_specs=[rows(D_MODEL), rows(1), full(nmix), full(w_big), full(qan), full(wq), full(kvan),
                  full(wk), full(wv), full(qn), full(kn), full(invf)],
        out_specs=(rows(qk_w), rows(qk_w), rows(v_w), rows(HG_WIDTH), rows(HG_WIDTH), rows(HG_WIDTH),
                   rows(HG_WIDTH), rows(HG_WIDTH)),
        out_shape=out_shape,
        compiler_params=_params(("parallel",)),
        name="inproj",
    )(x2d, pos2d, nmix, w_big, qan, wq, kvan, wk, wv, qn, kn, invf)


def _mla_kernel(q_ref, k_ref, v_ref, o_ref, s_scr, m_scr, l_scr, acc_scr, *, tk):
    tq = q_ref.shape[0]
    n_chunks = k_ref.shape[0] // tk
    nsub = tk // LANES
    m_scr[...] = jnp.full(m_scr.shape, -jnp.inf, F32)
    l_scr[...] = jnp.zeros(l_scr.shape, F32)
    acc_scr[...] = jnp.zeros(acc_scr.shape, F32)

    def scores(c, carry):
        start = pl.multiple_of(c * tk, tk)
        for j in range(2):
            qh = q_ref[:, j * LANES:(j + 1) * LANES]
            kh = k_ref[pl.ds(start, tk), j * LANES:(j + 1) * LANES]
            s = lax.dot_general(qh, kh, NT_DIMS, preferred_element_type=F32)
            s_scr[j, c] = s
            m = m_scr[j]
            for i in range(nsub):
                m = jnp.maximum(m, s[:, i * LANES:(i + 1) * LANES])
            m_scr[j] = m
        return carry

    lax.fori_loop(0, n_chunks, scores, 0)
    row_max = [jnp.max(m_scr[j], axis=-1, keepdims=True) for j in range(2)]

    def weighted(c, carry):
        start = pl.multiple_of(c * tk, tk)
        vv = v_ref[pl.ds(start, tk), :]
        for j in range(2):
            p = jnp.exp2(s_scr[j, c] - row_max[j])
            l = l_scr[j]
            for i in range(nsub):
                l = l + p[:, i * LANES:(i + 1) * LANES]
            l_scr[j] = l
            acc_scr[j] += _dot(p.astype(BF16), vv)
        return carry

    lax.fori_loop(0, n_chunks, weighted, 0)
    o0 = acc_scr[0] / jnp.sum(l_scr[0], axis=-1, keepdims=True)
    o1 = acc_scr[1] / jnp.sum(l_scr[1], axis=-1, keepdims=True)
    lane = lax.broadcasted_iota(jnp.int32, (1, LANES), 1)
    o_ref[...] = jnp.where(lane < MLA_V, o0, o1).astype(BF16)


def _mla_attention(q, k, v, B, S, tq, tk):
    T = B * S
    nq = S // tq
    pairs = MLA_HEADS // 2
    return pl.pallas_call(
        functools.partial(_mla_kernel, tk=tk),
        grid=(B, pairs, nq),
        in_specs=[pl.BlockSpec((tq, 2 * LANES), lambda b, h, i: (b * nq + i, h)),
                  pl.BlockSpec((S, 2 * LANES), lambda b, h, i: (b, h)),
                  pl.BlockSpec((S, 2 * MLA_V), lambda b, h, i: (b, h))],
        out_specs=pl.BlockSpec((tq, 2 * MLA_V), lambda b, h, i: (b * nq + i, h)),
        out_shape=jax.ShapeDtypeStruct((T, MLA_HEADS * MLA_V), BF16),
        scratch_shapes=[pltpu.VMEM((2, S // tk, tq, tk), F32), pltpu.VMEM((2, tq, LANES), F32),
                        pltpu.VMEM((2, tq, LANES), F32), pltpu.VMEM((2, tq, LANES), F32)],
        compiler_params=_params(("parallel", "parallel", "arbitrary")),
        name="mla_attention",
    )(q, k, v)


def _hgrn_chunk(q, z, v, lb_row, state, rev):
    C = HG_CHUNK
    row = lax.broadcasted_iota(jnp.int32, (C, 1), 0)
    col = lax.broadcasted_iota(jnp.int32, (1, C), 1)
    age = (C - 1 - row) if rev else row

    kk = (1.0 - lb_row) * (1.0 / (1.0 + jnp.exp(z)))
    f = 1.0 - kk
    g = jnp.log1p(-kk)

    tri = jnp.where((col >= row) if rev else (col <= row), 1.0, 0.0).astype(BF16)
    g1, g2, g3 = _split3(g)
    b = _dot(tri, g1) + _dot(tri, g2) + _dot(tri, g3)

    q_hat = (q * jnp.exp(b)).astype(BF16)
    o = lax.dot_general(q_hat, state.astype(BF16), NT_DIMS, preferred_element_type=F32)
    b_end = b[0:1, :] if rev else b[C - 1:C, :]
    k_hat = (kk * jnp.exp(b_end - b)).astype(BF16)
    new_state = state * jnp.exp(b_end) + lax.dot_general(v, k_hat, TN_DIMS, preferred_element_type=F32)

    same_band = (row // HG_BAND) == (col // HG_BAND)
    dist = jnp.where(same_band, (col - row) if rev else (row - col), -1)
    attn = jnp.where(dist == 0, jnp.sum(q * kk, axis=-1, keepdims=True), 0.0)
    w = f
    for d in range(1, HG_BAND):
        shift = (C - d) if rev else d
        a_d = jnp.sum(q * pltpu.roll(kk, shift, 0) * w, axis=-1, keepdims=True)
        attn = jnp.where(dist == d, a_d, attn)
        if d + 1 < HG_BAND:
            w = w * pltpu.roll(f, shift, 0)

    m = HG_BAND
    while m < C:
        nb = C // (2 * m)
        b3 = b.reshape(nb, 2 * m, HG_DIM)
        ref = b3[:, m:m + 1, :] if rev else b3[:, m - 1:m, :]
        e = jnp.exp(-jnp.abs(b3 - ref)).reshape(C, HG_DIM)
        is_query = (age % (2 * m)) >= m
        q_t = jnp.where(is_query, q * e, 0.0).astype(BF16)
        k_t = jnp.where(is_query, 0.0, kk * e).astype(BF16)
        a_m = lax.dot_general(q_t, k_t, NT_DIMS, preferred_element_type=F32)
        attn = attn + jnp.where((row // (2 * m)) == (col // (2 * m)), a_m, 0.0)
        m *= 2

    o = o + _dot(attn.astype(BF16), v)
    return o, new_state


def _hgrn_kernel(hq_ref, hff_ref, hfb_ref, hi_ref, hg_ref, lbl_ref, onorm_ref, out_ref, of_scr, ob_scr):
    C = HG_CHUNK
    n_chunks = hq_ref.shape[0] // C
    lg = lbl_ref[...]
    mx = jnp.maximum(lg[0], lg[1])
    e0 = jnp.exp(lg[0] - mx)
    lb = e0 / (e0 + jnp.exp(lg[1] - mx))

    def body(i, states):
        st_f, st_b = states
        sf = pl.multiple_of(i * C, C)
        sb = pl.multiple_of((n_chunks - 1 - i) * C, C)
        o_f, st_f = _hgrn_chunk(_silu(hq_ref[pl.ds(sf, C), :].astype(F32)), hff_ref[pl.ds(sf, C), :],
                                hi_ref[pl.ds(sf, C), :], lb[0:1, :], st_f, False)
        of_scr[pl.ds(sf, C), :] = o_f
        o_b, st_b = _hgrn_chunk(_silu(hq_ref[pl.ds(sb, C), :].astype(F32)), hfb_ref[pl.ds(sb, C), :],
                                hi_ref[pl.ds(sb, C), :], lb[1:2, :], st_b, True)
        ob_scr[pl.ds(sb, C), :] = o_b
        return st_f, st_b

    zero = jnp.zeros((HG_DIM, HG_DIM), F32)
    lax.fori_loop(0, n_chunks, body, (zero, zero))
    o = of_scr[...] + ob_scr[...]
    out_ref[...] = ((_rms(o, HG_DIM) * onorm_ref[...]).astype(BF16)
                    * _silu(hg_ref[...].astype(F32)).astype(BF16))


def _hgrn(hq, hff, hfb, hi, hg, lb_logits, o_norm, B, S):
    T = B * S
    blk = pl.BlockSpec((S, HG_DIM), lambda b, h: (b, h))
    n_layers = lb_logits.shape[0]
    return pl.pallas_call(
        _hgrn_kernel,
        grid=(B, HG_HEADS),
        in_specs=[blk, blk, blk, blk, blk,
                  pl.BlockSpec((n_layers, 2, HG_DIM), lambda b, h: (0, 0, h)),
                  pl.BlockSpec((1, HG_DIM), lambda b, h: (0, 0))],
        out_specs=blk,
        out_shape=jax.ShapeDtypeStruct((T, HG_WIDTH), BF16),
        scratch_shapes=[pltpu.VMEM((S, HG_DIM), F32), pltpu.VMEM((S, HG_DIM), F32)],
        compiler_params=_params(("parallel", "parallel")),
        name="hgrn2",
    )(hq, hff, hfb, hi, hg, lb_logits, o_norm.reshape(1, HG_DIM))


def _memkv_kernel(mem_ref, nmem_ref, wkv_ref, kn_ref, k_out, v_out):
    hm = (_rms(mem_ref[0], D_MODEL) * nmem_ref[...]).astype(BF16)
    kv = _dot(hm, wkv_ref[...])
    for hd in range(X_HEADS):
        sl = slice(hd * X_HEAD_DIM, (hd + 1) * X_HEAD_DIM)
        k_out[0, :, sl] = (_rms(kv[:, sl], X_HEAD_DIM) * kn_ref[...]).astype(BF16)
    v_out[0] = kv[:, D_MODEL:].astype(BF16)


def _memkv(mem, norm_mem, w_kv, k_norm):
    B, M, _ = mem.shape
    wkv = w_kv.astype(BF16)
    blk = pl.BlockSpec((1, M, D_MODEL), lambda b: (b, 0, 0))
    return pl.pallas_call(
        _memkv_kernel,
        grid=(B,),
        in_specs=[blk, pl.BlockSpec((1, D_MODEL), lambda b: (0, 0)),
                  pl.BlockSpec(wkv.shape, lambda b: (0, 0)),
                  pl.BlockSpec((1, X_HEAD_DIM), lambda b: (0, 0))],
        out_specs=(blk, blk),
        out_shape=(jax.ShapeDtypeStruct((B, M, D_MODEL), BF16), jax.ShapeDtypeStruct((B, M, D_MODEL), BF16)),
        compiler_params=_params(("parallel",)),
        name="mem_kv",
    )(mem, norm_mem.reshape(1, -1), wkv, k_norm.reshape(1, -1))


def _router_logits(h, w_hi_ref, w_lo_ref, bias_ref):
    h_hi = h.astype(BF16)
    h_lo = (h - h_hi.astype(F32)).astype(BF16)
    return _dot(h_hi, w_hi_ref) + _dot(h_hi, w_lo_ref) + _dot(h_lo, w_hi_ref) + bias_ref


def _first_max_lane(vals, lane_f):
    mx = jnp.max(vals, axis=-1, keepdims=True)
    idx = jnp.min(jnp.where(vals == mx, lane_f, float(LANES)), axis=-1, keepdims=True)
    return mx, idx


def _cross_kernel(x_ref, a_ref, r_ref, woa_ref, wor_ref, ncross_ref, wq_ref, qn_ref, kx_ref, vx_ref, wxo_ref,
                  nffn_ref, wg_hi_ref, wg_lo_ref, bg_ref, x2_out, gsel_out):
    x1 = x_ref[...] + _dot(a_ref[...], woa_ref[...]) + _dot(r_ref[...], wor_ref[...])
    hc = (_rms(x1, D_MODEL) * ncross_ref[...]).astype(BF16)
    qx = _dot(hc, wq_ref[...])
    q_scale = X_HEAD_DIM ** -0.5 * LOG2E
    heads = []
    for hd in range(X_HEADS):
        sl = slice(hd * X_HEAD_DIM, (hd + 1) * X_HEAD_DIM)
        qh = (_rms(qx[:, sl], X_HEAD_DIM) * qn_ref[...] * q_scale).astype(BF16)
        s = lax.dot_general(qh, kx_ref[0, :, sl], NT_DIMS, preferred_element_type=F32)
        p = jnp.exp2(s - jnp.max(s, axis=-1, keepdims=True))
        o = _dot(p.astype(BF16), vx_ref[0, :, sl]) / jnp.sum(p, axis=-1, keepdims=True)
        heads.append(o.astype(BF16))
    x2 = x1 + _dot(jnp.concatenate(heads, axis=-1), wxo_ref[...])
    x2_out[...] = x2

    h3 = _rms(x2, D_MODEL) * nffn_ref[...]
    logits = _router_logits(h3, wg_hi_ref[...], wg_lo_ref[...], bg_ref[...])
    lane = lax.broadcasted_iota(jnp.int32, (1, LANES), 1)
    lane_f = lane.astype(F32)
    _, g_idx = _first_max_lane(jnp.where(lane < MOE_GROUPS, logits, -jnp.inf), lane_f)
    gsel_out[...] = g_idx.astype(jnp.int32)


def _cross(x2d, a, r, w_out, norm_cross, w_q, q_norm, kx, vx, w_o, norm_ffn, w_group, b_group, B, S, tm):
    T = B * S
    per_b = S // tm
    M = kx.shape[1]
    woa = w_out[:MLA_HEADS * MLA_V].astype(BF16)
    wor = w_out[MLA_HEADS * MLA_V:].astype(BF16)
    wq = w_q.astype(BF16)
    wxo = w_o.astype(BF16)
    wg = jnp.pad(w_group, ((0, 0), (0, LANES - MOE_GROUPS)))
    wg_hi = wg.astype(BF16)
    wg_lo = (wg - wg_hi.astype(F32)).astype(BF16)
    bg = jnp.pad(b_group, (0, LANES - MOE_GROUPS)).reshape(1, LANES)

    def full(arr):
        return pl.BlockSpec(arr.shape, lambda i: (0,) * arr.ndim)

    def rows(width):
        return pl.BlockSpec((tm, width), lambda i: (i, 0))

    ncross = norm_cross.reshape(1, -1)
    qn = q_norm.reshape(1, -1)
    nffn = norm_ffn.reshape(1, -1)
    mem_blk = pl.BlockSpec((1, M, D_MODEL), lambda i: (i // per_b, 0, 0))
    return pl.pallas_call(
        _cross_kernel,
        grid=(T // tm,),
        in_specs=[rows(D_MODEL), rows(MLA_HEADS * MLA_V), rows(HG_WIDTH), full(woa), full(wor), full(ncross),
                  full(wq), full(qn), mem_blk, mem_blk, full(wxo), full(nffn), full(wg_hi), full(wg_lo), full(bg)],
        out_specs=(rows(D_MODEL), rows(1)),
        out_shape=(jax.ShapeDtypeStruct((T, D_MODEL), F32), jax.ShapeDtypeStruct((T, 1), jnp.int32)),
        compiler_params=_params(("parallel",)),
        name="cross",
    )(x2d, a, r, woa, wor, ncross, wq, qn, kx, vx, wxo, nffn, wg_hi, wg_lo, bg)


def _moe_kernel(grp_ref, nvalid_ref,
                src_ref, src_next_ref,
                x_hbm, nffn_ref, wr_hi_ref, wr_lo_ref, br_ref, wg_ref, wu_ref, wd_ref,
                out_hbm, xbuf, ybuf, gsem, ssem):
    i = pl.program_id(0)
    n_blocks = pl.num_programs(0)
    tb = xbuf.shape[1]
    slot = i % 2

    def gather(idx_ref, to_slot):
        def issue(r, carry):
            tok = idx_ref[0, 0, r]
            pltpu.make_async_copy(x_hbm.at[pl.ds(tok, 1)], xbuf.at[to_slot, pl.ds(r, 1)], gsem.at[to_slot]).start()
            return carry
        lax.fori_loop(0, tb, issue, 0)

    def wait_scatter(of_slot, count):
        whole = pl.multiple_of((count // SUBLANES) * SUBLANES, SUBLANES)

        @pl.when(whole > 0)
        def _():
            pltpu.make_async_copy(ybuf.at[of_slot, pl.ds(0, whole)], out_hbm.at[pl.ds(0, whole)],
                                  ssem.at[of_slot]).wait()

        def one(r, carry):
            pltpu.make_async_copy(ybuf.at[of_slot, pl.ds(0, 1)], out_hbm.at[pl.ds(0, 1)], ssem.at[of_slot]).wait()
            return carry
        lax.fori_loop(0, count - whole, one, 0)

    @pl.when(i == 0)
    def _():
        @pl.when(nvalid_ref[0] > 0)
        def _():
            gather(src_ref, 0)

    nxt = jnp.minimum(i + 1, n_blocks - 1)

    @pl.when(jnp.logical_and(i + 1 < n_blocks, nvalid_ref[nxt] > 0))
    def _():
        gather(src_next_ref, 1 - slot)

    prev2 = jnp.maximum(i - 2, 0)
    wait_scatter(slot, jnp.where(i >= 2, nvalid_ref[prev2], 0))

    n_valid = nvalid_ref[i]

    @pl.when(n_valid > 0)
    def _():
        pltpu.make_async_copy(x_hbm.at[pl.ds(0, tb)], xbuf.at[slot], gsem.at[slot]).wait()
        x = xbuf[slot]
        h = _rms(x, D_MODEL) * nffn_ref[...]
        logits = _router_logits(h, wr_hi_ref[0], wr_lo_ref[0], br_ref[0])
        lane = lax.broadcasted_iota(jnp.int32, (1, LANES), 1)
        lane_f = lane.astype(F32)
        is_group = lane < MOE_GROUPS
        is_expert = jnp.logical_and(lane >= MOE_GROUPS, lane < MOE_GROUPS + MOE_PER_GROUP)
        g_logits = jnp.where(is_group, logits, -jnp.inf)
        g_max = jnp.max(g_logits, axis=-1, keepdims=True)
        g_weight = 1.0 / jnp.sum(jnp.exp(g_logits - g_max), axis=-1, keepdims=True)
        e_logits = jnp.where(is_expert, logits, -jnp.inf)
        e1, i1 = _first_max_lane(e_logits, lane_f)
        e2, i2 = _first_max_lane(jnp.where(lane_f == i1, -jnp.inf, e_logits), lane_f)
        t = jnp.exp(e2 - e1)
        w1 = g_weight / (1.0 + t)
        w2 = g_weight * t / (1.0 + t)

        hb = h.astype(BF16)
        gate = _dot(hb, wg_ref[0])
        up = _dot(hb, wu_ref[0])
        pieces = []
        for e in range(MOE_PER_GROUP):
            sl = slice(e * MOE_HIDDEN, (e + 1) * MOE_HIDDEN)
            lane_e = float(MOE_GROUPS + e)
            w_e = jnp.where(i1 == lane_e, w1, w2)
            chosen = jnp.logical_or(i1 == lane_e, i2 == lane_e)
            act = _silu(gate[:, sl]) * up[:, sl] * w_e
            pieces.append(jnp.where(chosen, act, 0.0).astype(BF16))
        y = _dot(jnp.concatenate(pieces, axis=-1), wd_ref[0])
        ybuf[slot] = x + y

        def scatter(r, carry):
            tok = src_ref[0, 0, r]
            pltpu.make_async_copy(ybuf.at[slot, pl.ds(r, 1)], out_hbm.at[pl.ds(tok, 1)], ssem.at[slot]).start()
            return carry
        lax.fori_loop(0, n_valid, scatter, 0)

    @pl.when(i == n_blocks - 1)
    def _():
        wait_scatter(slot, n_valid)
        prev1 = jnp.maximum(i - 1, 0)
        wait_scatter(1 - slot, jnp.where(i >= 1, nvalid_ref[prev1], 0))


def _moe(x2, gsel, norm_ffn, w_group, b_group, w_expert, b_expert, w_gate, w_up, w_down, tb):
    T = x2.shape[0]
    G, E, H = MOE_GROUPS, MOE_PER_GROUP, MOE_HIDDEN
    n_blocks = T // tb + G

    g = gsel.reshape(T)
    order = jnp.argsort(g, stable=True).astype(jnp.int32)
    counts = jnp.sum(g[:, None] == jnp.arange(G, dtype=jnp.int32)[None, :], axis=0).astype(jnp.int32)
    blocks_per_group = (counts + tb - 1) // tb
    block_end = jnp.cumsum(blocks_per_group)
    block_start = block_end - blocks_per_group
    token_start = jnp.cumsum(counts) - counts
    bi = jnp.arange(n_blocks, dtype=jnp.int32)
    block_group = jnp.minimum(jnp.searchsorted(block_end, bi, side='right'), G - 1).astype(jnp.int32)
    first_row = (bi - block_start[block_group]) * tb
    n_valid = jnp.clip(counts[block_group] - first_row, 0, tb)
    n_valid = jnp.where(bi < block_end[G - 1], n_valid, 0).astype(jnp.int32)
    r = jnp.arange(tb, dtype=jnp.int32)[None, :]
    pos = token_start[block_group][:, None] + first_row[:, None] + r
    src = jnp.where(r < n_valid[:, None], order[jnp.clip(pos, 0, T - 1)], 0).astype(jnp.int32)
    src = src.reshape(n_blocks, 1, tb)

    w_e = w_expert.reshape(D_MODEL, G, E).transpose(1, 0, 2)
    w_r = jnp.concatenate([jnp.broadcast_to(w_group[None], (G, D_MODEL, G)), w_e,
                           jnp.zeros((G, D_MODEL, LANES - G - E), F32)], axis=-1)
    wr_hi = w_r.astype(BF16)
    wr_lo = (w_r - wr_hi.astype(F32)).astype(BF16)
    b_r = jnp.concatenate([jnp.broadcast_to(b_group[None], (G, G)), b_expert.reshape(G, E),
                           jnp.zeros((G, LANES - G - E), F32)], axis=-1).reshape(G, 1, LANES)
    wg = w_gate.reshape(G, E, D_MODEL, H).transpose(0, 2, 1, 3).reshape(G, D_MODEL, E * H).astype(BF16)
    wu = w_up.reshape(G, E, D_MODEL, H).transpose(0, 2, 1, 3).reshape(G, D_MODEL, E * H).astype(BF16)
    wd = w_down.reshape(G, E * H, D_MODEL).astype(BF16)
    nffn = norm_ffn.reshape(1, -1)

    def by_group(shape):
        return pl.BlockSpec((1,) + shape, lambda i, grp, nv: (grp[i], 0, 0))

    smem_rows = lambda f: pl.BlockSpec((1, 1, tb), f, memory_space=pltpu.SMEM)
    grid_spec = pltpu.PrefetchScalarGridSpec(
        num_scalar_prefetch=2,
        grid=(n_blocks,),
        in_specs=[smem_rows(lambda i, grp, nv: (i, 0, 0)),
                  smem_rows(lambda i, grp, nv: (jnp.minimum(i + 1, n_blocks - 1), 0, 0)),
                  pl.BlockSpec(memory_space=pl.ANY),
                  pl.BlockSpec((1, D_MODEL), lambda i, grp, nv: (0, 0)),
                  by_group((D_MODEL, LANES)), by_group((D_MODEL, LANES)), by_group((1, LANES)),
                  by_group((D_MODEL, E * H)), by_group((D_MODEL, E * H)), by_group((E * H, D_MODEL))],
        out_specs=pl.BlockSpec(memory_space=pl.ANY),
        scratch_shapes=[pltpu.VMEM((2, tb, D_MODEL), F32), pltpu.VMEM((2, tb, D_MODEL), F32),
                        pltpu.SemaphoreType.DMA((2,)), pltpu.SemaphoreType.DMA((2,))])
    return pl.pallas_call(
        _moe_kernel,
        grid_spec=grid_spec,
        out_shape=jax.ShapeDtypeStruct((T, D_MODEL), F32),
        compiler_params=_params(("arbitrary",)),
        name="experts",
    )(block_group, n_valid, src, src, x2, nffn, wr_hi, wr_lo, b_r, wg, wu, wd)


def _pick(n, pref):
    t = min(pref, n)
    while n % t:
        t //= 2
    return t


def kernel(x, mem, positions, norm_mix, w_in, mla_q_a_norm, mla_w_q_up, mla_kv_a_norm, mla_w_kv_up, mla_q_norm, mla_k_norm, hg_lb_logits, hg_o_norm, w_out, norm_cross, norm_mem, x_w_q, x_w_kv, x_q_norm, x_k_norm, x_w_o, norm_ffn, moe_w_group, moe_b_group, moe_w_expert, moe_b_expert, moe_w_gate, moe_w_up, moe_w_down):
    B, S, D = x.shape
    assert D == D_MODEL and w_in.shape[0] == 1 and S % HG_CHUNK == 0
    T = B * S
    x2d = x.reshape(T, D)
    pos2d = positions.reshape(T, 1).astype(jnp.int32)

    q, k, v, hq, hff, hfb, hi, hg = _inproj(
        x2d, pos2d, norm_mix[0], w_in[0], mla_q_a_norm[0], mla_w_q_up[0], mla_kv_a_norm[0], mla_w_kv_up[0],
        mla_q_norm[0], mla_k_norm[0], tm=_pick(T, 256))
    a = _mla_attention(q, k, v, B, S, tq=_pick(S, 512), tk=_pick(S, 512))
    r = _hgrn(hq, hff, hfb, hi, hg, hg_lb_logits, hg_o_norm[0], B, S)
    kx, vx = _memkv(mem, norm_mem[0], x_w_kv[0], x_k_norm[0])
    x2, gsel = _cross(x2d, a, r, w_out[0], norm_cross[0], x_w_q[0], x_q_norm[0], kx, vx, x_w_o[0], norm_ffn[0],
                      moe_w_group[0], moe_b_group[0], B, S, tm=_pick(S, 256))
    out = _moe(x2, gsel, norm_ffn[0], moe_w_group[0], moe_b_group[0], moe_w_expert[0], moe_b_expert[0],
               moe_w_gate[0], moe_w_up[0], moe_w_down[0], tb=_pick(T, 256))
    return out.reshape(B, S, D)
```

```python
import functools
import math

import jax
import jax.numpy as jnp
from jax import lax
from jax.experimental import pallas as pl
from jax.experimental.pallas import tpu as pltpu

F32 = jnp.float32
BF16 = jnp.bfloat16

LANES = 128
SUBLANES = 8
VMEM_LIMIT_BYTES = 56 * 1024 * 1024

NORM_EPS = 1e-6
LOG2E = math.log2(math.e)

D_MODEL = 1024
MLA_HEADS = 8
MLA_NOPE = 64
MLA_ROPE = 32
MLA_QK = MLA_NOPE + MLA_ROPE
MLA_V = 64
MLA_Q_RANK = 192
MLA_Q_RANK_PAD = 256
MLA_KV_RANK = 128
ROPE_BASE = 10000.0
CROSS_SUB_ROWS = 256
HG_HEADS = 4
HG_DIM = 128
HG_WIDTH = HG_HEADS * HG_DIM
HG_CHUNK = 128
HG_BAND = 4
HG_HEADS_PER_STEP = 2
X_HEADS = 4
X_HEAD_DIM = D_MODEL // X_HEADS
MOE_GROUPS = 8
MOE_PER_GROUP = 8
MOE_HIDDEN = 256
MOE_GROUP_HIDDEN = MOE_PER_GROUP * MOE_HIDDEN
IN_SIZES = (MLA_Q_RANK, MLA_KV_RANK, MLA_ROPE, HG_WIDTH, HG_WIDTH, HG_WIDTH, HG_WIDTH, HG_WIDTH)

NT_DIMS = (((1,), (1,)), ((), ()))
TN_DIMS = (((0,), (0,)), ((), ()))


def _rms(x, n):
    return x * lax.rsqrt(jnp.sum(x * x, axis=-1, keepdims=True) * (1.0 / n) + NORM_EPS)


def _silu(x):
    return x / (1.0 + jnp.exp(-x))


def _split3(x):
    a = x.astype(BF16)
    r = x - a.astype(F32)
    b = r.astype(BF16)
    c = (r - b.astype(F32)).astype(BF16)
    return a, b, c


def _dot(a, b):
    return jnp.dot(a, b, preferred_element_type=F32)


def _params(sem):
    return pltpu.CompilerParams(dimension_semantics=sem, vmem_limit_bytes=VMEM_LIMIT_BYTES)


def _inproj_kernel(x_ref, pos_ref, nmix_ref, w_ref, qan_ref, wq_ref, kvan_ref, wk_ref, wv_ref,
                   qn_ref, kn_ref, invf_ref,
                   q_out, k_out, v_out, hq_out, hff_out, hfb_out, hi_out, hg_out):
    h = (_rms(x_ref[...], D_MODEL) * nmix_ref[...]).astype(BF16)

    p = _dot(h, w_ref[:, 0:512])
    c_q = p[:, 0:MLA_Q_RANK_PAD]
    cqn = (_rms(c_q, MLA_Q_RANK) * qan_ref[...]).astype(BF16)
    q = _dot(cqn, wq_ref[...])
    c_kv = p[:, 256:384]
    ckvn = (_rms(c_kv, MLA_KV_RANK) * kvan_ref[...]).astype(BF16)
    k_nope = _dot(ckvn, wk_ref[...])
    v_out[...] = _dot(ckvn, wv_ref[...]).astype(BF16)
    k_rope = p[:, 384:512]

    ang = pos_ref[...].astype(F32) * invf_ref[...]
    cos = jnp.cos(ang)
    sin = jnp.sin(ang)
    lane = lax.broadcasted_iota(jnp.int32, (1, LANES), 1)
    half = MLA_ROPE // 2
    sin_lo = jnp.where(lane < MLA_NOPE + half, -sin, 0.0)
    sin_hi = jnp.where(lane >= MLA_NOPE + half, sin, 0.0)

    def rope(t):
        return t * cos + pltpu.roll(t, LANES - half, 1) * sin_lo + pltpu.roll(t, half, 1) * sin_hi

    q_scale = MLA_QK ** -0.5 * LOG2E
    mixer_outs = (hq_out, hff_out, hfb_out, hi_out, hg_out)
    for hd in range(MLA_HEADS):
        sl = slice(hd * LANES, (hd + 1) * LANES)
        qh = _rms(q[:, sl], MLA_QK) * qn_ref[...]
        q_out[:, sl] = (rope(qh) * q_scale).astype(BF16)
        kh = _rms(k_nope[:, sl] + k_rope, MLA_QK) * kn_ref[...]
        k_out[:, sl] = rope(kh).astype(BF16)
        if hd < len(mixer_outs):
            out = mixer_outs[hd]
            c0 = 512 + hd * HG_WIDTH
            out[...] = _dot(h, w_ref[:, c0:c0 + HG_WIDTH]).astype(out.dtype)


def _inproj(x2d, pos2d, norm_mix, w_in, q_a_norm, w_q_up, kv_a_norm, w_kv_up, q_norm, k_norm, tm):
    T = x2d.shape[0]
    c0 = 0
    cols = []
    for size in IN_SIZES:
        cols.append(w_in[:, c0:c0 + size])
        c0 += size
    w_cq, w_ckv, w_kr, w_hq, w_hff, w_hfb, w_hi, w_hg = cols
    w_cq = jnp.pad(w_cq, ((0, 0), (0, MLA_Q_RANK_PAD - MLA_Q_RANK)))
    w_kr = jnp.pad(w_kr, ((0, 0), (MLA_NOPE, LANES - MLA_QK)))
    w_big = jnp.concatenate([w_cq, w_ckv, w_kr, w_hq, w_hff, w_hfb, w_hi, w_hg], axis=1).astype(BF16)
    n_big = w_big.shape[1]

    qan = jnp.pad(q_a_norm, (0, MLA_Q_RANK_PAD - MLA_Q_RANK)).reshape(1, -1)
    wq = w_q_up.reshape(MLA_Q_RANK, MLA_HEADS, MLA_QK)
    wq = jnp.pad(wq, ((0, MLA_Q_RANK_PAD - MLA_Q_RANK), (0, 0), (0, LANES - MLA_QK)))
    wq = wq.reshape(MLA_Q_RANK_PAD, MLA_HEADS * LANES).astype(BF16)
    wkv = w_kv_up.reshape(MLA_KV_RANK, MLA_HEADS, MLA_NOPE + MLA_V)
    wk = jnp.pad(wkv[:, :, :MLA_NOPE], ((0, 0), (0, 0), (0, LANES - MLA_NOPE)))
    wk = wk.reshape(MLA_KV_RANK, MLA_HEADS * LANES).astype(BF16)
    wv = wkv[:, :, MLA_NOPE:].reshape(MLA_KV_RANK, MLA_HEADS * MLA_V).astype(BF16)
    qn = jnp.pad(q_norm, (0, LANES - MLA_QK)).reshape(1, LANES)
    kn = jnp.pad(k_norm, (0, LANES - MLA_QK)).reshape(1, LANES)
    half = MLA_ROPE // 2
    inv_freq = 1.0 / (ROPE_BASE ** (jnp.arange(half, dtype=F32) / half))
    invf = jnp.concatenate([jnp.zeros((MLA_NOPE,), F32), inv_freq, inv_freq,
                            jnp.zeros((LANES - MLA_QK,), F32)]).reshape(1, LANES)

    def full(a):
        return pl.BlockSpec(a.shape, lambda i: (0,) * a.ndim)

    def rows(width):
        return pl.BlockSpec((tm, width), lambda i: (i, 0))

    nmix = norm_mix.reshape(1, -1)
    kvan = kv_a_norm.reshape(1, -1)
    qk_w = MLA_HEADS * LANES
    v_w = MLA_HEADS * MLA_V
    out_shape = (
        jax.ShapeDtypeStruct((T, qk_w), BF16), jax.ShapeDtypeStruct((T, qk_w), BF16),
        jax.ShapeDtypeStruct((T, v_w), BF16),
        jax.ShapeDtypeStruct((T, HG_WIDTH), BF16), jax.ShapeDtypeStruct((T, HG_WIDTH), F32),
        jax.ShapeDtypeStruct((T, HG_WIDTH), F32), jax.ShapeDtypeStruct((T, HG_WIDTH), BF16),
        jax.ShapeDtypeStruct((T, HG_WIDTH), BF16))
    return pl.pallas_call(
        _inproj_kernel,
        grid=(T // tm,),
        in_specs=[rows(D_MODEL), rows(1), full(nmix), full(w_big), full(qan), full(wq), full(kvan),
                  full(wk), full(wv), full(qn), full(kn), full(invf)],
        out_specs=(rows(qk_w), rows(qk_w), rows(v_w), rows(HG_WIDTH), rows(HG_WIDTH), rows(HG_WIDTH),
                   rows(HG_WIDTH), rows(HG_WIDTH)),
        out_shape=out_shape,
        compiler_params=_params(("parallel",)),
        name="inproj",
    )(x2d, pos2d, nmix, w_big, qan, wq, kvan, wk, wv, qn, kn, invf)


def _mla_kernel(q_ref, k_ref, v_ref, o_ref, *, tk):
    n_chunks = k_ref.shape[0] // tk
    nsub = tk // LANES
    m = [None, None]
    l = [None, None]
    acc = [None, None]
    for c in range(n_chunks):
        rows = slice(c * tk, (c + 1) * tk)
        scores = [lax.dot_general(q_ref[:, j * LANES:(j + 1) * LANES], k_ref[rows, j * LANES:(j + 1) * LANES],
                                  NT_DIMS, preferred_element_type=F32) for j in range(2)]
        for j, s in enumerate(scores):
            blk_max = s[:, 0:LANES]
            for i in range(1, nsub):
                blk_max = jnp.maximum(blk_max, s[:, i * LANES:(i + 1) * LANES])
            m_new = jnp.max(blk_max, axis=-1, keepdims=True)
            if c > 0:
                m_new = jnp.maximum(m[j], m_new)
            p = jnp.exp2(s - m_new)
            blk_sum = p[:, 0:LANES]
            for i in range(1, nsub):
                blk_sum = blk_sum + p[:, i * LANES:(i + 1) * LANES]
            pv = _dot(p.astype(BF16), v_ref[rows, :])
            if c == 0:
                l[j], acc[j] = blk_sum, pv
            else:
                alpha = jnp.exp2(m[j] - m_new)
                l[j] = l[j] * alpha + blk_sum
                acc[j] = acc[j] * alpha + pv
            m[j] = m_new
    o0 = acc[0] / jnp.sum(l[0], axis=-1, keepdims=True)
    o1 = acc[1] / jnp.sum(l[1], axis=-1, keepdims=True)
    lane = lax.broadcasted_iota(jnp.int32, (1, LANES), 1)
    o_ref[...] = jnp.where(lane < MLA_V, o0, o1).astype(BF16)


def _mla_attention(q, k, v, B, S, tq, tk):
    T = B * S
    nq = S // tq
    pairs = MLA_HEADS // 2
    return pl.pallas_call(
        functools.partial(_mla_kernel, tk=tk),
        grid=(B, pairs, nq),
        in_specs=[pl.BlockSpec((tq, 2 * LANES), lambda b, h, i: (b * nq + i, h)),
                  pl.BlockSpec((S, 2 * LANES), lambda b, h, i: (b, h)),
                  pl.BlockSpec((S, 2 * MLA_V), lambda b, h, i: (b, h))],
        out_specs=pl.BlockSpec((tq, 2 * MLA_V), lambda b, h, i: (b * nq + i, h)),
        out_shape=jax.ShapeDtypeStruct((T, MLA_HEADS * MLA_V), BF16),
        compiler_params=_params(("parallel", "parallel", "arbitrary")),
        name="mla_attention",
    )(q, k, v)


def _hgrn_pair_codes(rev):
    C = HG_CHUNK
    row = lax.broadcasted_iota(jnp.int32, (C, C), 0)
    col = lax.broadcasted_iota(jnp.int32, (C, C), 1)
    dist = (col - row) if rev else (row - col)
    code = jnp.full((C, C), -1, jnp.int32)
    m, level = C // 2, HG_BAND
    levels = []
    while m >= HG_BAND:
        levels.append(m)
        m //= 2
    for j, m in enumerate(levels):
        same = (row // (2 * m)) == (col // (2 * m))
        code = jnp.where(same, HG_BAND + len(levels) - 1 - j, code)
    code = jnp.where((row // HG_BAND) == (col // HG_BAND), dist, code)
    return jnp.where(dist < 0, -1, code)


def _hgrn_chunks(chains):
    C = HG_CHUNK
    row = lax.broadcasted_iota(jnp.int32, (C, 1), 0)
    col = lax.broadcasted_iota(jnp.int32, (1, C), 1)
    n = len(chains)

    kks, fs, bs = [], [], []
    for q, z, v, lb_row, state, code, rev in chains:
        kk = (1.0 - lb_row) / (1.0 + jnp.exp(z))
        g = jnp.log1p(-kk) * LOG2E
        tri = jnp.where((col >= row) if rev else (col <= row), 1.0, 0.0).astype(BF16)
        g1, g2, g3 = _split3(g)
        kks.append(kk)
        fs.append(1.0 - kk)
        bs.append(_dot(tri, g1) + _dot(tri, g2) + _dot(tri, g3))

    outs, states = [], []
    for (q, z, v, lb_row, state, code, rev), kk, b in zip(chains, kks, bs):
        q_hat = (q * jnp.exp2(b)).astype(BF16)
        outs.append(lax.dot_general(q_hat, state.astype(BF16), NT_DIMS, preferred_element_type=F32))
        b_end = b[0:1, :] if rev else b[C - 1:C, :]
        k_hat = (kk * jnp.exp2(b_end - b)).astype(BF16)
        states.append(state * jnp.exp2(b_end) + lax.dot_general(v, k_hat, TN_DIMS, preferred_element_type=F32))

    attns = []
    for (q, z, v, lb_row, state, code, rev), kk, f in zip(chains, kks, fs):
        step = (C - 1) if rev else 1
        u = kk
        attn = jnp.where(code == 0, jnp.sum(q * u, axis=-1, keepdims=True), 0.0)
        for d in range(1, HG_BAND):
            u = f * pltpu.roll(u, step, 0)
            attn = jnp.where(code == d, jnp.sum(q * u, axis=-1, keepdims=True), attn)
        attns.append(attn)

    m, level = HG_BAND, HG_BAND
    while m < C:
        for idx in range(n):
            q, z, v, lb_row, state, code, rev = chains[idx]
            b3 = bs[idx].reshape(C // (2 * m), 2 * m, HG_DIM)
            ref = b3[:, m:m + 1, :] if rev else b3[:, m - 1:m, :]
            e = jnp.exp2(-jnp.abs(b3 - ref)).reshape(C, HG_DIM)
            a_m = lax.dot_general((q * e).astype(BF16), (kks[idx] * e).astype(BF16), NT_DIMS,
                                  preferred_element_type=F32)
            attns[idx] = jnp.where(code == level, a_m, attns[idx])
        m, level = 2 * m, level + 1

    return [(o + _dot(attn.astype(BF16), ch[2]), st) for o, attn, ch, st in zip(outs, attns, chains, states)]


def _hgrn_kernel(hq_ref, hff_ref, hfb_ref, hi_ref, hg_ref, lbl_ref, onorm_ref, out_ref,
                 q_scr, of_scr, ob_scr, code_scr):
    C = HG_CHUNK
    n_chunks = hq_ref.shape[0] // C
    lg = lbl_ref[...]
    mx = jnp.maximum(lg[0], lg[1])
    e0 = jnp.exp(lg[0] - mx)
    lb = e0 / (e0 + jnp.exp(lg[1] - mx))
    q_scr[...] = _silu(hq_ref[...].astype(F32))
    code_scr[0] = _hgrn_pair_codes(False)
    code_scr[1] = _hgrn_pair_codes(True)

    def body(i, states):
        sf = pl.multiple_of(i * C, C)
        sb = pl.multiple_of((n_chunks - 1 - i) * C, C)
        chains = []
        for hd in range(HG_HEADS_PER_STEP):
            sl = slice(hd * HG_DIM, (hd + 1) * HG_DIM)
            chains.append((q_scr[pl.ds(sf, C), sl], hff_ref[pl.ds(sf, C), sl], hi_ref[pl.ds(sf, C), sl],
                           lb[0:1, sl], states[hd][0], code_scr[0], False))
            chains.append((q_scr[pl.ds(sb, C), sl], hfb_ref[pl.ds(sb, C), sl], hi_ref[pl.ds(sb, C), sl],
                           lb[1:2, sl], states[hd][1], code_scr[1], True))
        results = _hgrn_chunks(chains)
        new_states = []
        for hd in range(HG_HEADS_PER_STEP):
            sl = slice(hd * HG_DIM, (hd + 1) * HG_DIM)
            (o_f, st_f), (o_b, st_b) = results[2 * hd], results[2 * hd + 1]
            of_scr[pl.ds(sf, C), sl] = o_f
            ob_scr[pl.ds(sb, C), sl] = o_b
            new_states.append((st_f, st_b))
        return tuple(new_states)

    zero = jnp.zeros((HG_DIM, HG_DIM), F32)
    lax.fori_loop(0, n_chunks, body, ((zero, zero),) * HG_HEADS_PER_STEP)
    for hd in range(HG_HEADS_PER_STEP):
        sl = slice(hd * HG_DIM, (hd + 1) * HG_DIM)
        o = of_scr[:, sl] + ob_scr[:, sl]
        out_ref[:, sl] = ((_rms(o, HG_DIM) * onorm_ref[...]).astype(BF16)
                          * _silu(hg_ref[:, sl].astype(F32)).astype(BF16))


def _hgrn(hq, hff, hfb, hi, hg, lb_logits, o_norm, B, S):
    T = B * S
    width = HG_HEADS_PER_STEP * HG_DIM
    blk = pl.BlockSpec((S, width), lambda b, h: (b, h))
    n_layers = lb_logits.shape[0]
    return pl.pallas_call(
        _hgrn_kernel,
        grid=(B, HG_HEADS // HG_HEADS_PER_STEP),
        in_specs=[blk, blk, blk, blk, blk,
                  pl.BlockSpec((n_layers, 2, width), lambda b, h: (0, 0, h)),
                  pl.BlockSpec((1, HG_DIM), lambda b, h: (0, 0))],
        out_specs=blk,
        out_shape=jax.ShapeDtypeStruct((T, HG_WIDTH), BF16),
        scratch_shapes=[pltpu.VMEM((S, width), F32), pltpu.VMEM((S, width), F32), pltpu.VMEM((S, width), F32),
                        pltpu.VMEM((2, HG_CHUNK, HG_CHUNK), jnp.int32)],
        compiler_params=_params(("parallel", "parallel")),
        name="hgrn2",
    )(hq, hff, hfb, hi, hg, lb_logits, o_norm.reshape(1, HG_DIM))


def _memkv_kernel(mem_ref, nmem_ref, wkv_ref, kn_ref, k_out, v_out):
    hm = (_rms(mem_ref[0], D_MODEL) * nmem_ref[...]).astype(BF16)
    kv = _dot(hm, wkv_ref[...])
    for hd in range(X_HEADS):
        sl = slice(hd * X_HEAD_DIM, (hd + 1) * X_HEAD_DIM)
        k_out[0, :, sl] = (_rms(kv[:, sl], X_HEAD_DIM) * kn_ref[...]).astype(BF16)
    v_out[0] = kv[:, D_MODEL:].astype(BF16)


def _memkv(mem, norm_mem, w_kv, k_norm):
    B, M, _ = mem.shape
    wkv = w_kv.astype(BF16)
    blk = pl.BlockSpec((1, M, D_MODEL), lambda b: (b, 0, 0))
    return pl.pallas_call(
        _memkv_kernel,
        grid=(B,),
        in_specs=[blk, pl.BlockSpec((1, D_MODEL), lambda b: (0, 0)),
                  pl.BlockSpec(wkv.shape, lambda b: (0, 0)),
                  pl.BlockSpec((1, X_HEAD_DIM), lambda b: (0, 0))],
        out_specs=(blk, blk),
        out_shape=(jax.ShapeDtypeStruct((B, M, D_MODEL), BF16), jax.ShapeDtypeStruct((B, M, D_MODEL), BF16)),
        compiler_params=_params(("parallel",)),
        name="mem_kv",
    )(mem, norm_mem.reshape(1, -1), wkv, k_norm.reshape(1, -1))


def _router_logits(h, w_ref, bias):
    h_hi = h.astype(BF16)
    h_lo = (h - h_hi.astype(F32)).astype(BF16)
    both = _dot(h_hi, w_ref[...])
    return both[:, :LANES] + both[:, LANES:] + _dot(h_lo, w_ref[:, :LANES]) + bias


def _split_hi_lo(w):
    hi = w.astype(BF16)
    return jnp.concatenate([hi, (w - hi.astype(F32)).astype(BF16)], axis=-1)


def _first_max_lane(vals, lane_f):
    mx = jnp.max(vals, axis=-1, keepdims=True)
    idx = jnp.min(jnp.where(vals == mx, lane_f, float(LANES)), axis=-1, keepdims=True)
    return mx, idx


def _cross_kernel(x_ref, a_ref, r_ref, woa_ref, wor_ref, ncross_ref, wq_ref, qn_ref, kx_ref, vx_ref, wxo_ref,
                  nffn_ref, wg_ref, bg_ref, x2_out, gsel_out):
    tm = x_ref.shape[0]
    blocks = [slice(r0, r0 + CROSS_SUB_ROWS) for r0 in range(0, tm, CROSS_SUB_ROWS)]
    q_scale = X_HEAD_DIM ** -0.5 * LOG2E
    lane = lax.broadcasted_iota(jnp.int32, (1, LANES), 1)
    lane_f = lane.astype(F32)

    x1s = [x_ref[rows, :] + _dot(a_ref[rows, :], woa_ref[...]) + _dot(r_ref[rows, :], wor_ref[...])
           for rows in blocks]
    qxs = [_dot((_rms(x1, D_MODEL) * ncross_ref[...]).astype(BF16), wq_ref[...]) for x1 in x1s]
    heads = [[] for _ in blocks]
    for hd in range(X_HEADS):
        sl = slice(hd * X_HEAD_DIM, (hd + 1) * X_HEAD_DIM)
        for n, qx in enumerate(qxs):
            qh = (_rms(qx[:, sl], X_HEAD_DIM) * qn_ref[...] * q_scale).astype(BF16)
            s = lax.dot_general(qh, kx_ref[0, :, sl], NT_DIMS, preferred_element_type=F32)
            p = jnp.exp2(s - jnp.max(s, axis=-1, keepdims=True))
            o = _dot(p.astype(BF16), vx_ref[0, :, sl]) / jnp.sum(p, axis=-1, keepdims=True)
            heads[n].append(o.astype(BF16))
    x2s = [x1 + _dot(jnp.concatenate(hs, axis=-1), wxo_ref[...]) for x1, hs in zip(x1s, heads)]
    for rows, x2 in zip(blocks, x2s):
        x2_out[rows, :] = x2
    for rows, x2 in zip(blocks, x2s):
        h3 = _rms(x2, D_MODEL) * nffn_ref[...]
        logits = _router_logits(h3, wg_ref, bg_ref[...])
        _, g_idx = _first_max_lane(jnp.where(lane < MOE_GROUPS, logits, -jnp.inf), lane_f)
        gsel_out[rows, :] = g_idx.astype(jnp.int32)


def _cross(x2d, a, r, w_out, norm_cross, w_q, q_norm, kx, vx, w_o, norm_ffn, w_group, b_group, B, S, tm):
    T = B * S
    per_b = S // tm
    M = kx.shape[1]
    woa = w_out[:MLA_HEADS * MLA_V].astype(BF16)
    wor = w_out[MLA_HEADS * MLA_V:].astype(BF16)
    wq = w_q.astype(BF16)
    wxo = w_o.astype(BF16)
    wg = _split_hi_lo(jnp.pad(w_group, ((0, 0), (0, LANES - MOE_GROUPS))))
    bg = jnp.pad(b_group, (0, LANES - MOE_GROUPS)).reshape(1, LANES)

    def full(arr):
        return pl.BlockSpec(arr.shape, lambda i: (0,) * arr.ndim)

    def rows(width):
        return pl.BlockSpec((tm, width), lambda i: (i, 0))

    ncross = norm_cross.reshape(1, -1)
    qn = q_norm.reshape(1, -1)
    nffn = norm_ffn.reshape(1, -1)
    mem_blk = pl.BlockSpec((1, M, D_MODEL), lambda i: (i // per_b, 0, 0))
    return pl.pallas_call(
        _cross_kernel,
        grid=(T // tm,),
        in_specs=[rows(D_MODEL), rows(MLA_HEADS * MLA_V), rows(HG_WIDTH), full(woa), full(wor), full(ncross),
                  full(wq), full(qn), mem_blk, mem_blk, full(wxo), full(nffn), full(wg), full(bg)],
        out_specs=(rows(D_MODEL), rows(1)),
        out_shape=(jax.ShapeDtypeStruct((T, D_MODEL), F32), jax.ShapeDtypeStruct((T, 1), jnp.int32)),
        compiler_params=_params(("parallel",)),
        name="cross",
    )(x2d, a, r, woa, wor, ncross, wq, qn, kx, vx, wxo, nffn, wg, bg)


def _moe_kernel(grp_ref, nvalid_ref,
                src_ref, src_next_ref, src_prev_ref,
                x_hbm, nffn_ref, wr_ref, br_ref, wg_ref, wu_ref, wd_ref,
                out_hbm, xbuf, ybuf, gsem, ssem):
    i = pl.program_id(0)
    n_blocks = pl.num_programs(0)
    tb = xbuf.shape[1]
    slot = i % 2
    other = 1 - slot

    def n_valid_at(j):
        inside = jnp.logical_and(j >= 0, j < n_blocks)
        return jnp.where(inside, nvalid_ref[jnp.clip(j, 0, n_blocks - 1)], 0)

    n_valid = nvalid_ref[i]
    n_prev = n_valid_at(i - 1)
    n_next = n_valid_at(i + 1)

    def gather_row(idx_ref, r, to_slot):
        tok = idx_ref[0, 0, r]
        pltpu.make_async_copy(x_hbm.at[pl.ds(tok, 1)], xbuf.at[to_slot, pl.ds(r, 1)], gsem.at[to_slot]).start()

    def scatter_row(idx_ref, r, from_slot):
        tok = idx_ref[0, 0, r]
        pltpu.make_async_copy(ybuf.at[from_slot, pl.ds(r, 1)], out_hbm.at[pl.ds(tok, 1)], ssem.at[from_slot]).start()

    def gather_all(idx_ref, to_slot):
        def body(r, carry):
            gather_row(idx_ref, r, to_slot)
            return carry
        lax.fori_loop(0, tb, body, 0)

    def scatter_some(idx_ref, from_slot, count):
        def body(r, carry):
            scatter_row(idx_ref, r, from_slot)
            return carry
        lax.fori_loop(0, count, body, 0)

    def wait_scatter(of_slot, count):
        whole = pl.multiple_of((count // SUBLANES) * SUBLANES, SUBLANES)

        @pl.when(whole > 0)
        def _():
            pltpu.make_async_copy(ybuf.at[of_slot, pl.ds(0, whole)], out_hbm.at[pl.ds(0, whole)],
                                  ssem.at[of_slot]).wait()

        def one(r, carry):
            pltpu.make_async_copy(ybuf.at[of_slot, pl.ds(0, 1)], out_hbm.at[pl.ds(0, 1)], ssem.at[of_slot]).wait()
            return carry
        lax.fori_loop(0, count - whole, one, 0)

    def evaluate(side_work):
        pltpu.make_async_copy(x_hbm.at[pl.ds(0, tb)], xbuf.at[slot], gsem.at[slot]).wait()
        x = xbuf[slot]
        h = _rms(x, D_MODEL) * nffn_ref[...]
        logits = _router_logits(h, wr_ref.at[0], br_ref[0])
        lane = lax.broadcasted_iota(jnp.int32, (1, LANES), 1)
        lane_f = lane.astype(F32)
        is_group = lane < MOE_GROUPS
        is_expert = jnp.logical_and(lane >= MOE_GROUPS, lane < MOE_GROUPS + MOE_PER_GROUP)
        g_logits = jnp.where(is_group, logits, -jnp.inf)
        g_max = jnp.max(g_logits, axis=-1, keepdims=True)
        g_weight = 1.0 / jnp.sum(jnp.exp(g_logits - g_max), axis=-1, keepdims=True)
        e_logits = jnp.where(is_expert, logits, -jnp.inf)
        e1, i1 = _first_max_lane(e_logits, lane_f)
        e2, i2 = _first_max_lane(jnp.where(lane_f == i1, -jnp.inf, e_logits), lane_f)
        t = jnp.exp(e2 - e1)
        w1 = g_weight / (1.0 + t)
        w2 = g_weight * t / (1.0 + t)

        hb = h.astype(BF16)
        pieces = []
        for e in range(MOE_PER_GROUP):
            side_work(e)
            sl = slice(e * MOE_HIDDEN, (e + 1) * MOE_HIDDEN)
            gate = _dot(hb, wg_ref[0, :, sl])
            up = _dot(hb, wu_ref[0, :, sl])
            lane_e = float(MOE_GROUPS + e)
            w_e = jnp.where(i1 == lane_e, w1, w2)
            chosen = jnp.logical_or(i1 == lane_e, i2 == lane_e)
            pieces.append(jnp.where(chosen, _silu(gate) * up * w_e, 0.0).astype(BF16))
        y = _dot(jnp.concatenate(pieces, axis=-1), wd_ref[0])
        ybuf[slot] = x + y

    @pl.when(jnp.logical_and(i == 0, n_valid > 0))
    def _():
        gather_all(src_ref, 0)

    wait_scatter(slot, n_valid_at(i - 2))

    steady = jnp.logical_and(jnp.logical_and(n_valid > 0, n_next > 0), n_prev == tb)
    share = tb // MOE_PER_GROUP

    @pl.when(steady)
    def _():
        def side_work(e):
            for r in range(e * share, (e + 1) * share):
                gather_row(src_next_ref, r, other)
                scatter_row(src_prev_ref, r, other)
        evaluate(side_work)

    @pl.when(jnp.logical_not(steady))
    def _():
        @pl.when(n_next > 0)
        def _():
            gather_all(src_next_ref, other)
        scatter_some(src_prev_ref, other, n_prev)

        @pl.when(n_valid > 0)
        def _():
            evaluate(lambda e: None)

    @pl.when(i == n_blocks - 1)
    def _():
        wait_scatter(other, n_prev)
        scatter_some(src_ref, slot, n_valid)
        wait_scatter(slot, n_valid)


def _moe(x2, gsel, norm_ffn, w_group, b_group, w_expert, b_expert, w_gate, w_up, w_down, tb):
    T = x2.shape[0]
    G, E, H = MOE_GROUPS, MOE_PER_GROUP, MOE_HIDDEN
    n_blocks = T // tb + G

    g = gsel.reshape(T)
    order = jnp.argsort(g, stable=True).astype(jnp.int32)
    counts = jnp.sum(g[:, None] == jnp.arange(G, dtype=jnp.int32)[None, :], axis=0).astype(jnp.int32)
    blocks_per_group = (counts + tb - 1) // tb
    block_end = jnp.cumsum(blocks_per_group)
    block_start = block_end - blocks_per_group
    token_start = jnp.cumsum(counts) - counts
    bi = jnp.arange(n_blocks, dtype=jnp.int32)
    block_group = jnp.minimum(jnp.searchsorted(block_end, bi, side='right'), G - 1).astype(jnp.int32)
    first_row = (bi - block_start[block_group]) * tb
    n_valid = jnp.clip(counts[block_group] - first_row, 0, tb)
    n_valid = jnp.where(bi < block_end[G - 1], n_valid, 0).astype(jnp.int32)
    r = jnp.arange(tb, dtype=jnp.int32)[None, :]
    pos = token_start[block_group][:, None] + first_row[:, None] + r
    src = jnp.where(r < n_valid[:, None], order[jnp.clip(pos, 0, T - 1)], 0).astype(jnp.int32)
    src = src.reshape(n_blocks, 1, tb)

    w_e = w_expert.reshape(D_MODEL, G, E).transpose(1, 0, 2)
    w_r = jnp.concatenate([jnp.broadcast_to(w_group[None], (G, D_MODEL, G)), w_e,
                           jnp.zeros((G, D_MODEL, LANES - G - E), F32)], axis=-1)
    wr = _split_hi_lo(w_r)
    b_r = jnp.concatenate([jnp.broadcast_to(b_group[None], (G, G)), b_expert.reshape(G, E),
                           jnp.zeros((G, LANES - G - E), F32)], axis=-1).reshape(G, 1, LANES)
    wg = w_gate.reshape(G, E, D_MODEL, H).transpose(0, 2, 1, 3).reshape(G, D_MODEL, E * H).astype(BF16)
    wu = w_up.reshape(G, E, D_MODEL, H).transpose(0, 2, 1, 3).reshape(G, D_MODEL, E * H).astype(BF16)
    wd = w_down.reshape(G, E * H, D_MODEL).astype(BF16)
    nffn = norm_ffn.reshape(1, -1)

    def by_group(shape):
        return pl.BlockSpec((1,) + shape, lambda i, grp, nv: (grp[i], 0, 0))

    smem_rows = lambda f: pl.BlockSpec((1, 1, tb), f, memory_space=pltpu.SMEM)
    grid_spec = pltpu.PrefetchScalarGridSpec(
        num_scalar_prefetch=2,
        grid=(n_blocks,),
        in_specs=[smem_rows(lambda i, grp, nv: (i, 0, 0)),
                  smem_rows(lambda i, grp, nv: (jnp.minimum(i + 1, n_blocks - 1), 0, 0)),
                  smem_rows(lambda i, grp, nv: (jnp.maximum(i - 1, 0), 0, 0)),
                  pl.BlockSpec(memory_space=pl.ANY),
                  pl.BlockSpec((1, D_MODEL), lambda i, grp, nv: (0, 0)),
                  by_group((D_MODEL, 2 * LANES)), by_group((1, LANES)),
                  by_group((D_MODEL, E * H)), by_group((D_MODEL, E * H)), by_group((E * H, D_MODEL))],
        out_specs=pl.BlockSpec(memory_space=pl.ANY),
        scratch_shapes=[pltpu.VMEM((2, tb, D_MODEL), F32), pltpu.VMEM((2, tb, D_MODEL), F32),
                        pltpu.SemaphoreType.DMA((2,)), pltpu.SemaphoreType.DMA((2,))])
    return pl.pallas_call(
        _moe_kernel,
        grid_spec=grid_spec,
        out_shape=jax.ShapeDtypeStruct((T, D_MODEL), F32),
        compiler_params=_params(("arbitrary",)),
        name="experts",
    )(block_group, n_valid, src, src, src, x2, nffn, wr, b_r, wg, wu, wd)


def _pick(n, pref):
    t = min(pref, n)
    while n % t:
        t //= 2
    return t


def kernel(x, mem, positions, norm_mix, w_in, mla_q_a_norm, mla_w_q_up, mla_kv_a_norm, mla_w_kv_up, mla_q_norm, mla_k_norm, hg_lb_logits, hg_o_norm, w_out, norm_cross, norm_mem, x_w_q, x_w_kv, x_q_norm, x_k_norm, x_w_o, norm_ffn, moe_w_group, moe_b_group, moe_w_expert, moe_b_expert, moe_w_gate, moe_w_up, moe_w_down):
    B, S, D = x.shape
    assert D == D_MODEL and w_in.shape[0] == 1 and S % HG_CHUNK == 0
    T = B * S
    x2d = x.reshape(T, D)
    pos2d = positions.reshape(T, 1).astype(jnp.int32)

    q, k, v, hq, hff, hfb, hi, hg = _inproj(
        x2d, pos2d, norm_mix[0], w_in[0], mla_q_a_norm[0], mla_w_q_up[0], mla_kv_a_norm[0], mla_w_kv_up[0],
        mla_q_norm[0], mla_k_norm[0], tm=_pick(T, 256))
    a = _mla_attention(q, k, v, B, S, tq=_pick(S, 512), tk=_pick(S, 1024))
    r = _hgrn(hq, hff, hfb, hi, hg, hg_lb_logits, hg_o_norm[0], B, S)
    kx, vx = _memkv(mem, norm_mem[0], x_w_kv[0], x_k_norm[0])
    x2, gsel = _cross(x2d, a, r, w_out[0], norm_cross[0], x_w_q[0], x_q_norm[0], kx, vx, x_w_o[0], norm_ffn[0],
                      moe_w_group[0], moe_b_group[0], B, S, tm=_pick(S, 512))
    out = _moe(x2, gsel, norm_ffn[0], moe_w_group[0], moe_b_group[0], moe_w_expert[0], moe_b_expert[0],
               moe_w_gate[0], moe_w_up[0], moe_w_down[0], tb=_pick(T, 256))
    return out.reshape(B, S, D)
```

```python
import functools
import math

import jax
import jax.numpy as jnp
from jax import lax
from jax.experimental import pallas as pl
from jax.experimental.pallas import tpu as pltpu

F32 = jnp.float32
BF16 = jnp.bfloat16

LANES = 128
SUBLANES = 8
VMEM_LIMIT_BYTES = 56 * 1024 * 1024

NORM_EPS = 1e-6
LOG2E = math.log2(math.e)

D_MODEL = 1024
MLA_HEADS = 8
MLA_NOPE = 64
MLA_ROPE = 32
MLA_QK = MLA_NOPE + MLA_ROPE
MLA_V = 64
MLA_Q_RANK = 192
MLA_Q_RANK_PAD = 256
MLA_KV_RANK = 128
ROPE_BASE = 10000.0
CROSS_SUB_ROWS = 256
HG_HEADS = 4
HG_DIM = 128
HG_WIDTH = HG_HEADS * HG_DIM
HG_CHUNK = 128
HG_BAND = 4
HG_HEADS_PER_STEP = 2
X_HEADS = 4
X_HEAD_DIM = D_MODEL // X_HEADS
MOE_GROUPS = 8
MOE_PER_GROUP = 8
MOE_HIDDEN = 256
MOE_GROUP_HIDDEN = MOE_PER_GROUP * MOE_HIDDEN
MOE_SLOTS = 3
IN_SIZES = (MLA_Q_RANK, MLA_KV_RANK, MLA_ROPE, HG_WIDTH, HG_WIDTH, HG_WIDTH, HG_WIDTH, HG_WIDTH)

NT_DIMS = (((1,), (1,)), ((), ()))
TN_DIMS = (((0,), (0,)), ((), ()))


def _rms(x, n):
    return x * lax.rsqrt(jnp.sum(x * x, axis=-1, keepdims=True) * (1.0 / n) + NORM_EPS)


def _silu(x):
    return x / (1.0 + jnp.exp(-x))


def _split3(x):
    a = x.astype(BF16)
    r = x - a.astype(F32)
    b = r.astype(BF16)
    c = (r - b.astype(F32)).astype(BF16)
    return a, b, c


def _dot(a, b):
    return jnp.dot(a, b, preferred_element_type=F32)


def _params(sem):
    return pltpu.CompilerParams(dimension_semantics=sem, vmem_limit_bytes=VMEM_LIMIT_BYTES)


def _inproj_kernel(x_ref, pos_ref, nmix_ref, w_ref, qan_ref, wq_ref, kvan_ref, wk_ref, wv_ref,
                   qn_ref, kn_ref, invf_ref,
                   q_out, k_out, v_out, hq_out, hff_out, hfb_out, hi_out, hg_out):
    h = (_rms(x_ref[...], D_MODEL) * nmix_ref[...]).astype(BF16)

    p = _dot(h, w_ref[:, 0:512])
    c_q = p[:, 0:MLA_Q_RANK_PAD]
    cqn = (_rms(c_q, MLA_Q_RANK) * qan_ref[...]).astype(BF16)
    q = _dot(cqn, wq_ref[...])
    c_kv = p[:, 256:384]
    ckvn = (_rms(c_kv, MLA_KV_RANK) * kvan_ref[...]).astype(BF16)
    k_nope = _dot(ckvn, wk_ref[...])
    v_out[...] = _dot(ckvn, wv_ref[...]).astype(BF16)
    k_rope = p[:, 384:512]

    ang = pos_ref[...].astype(F32) * invf_ref[...]
    cos = jnp.cos(ang)
    sin = jnp.sin(ang)
    lane = lax.broadcasted_iota(jnp.int32, (1, LANES), 1)
    half = MLA_ROPE // 2
    sin_lo = jnp.where(lane < MLA_NOPE + half, -sin, 0.0)
    sin_hi = jnp.where(lane >= MLA_NOPE + half, sin, 0.0)

    def rope(t):
        return t * cos + pltpu.roll(t, LANES - half, 1) * sin_lo + pltpu.roll(t, half, 1) * sin_hi

    q_scale = MLA_QK ** -0.5 * LOG2E
    mixer_outs = (hq_out, hff_out, hfb_out, hi_out, hg_out)
    for hd in range(MLA_HEADS):
        sl = slice(hd * LANES, (hd + 1) * LANES)
        qh = _rms(q[:, sl], MLA_QK) * qn_ref[...]
        q_out[:, sl] = (rope(qh) * q_scale).astype(BF16)
        kh = _rms(k_nope[:, sl] + k_rope, MLA_QK) * kn_ref[...]
        k_out[:, sl] = rope(kh).astype(BF16)
        if hd < len(mixer_outs):
            out = mixer_outs[hd]
            c0 = 512 + hd * HG_WIDTH
            out[...] = _dot(h, w_ref[:, c0:c0 + HG_WIDTH]).astype(out.dtype)


def _inproj(x2d, pos2d, norm_mix, w_in, q_a_norm, w_q_up, kv_a_norm, w_kv_up, q_norm, k_norm, tm):
    T = x2d.shape[0]
    c0 = 0
    cols = []
    for size in IN_SIZES:
        cols.append(w_in[:, c0:c0 + size])
        c0 += size
    w_cq, w_ckv, w_kr, w_hq, w_hff, w_hfb, w_hi, w_hg = cols
    w_cq = jnp.pad(w_cq, ((0, 0), (0, MLA_Q_RANK_PAD - MLA_Q_RANK)))
    w_kr = jnp.pad(w_kr, ((0, 0), (MLA_NOPE, LANES - MLA_QK)))
    w_big = jnp.concatenate([w_cq, w_ckv, w_kr, w_hq, w_hff, w_hfb, w_hi, w_hg], axis=1).astype(BF16)
    n_big = w_big.shape[1]

    qan = jnp.pad(q_a_norm, (0, MLA_Q_RANK_PAD - MLA_Q_RANK)).reshape(1, -1)
    wq = w_q_up.reshape(MLA_Q_RANK, MLA_HEADS, MLA_QK)
    wq = jnp.pad(wq, ((0, MLA_Q_RANK_PAD - MLA_Q_RANK), (0, 0), (0, LANES - MLA_QK)))
    wq = wq.reshape(MLA_Q_RANK_PAD, MLA_HEADS * LANES).astype(BF16)
    wkv = w_kv_up.reshape(MLA_KV_RANK, MLA_HEADS, MLA_NOPE + MLA_V)
    wk = jnp.pad(wkv[:, :, :MLA_NOPE], ((0, 0), (0, 0), (0, LANES - MLA_NOPE)))
    wk = wk.reshape(MLA_KV_RANK, MLA_HEADS * LANES).astype(BF16)
    wv = wkv[:, :, MLA_NOPE:].reshape(MLA_KV_RANK, MLA_HEADS * MLA_V).astype(BF16)
    qn = jnp.pad(q_norm, (0, LANES - MLA_QK)).reshape(1, LANES)
    kn = jnp.pad(k_norm, (0, LANES - MLA_QK)).reshape(1, LANES)
    half = MLA_ROPE // 2
    inv_freq = 1.0 / (ROPE_BASE ** (jnp.arange(half, dtype=F32) / half))
    invf = jnp.concatenate([jnp.zeros((MLA_NOPE,), F32), inv_freq, inv_freq,
                            jnp.zeros((LANES - MLA_QK,), F32)]).reshape(1, LANES)

    def full(a):
        return pl.BlockSpec(a.shape, lambda i: (0,) * a.ndim)

    def rows(width):
        return pl.BlockSpec((tm, width), lambda i: (i, 0))

    nmix = norm_mix.reshape(1, -1)
    kvan = kv_a_norm.reshape(1, -1)
    qk_w = MLA_HEADS * LANES
    v_w = MLA_HEADS * MLA_V
    out_shape = (
        jax.ShapeDtypeStruct((T, qk_w), BF16), jax.ShapeDtypeStruct((T, qk_w), BF16),
        jax.ShapeDtypeStruct((T, v_w), BF16),
        jax.ShapeDtypeStruct((T, HG_WIDTH), BF16), jax.ShapeDtypeStruct((T, HG_WIDTH), F32),
        jax.ShapeDtypeStruct((T, HG_WIDTH), F32), jax.ShapeDtypeStruct((T, HG_WIDTH), BF16),
        jax.ShapeDtypeStruct((T, HG_WIDTH), BF16))
    return pl.pallas_call(
        _inproj_kernel,
        grid=(T // tm,),
        in_specs=[rows(D_MODEL), rows(1), full(nmix), full(w_big), full(qan), full(wq), full(kvan),
                  full(wk), full(wv), full(qn), full(kn), full(invf)],
        out_specs=(rows(qk_w), rows(qk_w), rows(v_w), rows(HG_WIDTH), rows(HG_WIDTH), rows(HG_WIDTH),
                   rows(HG_WIDTH), rows(HG_WIDTH)),
        out_shape=out_shape,
        compiler_params=_params(("parallel",)),
        name="inproj",
    )(x2d, pos2d, nmix, w_big, qan, wq, kvan, wk, wv, qn, kn, invf)


def _mla_kernel(q_ref, k_ref, v_ref, o_ref, *, tk):
    n_chunks = k_ref.shape[0] // tk
    nsub = tk // LANES
    lane = lax.broadcasted_iota(jnp.int32, (1, LANES), 1)
    own = [lane < MLA_V, lane >= MLA_V]
    m = [None, None]
    acc = [None, None]
    for c in range(n_chunks):
        rows = slice(c * tk, (c + 1) * tk)
        scores = [lax.dot_general(q_ref[:, j * LANES:(j + 1) * LANES], k_ref[rows, j * LANES:(j + 1) * LANES],
                                  NT_DIMS, preferred_element_type=F32) for j in range(2)]
        vv = v_ref[rows, :]
        for j, s in enumerate(scores):
            blk_max = s[:, 0:LANES]
            for i in range(1, nsub):
                blk_max = jnp.maximum(blk_max, s[:, i * LANES:(i + 1) * LANES])
            m_new = jnp.max(blk_max, axis=-1, keepdims=True)
            if c > 0:
                m_new = jnp.maximum(m[j], m_new)
            p = jnp.exp2((s - m_new).astype(BF16))
            pv = _dot(p, jnp.where(own[j], vv, jnp.ones_like(vv)))
            acc[j] = pv if c == 0 else acc[j] * jnp.exp2(m[j] - m_new) + pv
            m[j] = m_new
    o0 = acc[0] / acc[0][:, MLA_V:MLA_V + 1]
    o1 = acc[1] / acc[1][:, 0:1]
    o_ref[...] = jnp.where(own[0], o0, o1).astype(BF16)


def _mla_attention(q, k, v, B, S, tq, tk):
    T = B * S
    nq = S // tq
    pairs = MLA_HEADS // 2
    return pl.pallas_call(
        functools.partial(_mla_kernel, tk=tk),
        grid=(B, pairs, nq),
        in_specs=[pl.BlockSpec((tq, 2 * LANES), lambda b, h, i: (b * nq + i, h)),
                  pl.BlockSpec((S, 2 * LANES), lambda b, h, i: (b, h)),
                  pl.BlockSpec((S, 2 * MLA_V), lambda b, h, i: (b, h))],
        out_specs=pl.BlockSpec((tq, 2 * MLA_V), lambda b, h, i: (b * nq + i, h)),
        out_shape=jax.ShapeDtypeStruct((T, MLA_HEADS * MLA_V), BF16),
        compiler_params=_params(("parallel", "parallel", "arbitrary")),
        name="mla_attention",
    )(q, k, v)


def _hgrn_pair_codes(rev):
    C = HG_CHUNK
    row = lax.broadcasted_iota(jnp.int32, (C, C), 0)
    col = lax.broadcasted_iota(jnp.int32, (C, C), 1)
    dist = (col - row) if rev else (row - col)
    code = jnp.full((C, C), -1, jnp.int32)
    m, level = C // 2, HG_BAND
    levels = []
    while m >= HG_BAND:
        levels.append(m)
        m //= 2
    for j, m in enumerate(levels):
        same = (row // (2 * m)) == (col // (2 * m))
        code = jnp.where(same, HG_BAND + len(levels) - 1 - j, code)
    code = jnp.where((row // HG_BAND) == (col // HG_BAND), dist, code)
    return jnp.where(dist < 0, -1, code)


def _hgrn_chunks(chains):
    C = HG_CHUNK
    row = lax.broadcasted_iota(jnp.int32, (C, 1), 0)
    col = lax.broadcasted_iota(jnp.int32, (1, C), 1)
    n = len(chains)

    kks, fs, bs = [], [], []
    for q, z, v, lb_row, state, code, rev in chains:
        kk = (1.0 - lb_row) / (1.0 + jnp.exp(z))
        g = jnp.log1p(-kk) * LOG2E
        tri = jnp.where((col >= row) if rev else (col <= row), 1.0, 0.0).astype(BF16)
        g1, g2, g3 = _split3(g)
        kks.append(kk)
        fs.append(1.0 - kk)
        bs.append(_dot(tri, g1) + _dot(tri, g2) + _dot(tri, g3))

    outs, states = [], []
    for (q, z, v, lb_row, state, code, rev), kk, b in zip(chains, kks, bs):
        q_hat = (q * jnp.exp2(b)).astype(BF16)
        outs.append(lax.dot_general(q_hat, state.astype(BF16), NT_DIMS, preferred_element_type=F32))
        b_end = b[0:1, :] if rev else b[C - 1:C, :]
        k_hat = (kk * jnp.exp2(b_end - b)).astype(BF16)
        states.append(state * jnp.exp2(b_end) + lax.dot_general(v, k_hat, TN_DIMS, preferred_element_type=F32))

    attns = []
    for (q, z, v, lb_row, state, code, rev), kk, f in zip(chains, kks, fs):
        step = (C - 1) if rev else 1
        u = kk
        attn = jnp.where(code == 0, jnp.sum(q * u, axis=-1, keepdims=True), 0.0)
        for d in range(1, HG_BAND):
            u = f * pltpu.roll(u, step, 0)
            attn = jnp.where(code == d, jnp.sum(q * u, axis=-1, keepdims=True), attn)
        attns.append(attn)

    m, level = HG_BAND, HG_BAND
    while m < C:
        for idx in range(n):
            q, z, v, lb_row, state, code, rev = chains[idx]
            b3 = bs[idx].reshape(C // (2 * m), 2 * m, HG_DIM)
            ref = b3[:, m:m + 1, :] if rev else b3[:, m - 1:m, :]
            e = jnp.exp2(-jnp.abs(b3 - ref)).reshape(C, HG_DIM)
            a_m = lax.dot_general((q * e).astype(BF16), (kks[idx] * e).astype(BF16), NT_DIMS,
                                  preferred_element_type=F32)
            attns[idx] = jnp.where(code == level, a_m, attns[idx])
        m, level = 2 * m, level + 1

    return [(o + _dot(attn.astype(BF16), ch[2]), st) for o, attn, ch, st in zip(outs, attns, chains, states)]


def _hgrn_kernel(hq_ref, hff_ref, hfb_ref, hi_ref, hg_ref, lbl_ref, onorm_ref, out_ref,
                 q_scr, of_scr, ob_scr, code_scr):
    C = HG_CHUNK
    n_chunks = hq_ref.shape[0] // C
    lg = lbl_ref[...]
    mx = jnp.maximum(lg[0], lg[1])
    e0 = jnp.exp(lg[0] - mx)
    lb = e0 / (e0 + jnp.exp(lg[1] - mx))
    q_scr[...] = _silu(hq_ref[...].astype(F32))
    code_scr[0] = _hgrn_pair_codes(False)
    code_scr[1] = _hgrn_pair_codes(True)

    def body(i, states):
        sf = pl.multiple_of(i * C, C)
        sb = pl.multiple_of((n_chunks - 1 - i) * C, C)
        chains = []
        for hd in range(HG_HEADS_PER_STEP):
            sl = slice(hd * HG_DIM, (hd + 1) * HG_DIM)
            chains.append((q_scr[pl.ds(sf, C), sl], hff_ref[pl.ds(sf, C), sl], hi_ref[pl.ds(sf, C), sl],
                           lb[0:1, sl], states[hd][0], code_scr[0], False))
            chains.append((q_scr[pl.ds(sb, C), sl], hfb_ref[pl.ds(sb, C), sl], hi_ref[pl.ds(sb, C), sl],
                           lb[1:2, sl], states[hd][1], code_scr[1], True))
        results = _hgrn_chunks(chains)
        new_states = []
        for hd in range(HG_HEADS_PER_STEP):
            sl = slice(hd * HG_DIM, (hd + 1) * HG_DIM)
            (o_f, st_f), (o_b, st_b) = results[2 * hd], results[2 * hd + 1]
            of_scr[pl.ds(sf, C), sl] = o_f
            ob_scr[pl.ds(sb, C), sl] = o_b
            new_states.append((st_f, st_b))
        return tuple(new_states)

    zero = jnp.zeros((HG_DIM, HG_DIM), F32)
    lax.fori_loop(0, n_chunks, body, ((zero, zero),) * HG_HEADS_PER_STEP)
    for hd in range(HG_HEADS_PER_STEP):
        sl = slice(hd * HG_DIM, (hd + 1) * HG_DIM)
        o = of_scr[:, sl] + ob_scr[:, sl]
        out_ref[:, sl] = ((_rms(o, HG_DIM) * onorm_ref[...]).astype(BF16)
                          * _silu(hg_ref[:, sl].astype(F32)).astype(BF16))


def _hgrn(hq, hff, hfb, hi, hg, lb_logits, o_norm, B, S):
    T = B * S
    width = HG_HEADS_PER_STEP * HG_DIM
    blk = pl.BlockSpec((S, width), lambda b, h: (b, h))
    n_layers = lb_logits.shape[0]
    return pl.pallas_call(
        _hgrn_kernel,
        grid=(B, HG_HEADS // HG_HEADS_PER_STEP),
        in_specs=[blk, blk, blk, blk, blk,
                  pl.BlockSpec((n_layers, 2, width), lambda b, h: (0, 0, h)),
                  pl.BlockSpec((1, HG_DIM), lambda b, h: (0, 0))],
        out_specs=blk,
        out_shape=jax.ShapeDtypeStruct((T, HG_WIDTH), BF16),
        scratch_shapes=[pltpu.VMEM((S, width), F32), pltpu.VMEM((S, width), F32), pltpu.VMEM((S, width), F32),
                        pltpu.VMEM((2, HG_CHUNK, HG_CHUNK), jnp.int32)],
        compiler_params=_params(("parallel", "parallel")),
        name="hgrn2",
    )(hq, hff, hfb, hi, hg, lb_logits, o_norm.reshape(1, HG_DIM))


def _memkv_kernel(mem_ref, nmem_ref, wkv_ref, kn_ref, k_out, v_out):
    hm = (_rms(mem_ref[0], D_MODEL) * nmem_ref[...]).astype(BF16)
    kv = _dot(hm, wkv_ref[...])
    for hd in range(X_HEADS):
        sl = slice(hd * X_HEAD_DIM, (hd + 1) * X_HEAD_DIM)
        k_out[0, :, sl] = (_rms(kv[:, sl], X_HEAD_DIM) * kn_ref[...]).astype(BF16)
    v_out[0] = kv[:, D_MODEL:].astype(BF16)


def _memkv(mem, norm_mem, w_kv, k_norm):
    B, M, _ = mem.shape
    wkv = w_kv.astype(BF16)
    blk = pl.BlockSpec((1, M, D_MODEL), lambda b: (b, 0, 0))
    return pl.pallas_call(
        _memkv_kernel,
        grid=(B,),
        in_specs=[blk, pl.BlockSpec((1, D_MODEL), lambda b: (0, 0)),
                  pl.BlockSpec(wkv.shape, lambda b: (0, 0)),
                  pl.BlockSpec((1, X_HEAD_DIM), lambda b: (0, 0))],
        out_specs=(blk, blk),
        out_shape=(jax.ShapeDtypeStruct((B, M, D_MODEL), BF16), jax.ShapeDtypeStruct((B, M, D_MODEL), BF16)),
        compiler_params=_params(("parallel",)),
        name="mem_kv",
    )(mem, norm_mem.reshape(1, -1), wkv, k_norm.reshape(1, -1))


def _router_logits(h, w_ref, bias):
    h_hi = h.astype(BF16)
    h_lo = (h - h_hi.astype(F32)).astype(BF16)
    both = _dot(h_hi, w_ref[...])
    return both[:, :LANES] + both[:, LANES:] + _dot(h_lo, w_ref[:, :LANES]) + bias


def _split_hi_lo(w):
    hi = w.astype(BF16)
    return jnp.concatenate([hi, (w - hi.astype(F32)).astype(BF16)], axis=-1)


def _first_max_lane(vals, lane_f):
    mx = jnp.max(vals, axis=-1, keepdims=True)
    idx = jnp.min(jnp.where(vals == mx, lane_f, float(LANES)), axis=-1, keepdims=True)
    return mx, idx


def _cross_kernel(x_ref, a_ref, r_ref, woa_ref, wor_ref, ncross_ref, wq_ref, qn_ref, kx_ref, vx_ref, wxo_ref,
                  nffn_ref, wg_ref, bg_ref, x2_out, gsel_out):
    tm = x_ref.shape[0]
    blocks = [slice(r0, r0 + CROSS_SUB_ROWS) for r0 in range(0, tm, CROSS_SUB_ROWS)]
    q_scale = X_HEAD_DIM ** -0.5 * LOG2E
    lane = lax.broadcasted_iota(jnp.int32, (1, LANES), 1)
    lane_f = lane.astype(F32)

    x1s = [x_ref[rows, :] + _dot(a_ref[rows, :], woa_ref[...]) + _dot(r_ref[rows, :], wor_ref[...])
           for rows in blocks]
    qxs = [_dot((_rms(x1, D_MODEL) * ncross_ref[...]).astype(BF16), wq_ref[...]) for x1 in x1s]
    heads = [[] for _ in blocks]
    for hd in range(X_HEADS):
        sl = slice(hd * X_HEAD_DIM, (hd + 1) * X_HEAD_DIM)
        for n, qx in enumerate(qxs):
            qh = (_rms(qx[:, sl], X_HEAD_DIM) * qn_ref[...] * q_scale).astype(BF16)
            s = lax.dot_general(qh, kx_ref[0, :, sl], NT_DIMS, preferred_element_type=F32)
            p = jnp.exp2(s - jnp.max(s, axis=-1, keepdims=True))
            o = _dot(p.astype(BF16), vx_ref[0, :, sl]) / jnp.sum(p, axis=-1, keepdims=True)
            heads[n].append(o.astype(BF16))
    x2s = [x1 + _dot(jnp.concatenate(hs, axis=-1), wxo_ref[...]) for x1, hs in zip(x1s, heads)]
    for rows, x2 in zip(blocks, x2s):
        x2_out[rows, :] = x2
    for rows, x2 in zip(blocks, x2s):
        h3 = _rms(x2, D_MODEL) * nffn_ref[...]
        logits = _router_logits(h3, wg_ref, bg_ref[...])
        _, g_idx = _first_max_lane(jnp.where(lane < MOE_GROUPS, logits, -jnp.inf), lane_f)
        gsel_out[rows, :] = g_idx.astype(jnp.int32)


def _cross(x2d, a, r, w_out, norm_cross, w_q, q_norm, kx, vx, w_o, norm_ffn, w_group, b_group, B, S, tm):
    T = B * S
    per_b = S // tm
    M = kx.shape[1]
    woa = w_out[:MLA_HEADS * MLA_V].astype(BF16)
    wor = w_out[MLA_HEADS * MLA_V:].astype(BF16)
    wq = w_q.astype(BF16)
    wxo = w_o.astype(BF16)
    wg = _split_hi_lo(jnp.pad(w_group, ((0, 0), (0, LANES - MOE_GROUPS))))
    bg = jnp.pad(b_group, (0, LANES - MOE_GROUPS)).reshape(1, LANES)

    def full(arr):
        return pl.BlockSpec(arr.shape, lambda i: (0,) * arr.ndim)

    def rows(width):
        return pl.BlockSpec((tm, width), lambda i: (i, 0))

    ncross = norm_cross.reshape(1, -1)
    qn = q_norm.reshape(1, -1)
    nffn = norm_ffn.reshape(1, -1)
    mem_blk = pl.BlockSpec((1, M, D_MODEL), lambda i: (i // per_b, 0, 0))
    return pl.pallas_call(
        _cross_kernel,
        grid=(T // tm,),
        in_specs=[rows(D_MODEL), rows(MLA_HEADS * MLA_V), rows(HG_WIDTH), full(woa), full(wor), full(ncross),
                  full(wq), full(qn), mem_blk, mem_blk, full(wxo), full(nffn), full(wg), full(bg)],
        out_specs=(rows(D_MODEL), rows(1)),
        out_shape=(jax.ShapeDtypeStruct((T, D_MODEL), F32), jax.ShapeDtypeStruct((T, 1), jnp.int32)),
        compiler_params=_params(("parallel",)),
        name="cross",
    )(x2d, a, r, woa, wor, ncross, wq, qn, kx, vx, wxo, nffn, wg, bg)


def _moe_kernel(grp_ref, nvalid_ref,
                src_ref, src_next_ref, src_next2_ref, src_prev_ref,
                x_hbm, nffn_ref, wr_ref, br_ref, wg_ref, wu_ref, wd_ref,
                out_hbm, xbuf, ybuf, gsem, ssem):
    i = pl.program_id(0)
    n_blocks = pl.num_programs(0)
    tb = xbuf.shape[1]
    slot = i % MOE_SLOTS
    slot_next2 = (i + 2) % MOE_SLOTS
    slot_prev = (i + MOE_SLOTS - 1) % MOE_SLOTS

    def n_valid_at(j):
        inside = jnp.logical_and(j >= 0, j < n_blocks)
        return jnp.where(inside, nvalid_ref[jnp.clip(j, 0, n_blocks - 1)], 0)

    n_valid = nvalid_ref[i]
    n_prev = n_valid_at(i - 1)
    n_next2 = n_valid_at(i + 2)

    def gather_row(idx_ref, r, to_slot):
        tok = idx_ref[0, 0, r]
        pltpu.make_async_copy(x_hbm.at[pl.ds(tok, 1)], xbuf.at[to_slot, pl.ds(r, 1)], gsem.at[to_slot]).start()

    def scatter_row(idx_ref, r, from_slot):
        tok = idx_ref[0, 0, r]
        pltpu.make_async_copy(ybuf.at[from_slot, pl.ds(r, 1)], out_hbm.at[pl.ds(tok, 1)], ssem.at[from_slot]).start()

    def gather_all(idx_ref, to_slot):
        def body(r, carry):
            gather_row(idx_ref, r, to_slot)
            return carry
        lax.fori_loop(0, tb, body, 0)

    def scatter_some(idx_ref, from_slot, count):
        def body(r, carry):
            scatter_row(idx_ref, r, from_slot)
            return carry
        lax.fori_loop(0, count, body, 0)

    def wait_scatter(of_slot, count):
        whole = pl.multiple_of((count // SUBLANES) * SUBLANES, SUBLANES)

        @pl.when(whole > 0)
        def _():
            pltpu.make_async_copy(ybuf.at[of_slot, pl.ds(0, whole)], out_hbm.at[pl.ds(0, whole)],
                                  ssem.at[of_slot]).wait()

        def one(r, carry):
            pltpu.make_async_copy(ybuf.at[of_slot, pl.ds(0, 1)], out_hbm.at[pl.ds(0, 1)], ssem.at[of_slot]).wait()
            return carry
        lax.fori_loop(0, count - whole, one, 0)

    def evaluate(side_work):
        pltpu.make_async_copy(x_hbm.at[pl.ds(0, tb)], xbuf.at[slot], gsem.at[slot]).wait()
        x = xbuf[slot]
        h = _rms(x, D_MODEL) * nffn_ref[...]
        logits = _router_logits(h, wr_ref.at[0], br_ref[0])
        lane = lax.broadcasted_iota(jnp.int32, (1, LANES), 1)
        lane_f = lane.astype(F32)
        is_group = lane < MOE_GROUPS
        is_expert = jnp.logical_and(lane >= MOE_GROUPS, lane < MOE_GROUPS + MOE_PER_GROUP)
        g_logits = jnp.where(is_group, logits, -jnp.inf)
        g_max = jnp.max(g_logits, axis=-1, keepdims=True)
        g_weight = 1.0 / jnp.sum(jnp.exp(g_logits - g_max), axis=-1, keepdims=True)
        e_logits = jnp.where(is_expert, logits, -jnp.inf)
        e1, i1 = _first_max_lane(e_logits, lane_f)
        e2, i2 = _first_max_lane(jnp.where(lane_f == i1, -jnp.inf, e_logits), lane_f)
        t = jnp.exp(e2 - e1)
        w1 = g_weight / (1.0 + t)
        w2 = g_weight * t / (1.0 + t)

        hb = h.astype(BF16)
        pieces = []
        for e in range(MOE_PER_GROUP):
            side_work(e)
            sl = slice(e * MOE_HIDDEN, (e + 1) * MOE_HIDDEN)
            gate = _dot(hb, wg_ref[0, e])
            up = _dot(hb, wu_ref[0, e])
            lane_e = float(MOE_GROUPS + e)
            w_e = jnp.where(i1 == lane_e, w1, w2)
            chosen = jnp.logical_or(i1 == lane_e, i2 == lane_e)
            pieces.append(jnp.where(chosen, _silu(gate) * up * w_e, 0.0).astype(BF16))
        y = _dot(jnp.concatenate(pieces, axis=-1), wd_ref[0])
        ybuf[slot] = x + y

    @pl.when(i == 0)
    def _():
        @pl.when(n_valid > 0)
        def _():
            gather_all(src_ref, 0)

        @pl.when(n_valid_at(1) > 0)
        def _():
            gather_all(src_next_ref, 1)

    wait_scatter(slot, n_valid_at(i - MOE_SLOTS))

    steady = jnp.logical_and(jnp.logical_and(n_valid > 0, n_next2 > 0), n_prev == tb)
    share = tb // MOE_PER_GROUP

    @pl.when(steady)
    def _():
        def side_work(e):
            for r in range(e * share, (e + 1) * share):
                gather_row(src_next2_ref, r, slot_next2)
                scatter_row(src_prev_ref, r, slot_prev)
        evaluate(side_work)

    @pl.when(jnp.logical_not(steady))
    def _():
        @pl.when(n_next2 > 0)
        def _():
            gather_all(src_next2_ref, slot_next2)
        scatter_some(src_prev_ref, slot_prev, n_prev)

        @pl.when(n_valid > 0)
        def _():
            evaluate(lambda e: None)

    @pl.when(i == n_blocks - 1)
    def _():
        wait_scatter((i + MOE_SLOTS - 2) % MOE_SLOTS, n_valid_at(i - 2))
        wait_scatter(slot_prev, n_prev)
        scatter_some(src_ref, slot, n_valid)
        wait_scatter(slot, n_valid)


def _moe(x2, gsel, norm_ffn, w_group, b_group, w_expert, b_expert, w_gate, w_up, w_down, tb):
    T = x2.shape[0]
    G, E, H = MOE_GROUPS, MOE_PER_GROUP, MOE_HIDDEN
    n_blocks = T // tb + G

    g = gsel.reshape(T)
    order = jnp.argsort(g, stable=True).astype(jnp.int32)
    counts = jnp.sum(g[:, None] == jnp.arange(G, dtype=jnp.int32)[None, :], axis=0).astype(jnp.int32)
    blocks_per_group = (counts + tb - 1) // tb
    block_end = jnp.cumsum(blocks_per_group)
    block_start = block_end - blocks_per_group
    token_start = jnp.cumsum(counts) - counts
    bi = jnp.arange(n_blocks, dtype=jnp.int32)
    block_group = jnp.minimum(jnp.searchsorted(block_end, bi, side='right'), G - 1).astype(jnp.int32)
    first_row = (bi - block_start[block_group]) * tb
    n_valid = jnp.clip(counts[block_group] - first_row, 0, tb)
    n_valid = jnp.where(bi < block_end[G - 1], n_valid, 0).astype(jnp.int32)
    r = jnp.arange(tb, dtype=jnp.int32)[None, :]
    pos = token_start[block_group][:, None] + first_row[:, None] + r
    src = jnp.where(r < n_valid[:, None], order[jnp.clip(pos, 0, T - 1)], 0).astype(jnp.int32)
    src = src.reshape(n_blocks, 1, tb)

    w_e = w_expert.reshape(D_MODEL, G, E).transpose(1, 0, 2)
    w_r = jnp.concatenate([jnp.broadcast_to(w_group[None], (G, D_MODEL, G)), w_e,
                           jnp.zeros((G, D_MODEL, LANES - G - E), F32)], axis=-1)
    wr = _split_hi_lo(w_r)
    b_r = jnp.concatenate([jnp.broadcast_to(b_group[None], (G, G)), b_expert.reshape(G, E),
                           jnp.zeros((G, LANES - G - E), F32)], axis=-1).reshape(G, 1, LANES)
    wg = w_gate.reshape(G, E, D_MODEL, H).astype(BF16)
    wu = w_up.reshape(G, E, D_MODEL, H).astype(BF16)
    wd = w_down.reshape(G, E * H, D_MODEL).astype(BF16)
    nffn = norm_ffn.reshape(1, -1)

    def by_group(shape):
        return pl.BlockSpec((1,) + shape, lambda i, grp, nv: (grp[i], 0, 0))

    smem_rows = lambda f: pl.BlockSpec((1, 1, tb), f, memory_space=pltpu.SMEM)
    grid_spec = pltpu.PrefetchScalarGridSpec(
        num_scalar_prefetch=2,
        grid=(n_blocks,),
        in_specs=[smem_rows(lambda i, grp, nv: (i, 0, 0)),
                  smem_rows(lambda i, grp, nv: (jnp.minimum(i + 1, n_blocks - 1), 0, 0)),
                  smem_rows(lambda i, grp, nv: (jnp.minimum(i + 2, n_blocks - 1), 0, 0)),
                  smem_rows(lambda i, grp, nv: (jnp.maximum(i - 1, 0), 0, 0)),
                  pl.BlockSpec(memory_space=pl.ANY),
                  pl.BlockSpec((1, D_MODEL), lambda i, grp, nv: (0, 0)),
                  by_group((D_MODEL, 2 * LANES)), by_group((1, LANES)),
                  pl.BlockSpec((1, E, D_MODEL, H), lambda i, grp, nv: (grp[i], 0, 0, 0)),
                  pl.BlockSpec((1, E, D_MODEL, H), lambda i, grp, nv: (grp[i], 0, 0, 0)),
                  by_group((E * H, D_MODEL))],
        out_specs=pl.BlockSpec(memory_space=pl.ANY),
        scratch_shapes=[pltpu.VMEM((MOE_SLOTS, tb, D_MODEL), F32), pltpu.VMEM((MOE_SLOTS, tb, D_MODEL), F32),
                        pltpu.SemaphoreType.DMA((MOE_SLOTS,)), pltpu.SemaphoreType.DMA((MOE_SLOTS,))])
    return pl.pallas_call(
        _moe_kernel,
        grid_spec=grid_spec,
        out_shape=jax.ShapeDtypeStruct((T, D_MODEL), F32),
        compiler_params=_params(("arbitrary",)),
        name="experts",
    )(block_group, n_valid, src, src, src, src, x2, nffn, wr, b_r, wg, wu, wd)


def _pick(n, pref):
    t = min(pref, n)
    while n % t:
        t //= 2
    return t


def kernel(x, mem, positions, norm_mix, w_in, mla_q_a_norm, mla_w_q_up, mla_kv_a_norm, mla_w_kv_up, mla_q_norm, mla_k_norm, hg_lb_logits, hg_o_norm, w_out, norm_cross, norm_mem, x_w_q, x_w_kv, x_q_norm, x_k_norm, x_w_o, norm_ffn, moe_w_group, moe_b_group, moe_w_expert, moe_b_expert, moe_w_gate, moe_w_up, moe_w_down):
    B, S, D = x.shape
    assert D == D_MODEL and w_in.shape[0] == 1 and S % HG_CHUNK == 0
    T = B * S
    x2d = x.reshape(T, D)
    pos2d = positions.reshape(T, 1).astype(jnp.int32)

    q, k, v, hq, hff, hfb, hi, hg = _inproj(
        x2d, pos2d, norm_mix[0], w_in[0], mla_q_a_norm[0], mla_w_q_up[0], mla_kv_a_norm[0], mla_w_kv_up[0],
        mla_q_norm[0], mla_k_norm[0], tm=_pick(T, 256))
    a = _mla_attention(q, k, v, B, S, tq=_pick(S, 512), tk=_pick(S, 1024))
    r = _hgrn(hq, hff, hfb, hi, hg, hg_lb_logits, hg_o_norm[0], B, S)
    kx, vx = _memkv(mem, norm_mem[0], x_w_kv[0], x_k_norm[0])
    x2, gsel = _cross(x2d, a, r, w_out[0], norm_cross[0], x_w_q[0], x_q_norm[0], kx, vx, x_w_o[0], norm_ffn[0],
                      moe_w_group[0], moe_b_group[0], B, S, tm=_pick(S, 512))
    out = _moe(x2, gsel, norm_ffn[0], moe_w_group[0], moe_b_group[0], moe_w_expert[0], moe_b_expert[0],
               moe_w_gate[0], moe_w_up[0], moe_w_down[0], tb=_pick(T, 256))
    return out.reshape(B, S, D)
```

```python
import functools
import math

import jax
import jax.numpy as jnp
from jax import lax
from jax.experimental import pallas as pl
from jax.experimental.pallas import tpu as pltpu

F32 = jnp.float32
BF16 = jnp.bfloat16

LANES = 128
SUBLANES = 8
VMEM_LIMIT_BYTES = 56 * 1024 * 1024

NORM_EPS = 1e-6
LOG2E = math.log2(math.e)

D_MODEL = 1024
MLA_HEADS = 8
MLA_NOPE = 64
MLA_ROPE = 32
MLA_QK = MLA_NOPE + MLA_ROPE
MLA_V = 64
MLA_Q_RANK = 192
MLA_Q_RANK_PAD = 256
MLA_KV_RANK = 128
MLA_HEADS_PER_STEP = 4
ROPE_BASE = 10000.0
CROSS_SUB_ROWS = 256
HG_HEADS = 4
HG_DIM = 128
HG_WIDTH = HG_HEADS * HG_DIM
HG_CHUNK = 128
HG_BAND = 4
HG_HEADS_PER_STEP = 2
X_HEADS = 4
X_HEAD_DIM = D_MODEL // X_HEADS
MOE_GROUPS = 8
MOE_PER_GROUP = 8
MOE_HIDDEN = 256
MOE_GROUP_HIDDEN = MOE_PER_GROUP * MOE_HIDDEN
MOE_SLOTS = 3
IN_SIZES = (MLA_Q_RANK, MLA_KV_RANK, MLA_ROPE, HG_WIDTH, HG_WIDTH, HG_WIDTH, HG_WIDTH, HG_WIDTH)

NT_DIMS = (((1,), (1,)), ((), ()))
TN_DIMS = (((0,), (0,)), ((), ()))


def _rms(x, n):
    return x * lax.rsqrt(jnp.sum(x * x, axis=-1, keepdims=True) * (1.0 / n) + NORM_EPS)


def _silu(x):
    return x / (1.0 + jnp.exp(-x))


def _split3(x):
    a = x.astype(BF16)
    r = x - a.astype(F32)
    b = r.astype(BF16)
    c = (r - b.astype(F32)).astype(BF16)
    return a, b, c


def _dot(a, b):
    return jnp.dot(a, b, preferred_element_type=F32)


def _params(sem):
    return pltpu.CompilerParams(dimension_semantics=sem, vmem_limit_bytes=VMEM_LIMIT_BYTES)


def _inproj_kernel(x_ref, pos_ref, nmix_ref, w_ref, qan_ref, wq_ref, kvan_ref, wk_ref, wv_ref,
                   qn_ref, kn_ref, invf_ref,
                   q_out, k_out, v_out, hq_out, hff_out, hfb_out, hi_out, hg_out):
    h = (_rms(x_ref[...], D_MODEL) * nmix_ref[...]).astype(BF16)

    p = _dot(h, w_ref[:, 0:512])
    c_q = p[:, 0:MLA_Q_RANK_PAD]
    cqn = (_rms(c_q, MLA_Q_RANK) * qan_ref[...]).astype(BF16)
    q = _dot(cqn, wq_ref[...])
    c_kv = p[:, 256:384]
    ckvn = (_rms(c_kv, MLA_KV_RANK) * kvan_ref[...]).astype(BF16)
    k_nope = _dot(ckvn, wk_ref[...])
    v_out[...] = _dot(ckvn, wv_ref[...]).astype(BF16)
    k_rope = p[:, 384:512]

    ang = pos_ref[...].astype(F32) * invf_ref[...]
    cos = jnp.cos(ang)
    sin = jnp.sin(ang)
    lane = lax.broadcasted_iota(jnp.int32, (1, LANES), 1)
    half = MLA_ROPE // 2
    sin_lo = jnp.where(lane < MLA_NOPE + half, -sin, 0.0)
    sin_hi = jnp.where(lane >= MLA_NOPE + half, sin, 0.0)

    def rope(t):
        return t * cos + pltpu.roll(t, LANES - half, 1) * sin_lo + pltpu.roll(t, half, 1) * sin_hi

    q_scale = MLA_QK ** -0.5 * LOG2E
    mixer_outs = (hq_out, hff_out, hfb_out, hi_out, hg_out)
    for hd in range(MLA_HEADS):
        sl = slice(hd * LANES, (hd + 1) * LANES)
        qh = _rms(q[:, sl], MLA_QK) * qn_ref[...]
        q_out[:, sl] = (rope(qh) * q_scale).astype(BF16)
        kh = _rms(k_nope[:, sl] + k_rope, MLA_QK) * kn_ref[...]
        k_out[:, sl] = rope(kh).astype(BF16)
        if hd < len(mixer_outs):
            out = mixer_outs[hd]
            c0 = 512 + hd * HG_WIDTH
            out[...] = _dot(h, w_ref[:, c0:c0 + HG_WIDTH]).astype(out.dtype)


def _inproj(x2d, pos2d, norm_mix, w_in, q_a_norm, w_q_up, kv_a_norm, w_kv_up, q_norm, k_norm, tm):
    T = x2d.shape[0]
    c0 = 0
    cols = []
    for size in IN_SIZES:
        cols.append(w_in[:, c0:c0 + size])
        c0 += size
    w_cq, w_ckv, w_kr, w_hq, w_hff, w_hfb, w_hi, w_hg = cols
    w_cq = jnp.pad(w_cq, ((0, 0), (0, MLA_Q_RANK_PAD - MLA_Q_RANK)))
    w_kr = jnp.pad(w_kr, ((0, 0), (MLA_NOPE, LANES - MLA_QK)))
    w_big = jnp.concatenate([w_cq, w_ckv, w_kr, w_hq, w_hff, w_hfb, w_hi, w_hg], axis=1).astype(BF16)
    n_big = w_big.shape[1]

    qan = jnp.pad(q_a_norm, (0, MLA_Q_RANK_PAD - MLA_Q_RANK)).reshape(1, -1)
    wq = w_q_up.reshape(MLA_Q_RANK, MLA_HEADS, MLA_QK)
    wq = jnp.pad(wq, ((0, MLA_Q_RANK_PAD - MLA_Q_RANK), (0, 0), (0, LANES - MLA_QK)))
    wq = wq.reshape(MLA_Q_RANK_PAD, MLA_HEADS * LANES).astype(BF16)
    wkv = w_kv_up.reshape(MLA_KV_RANK, MLA_HEADS, MLA_NOPE + MLA_V)
    wk = jnp.pad(wkv[:, :, :MLA_NOPE], ((0, 0), (0, 0), (0, LANES - MLA_NOPE)))
    wk = wk.reshape(MLA_KV_RANK, MLA_HEADS * LANES).astype(BF16)
    wv = wkv[:, :, MLA_NOPE:].reshape(MLA_KV_RANK, MLA_HEADS * MLA_V).astype(BF16)
    qn = jnp.pad(q_norm, (0, LANES - MLA_QK)).reshape(1, LANES)
    kn = jnp.pad(k_norm, (0, LANES - MLA_QK)).reshape(1, LANES)
    half = MLA_ROPE // 2
    inv_freq = 1.0 / (ROPE_BASE ** (jnp.arange(half, dtype=F32) / half))
    invf = jnp.concatenate([jnp.zeros((MLA_NOPE,), F32), inv_freq, inv_freq,
                            jnp.zeros((LANES - MLA_QK,), F32)]).reshape(1, LANES)

    def full(a):
        return pl.BlockSpec(a.shape, lambda i: (0,) * a.ndim)

    def rows(width):
        return pl.BlockSpec((tm, width), lambda i: (i, 0))

    nmix = norm_mix.reshape(1, -1)
    kvan = kv_a_norm.reshape(1, -1)
    qk_w = MLA_HEADS * LANES
    v_w = MLA_HEADS * MLA_V
    out_shape = (
        jax.ShapeDtypeStruct((T, qk_w), BF16), jax.ShapeDtypeStruct((T, qk_w), BF16),
        jax.ShapeDtypeStruct((T, v_w), BF16),
        jax.ShapeDtypeStruct((T, HG_WIDTH), BF16), jax.ShapeDtypeStruct((T, HG_WIDTH), F32),
        jax.ShapeDtypeStruct((T, HG_WIDTH), F32), jax.ShapeDtypeStruct((T, HG_WIDTH), BF16),
        jax.ShapeDtypeStruct((T, HG_WIDTH), BF16))
    return pl.pallas_call(
        _inproj_kernel,
        grid=(T // tm,),
        in_specs=[rows(D_MODEL), rows(1), full(nmix), full(w_big), full(qan), full(wq), full(kvan),
                  full(wk), full(wv), full(qn), full(kn), full(invf)],
        out_specs=(rows(qk_w), rows(qk_w), rows(v_w), rows(HG_WIDTH), rows(HG_WIDTH), rows(HG_WIDTH),
                   rows(HG_WIDTH), rows(HG_WIDTH)),
        out_shape=out_shape,
        compiler_params=_params(("parallel",)),
        name="inproj",
    )(x2d, pos2d, nmix, w_big, qan, wq, kvan, wk, wv, qn, kn, invf)


def _mla_kernel(q_ref, k_ref, v_ref, o_ref, *, tk):
    heads = MLA_HEADS_PER_STEP
    n_chunks = k_ref.shape[0] // tk
    nsub = tk // LANES
    lane = lax.broadcasted_iota(jnp.int32, (1, LANES), 1)
    own = [lane < MLA_V, lane >= MLA_V]
    m = [None] * heads
    acc = [None] * heads
    for c in range(n_chunks):
        rows = slice(c * tk, (c + 1) * tk)
        scores = [lax.dot_general(q_ref[:, j * LANES:(j + 1) * LANES], k_ref[rows, j * LANES:(j + 1) * LANES],
                                  NT_DIMS, preferred_element_type=F32) for j in range(heads)]
        for j, s in enumerate(scores):
            vv = v_ref[rows, (j // 2) * LANES:(j // 2 + 1) * LANES]
            blk_max = s[:, 0:LANES]
            for i in range(1, nsub):
                blk_max = jnp.maximum(blk_max, s[:, i * LANES:(i + 1) * LANES])
            m_new = jnp.max(blk_max, axis=-1, keepdims=True)
            if c > 0:
                m_new = jnp.maximum(m[j], m_new)
            p = jnp.exp2((s - m_new).astype(BF16))
            pv = _dot(p, jnp.where(own[j % 2], vv, jnp.ones_like(vv)))
            acc[j] = pv if c == 0 else acc[j] * jnp.exp2(m[j] - m_new) + pv
            m[j] = m_new
    for pair in range(heads // 2):
        a0, a1 = acc[2 * pair], acc[2 * pair + 1]
        o0 = a0 / a0[:, MLA_V:MLA_V + 1]
        o1 = a1 / a1[:, 0:1]
        o_ref[:, pair * LANES:(pair + 1) * LANES] = jnp.where(own[0], o0, o1).astype(BF16)


def _mla_attention(q, k, v, B, S, tq, tk):
    T = B * S
    nq = S // tq
    heads = MLA_HEADS_PER_STEP
    return pl.pallas_call(
        functools.partial(_mla_kernel, tk=tk),
        grid=(B, MLA_HEADS // heads, nq),
        in_specs=[pl.BlockSpec((tq, heads * LANES), lambda b, h, i: (b * nq + i, h)),
                  pl.BlockSpec((S, heads * LANES), lambda b, h, i: (b, h)),
                  pl.BlockSpec((S, heads * MLA_V), lambda b, h, i: (b, h))],
        out_specs=pl.BlockSpec((tq, heads * MLA_V), lambda b, h, i: (b * nq + i, h)),
        out_shape=jax.ShapeDtypeStruct((T, MLA_HEADS * MLA_V), BF16),
        compiler_params=_params(("parallel", "parallel", "arbitrary")),
        name="mla_attention",
    )(q, k, v)


def _hgrn_pair_codes(rev):
    C = HG_CHUNK
    row = lax.broadcasted_iota(jnp.int32, (C, C), 0)
    col = lax.broadcasted_iota(jnp.int32, (C, C), 1)
    dist = (col - row) if rev else (row - col)
    code = jnp.full((C, C), -1, jnp.int32)
    m, level = C // 2, HG_BAND
    levels = []
    while m >= HG_BAND:
        levels.append(m)
        m //= 2
    for j, m in enumerate(levels):
        same = (row // (2 * m)) == (col // (2 * m))
        code = jnp.where(same, HG_BAND + len(levels) - 1 - j, code)
    code = jnp.where((row // HG_BAND) == (col // HG_BAND), dist, code)
    return jnp.where(dist < 0, -1, code)


def _hgrn_chunks(chains):
    C = HG_CHUNK
    row = lax.broadcasted_iota(jnp.int32, (C, 1), 0)
    col = lax.broadcasted_iota(jnp.int32, (1, C), 1)
    n = len(chains)

    kks, fs, bs = [], [], []
    for q, z, v, lb_row, state, code, rev in chains:
        kk = (1.0 - lb_row) / (1.0 + jnp.exp(z))
        f = 1.0 - kk
        g = jnp.log2(f)
        tri = jnp.where((col >= row) if rev else (col <= row), 1.0, 0.0).astype(BF16)
        g1, g2, g3 = _split3(g)
        kks.append(kk)
        fs.append(f)
        bs.append(_dot(tri, g1) + _dot(tri, g2) + _dot(tri, g3))

    outs, states = [], []
    for (q, z, v, lb_row, state, code, rev), kk, b in zip(chains, kks, bs):
        q_hat = (q * jnp.exp2(b)).astype(BF16)
        outs.append(lax.dot_general(q_hat, state.astype(BF16), NT_DIMS, preferred_element_type=F32))
        b_end = b[0:1, :] if rev else b[C - 1:C, :]
        k_hat = (kk * jnp.exp2(b_end - b)).astype(BF16)
        states.append(state * jnp.exp2(b_end) + lax.dot_general(v, k_hat, TN_DIMS, preferred_element_type=F32))

    attns = []
    for (q, z, v, lb_row, state, code, rev), kk, f in zip(chains, kks, fs):
        step = (C - 1) if rev else 1
        u = kk
        attn = jnp.where(code == 0, jnp.sum(q * u, axis=-1, keepdims=True), 0.0)
        for d in range(1, HG_BAND):
            u = f * pltpu.roll(u, step, 0)
            attn = jnp.where(code == d, jnp.sum(q * u, axis=-1, keepdims=True), attn)
        attns.append(attn)

    m, level = HG_BAND, HG_BAND
    while m < C:
        for idx in range(n):
            q, z, v, lb_row, state, code, rev = chains[idx]
            b3 = bs[idx].reshape(C // (2 * m), 2 * m, HG_DIM)
            ref = b3[:, m:m + 1, :] if rev else b3[:, m - 1:m, :]
            e = jnp.exp2(-jnp.abs(b3 - ref)).reshape(C, HG_DIM)
            a_m = lax.dot_general((q * e).astype(BF16), (kks[idx] * e).astype(BF16), NT_DIMS,
                                  preferred_element_type=F32)
            attns[idx] = jnp.where(code == level, a_m, attns[idx])
        m, level = 2 * m, level + 1

    return [(o + _dot(attn.astype(BF16), ch[2]), st) for o, attn, ch, st in zip(outs, attns, chains, states)]


def _hgrn_kernel(hq_ref, hff_ref, hfb_ref, hi_ref, hg_ref, lbl_ref, onorm_ref, out_ref,
                 q_scr, of_scr, ob_scr, code_scr):
    C = HG_CHUNK
    n_chunks = hq_ref.shape[0] // C
    lg = lbl_ref[...]
    mx = jnp.maximum(lg[0], lg[1])
    e0 = jnp.exp(lg[0] - mx)
    lb = e0 / (e0 + jnp.exp(lg[1] - mx))
    q_scr[...] = _silu(hq_ref[...].astype(F32))
    code_scr[0] = _hgrn_pair_codes(False)
    code_scr[1] = _hgrn_pair_codes(True)

    def body(i, states):
        sf = pl.multiple_of(i * C, C)
        sb = pl.multiple_of((n_chunks - 1 - i) * C, C)
        chains = []
        for hd in range(HG_HEADS_PER_STEP):
            sl = slice(hd * HG_DIM, (hd + 1) * HG_DIM)
            chains.append((q_scr[pl.ds(sf, C), sl], hff_ref[pl.ds(sf, C), sl], hi_ref[pl.ds(sf, C), sl],
                           lb[0:1, sl], states[hd][0], code_scr[0], False))
            chains.append((q_scr[pl.ds(sb, C), sl], hfb_ref[pl.ds(sb, C), sl], hi_ref[pl.ds(sb, C), sl],
                           lb[1:2, sl], states[hd][1], code_scr[1], True))
        results = _hgrn_chunks(chains)
        new_states = []
        for hd in range(HG_HEADS_PER_STEP):
            sl = slice(hd * HG_DIM, (hd + 1) * HG_DIM)
            (o_f, st_f), (o_b, st_b) = results[2 * hd], results[2 * hd + 1]
            of_scr[pl.ds(sf, C), sl] = o_f
            ob_scr[pl.ds(sb, C), sl] = o_b
            new_states.append((st_f, st_b))
        return tuple(new_states)

    zero = jnp.zeros((HG_DIM, HG_DIM), F32)
    lax.fori_loop(0, n_chunks, body, ((zero, zero),) * HG_HEADS_PER_STEP)
    for hd in range(HG_HEADS_PER_STEP):
        sl = slice(hd * HG_DIM, (hd + 1) * HG_DIM)
        o = of_scr[:, sl] + ob_scr[:, sl]
        out_ref[:, sl] = ((_rms(o, HG_DIM) * onorm_ref[...]).astype(BF16)
                          * _silu(hg_ref[:, sl].astype(F32)).astype(BF16))


def _hgrn(hq, hff, hfb, hi, hg, lb_logits, o_norm, B, S):
    T = B * S
    width = HG_HEADS_PER_STEP * HG_DIM
    blk = pl.BlockSpec((S, width), lambda b, h: (b, h))
    n_layers = lb_logits.shape[0]
    return pl.pallas_call(
        _hgrn_kernel,
        grid=(B, HG_HEADS // HG_HEADS_PER_STEP),
        in_specs=[blk, blk, blk, blk, blk,
                  pl.BlockSpec((n_layers, 2, width), lambda b, h: (0, 0, h)),
                  pl.BlockSpec((1, HG_DIM), lambda b, h: (0, 0))],
        out_specs=blk,
        out_shape=jax.ShapeDtypeStruct((T, HG_WIDTH), BF16),
        scratch_shapes=[pltpu.VMEM((S, width), F32), pltpu.VMEM((S, width), F32), pltpu.VMEM((S, width), F32),
                        pltpu.VMEM((2, HG_CHUNK, HG_CHUNK), jnp.int32)],
        compiler_params=_params(("parallel", "parallel")),
        name="hgrn2",
    )(hq, hff, hfb, hi, hg, lb_logits, o_norm.reshape(1, HG_DIM))


def _memkv_kernel(mem_ref, nmem_ref, wkv_ref, kn_ref, k_out, v_out):
    hm = (_rms(mem_ref[0], D_MODEL) * nmem_ref[...]).astype(BF16)
    kv = _dot(hm, wkv_ref[...])
    for hd in range(X_HEADS):
        sl = slice(hd * X_HEAD_DIM, (hd + 1) * X_HEAD_DIM)
        k_out[0, :, sl] = (_rms(kv[:, sl], X_HEAD_DIM) * kn_ref[...]).astype(BF16)
    v_out[0] = kv[:, D_MODEL:].astype(BF16)


def _memkv(mem, norm_mem, w_kv, k_norm):
    B, M, _ = mem.shape
    wkv = w_kv.astype(BF16)
    blk = pl.BlockSpec((1, M, D_MODEL), lambda b: (b, 0, 0))
    return pl.pallas_call(
        _memkv_kernel,
        grid=(B,),
        in_specs=[blk, pl.BlockSpec((1, D_MODEL), lambda b: (0, 0)),
                  pl.BlockSpec(wkv.shape, lambda b: (0, 0)),
                  pl.BlockSpec((1, X_HEAD_DIM), lambda b: (0, 0))],
        out_specs=(blk, blk),
        out_shape=(jax.ShapeDtypeStruct((B, M, D_MODEL), BF16), jax.ShapeDtypeStruct((B, M, D_MODEL), BF16)),
        compiler_params=_params(("parallel",)),
        name="mem_kv",
    )(mem, norm_mem.reshape(1, -1), wkv, k_norm.reshape(1, -1))


def _router_logits(h, w_ref, bias):
    h_hi = h.astype(BF16)
    h_lo = (h - h_hi.astype(F32)).astype(BF16)
    both = _dot(h_hi, w_ref[...])
    return both[:, :LANES] + both[:, LANES:] + _dot(h_lo, w_ref[:, :LANES]) + bias


def _split_hi_lo(w):
    hi = w.astype(BF16)
    return jnp.concatenate([hi, (w - hi.astype(F32)).astype(BF16)], axis=-1)


def _first_max_lane(vals, lane_f):
    mx = jnp.max(vals, axis=-1, keepdims=True)
    idx = jnp.min(jnp.where(vals == mx, lane_f, float(LANES)), axis=-1, keepdims=True)
    return mx, idx


def _cross_kernel(x_ref, a_ref, r_ref, woa_ref, wor_ref, ncross_ref, wq_ref, qn_ref, kx_ref, vx_ref, wxo_ref,
                  nffn_ref, wg_ref, bg_ref, x2_out, gsel_out):
    tm = x_ref.shape[0]
    sub = min(CROSS_SUB_ROWS, tm)
    blocks = [slice(r0, r0 + sub) for r0 in range(0, tm, sub)]
    q_scale = X_HEAD_DIM ** -0.5 * LOG2E
    lane = lax.broadcasted_iota(jnp.int32, (1, LANES), 1)
    lane_f = lane.astype(F32)

    x1s = [x_ref[rows, :] + _dot(a_ref[rows, :], woa_ref[...]) + _dot(r_ref[rows, :], wor_ref[...])
           for rows in blocks]
    qxs = [_dot((_rms(x1, D_MODEL) * ncross_ref[...]).astype(BF16), wq_ref[...]) for x1 in x1s]
    heads = [[] for _ in blocks]
    for hd in range(X_HEADS):
        sl = slice(hd * X_HEAD_DIM, (hd + 1) * X_HEAD_DIM)
        for n, qx in enumerate(qxs):
            qh = (_rms(qx[:, sl], X_HEAD_DIM) * qn_ref[...] * q_scale).astype(BF16)
            s = lax.dot_general(qh, kx_ref[0, :, sl], NT_DIMS, preferred_element_type=F32)
            p = jnp.exp2(s - jnp.max(s, axis=-1, keepdims=True))
            o = _dot(p.astype(BF16), vx_ref[0, :, sl]) / jnp.sum(p, axis=-1, keepdims=True)
            heads[n].append(o.astype(BF16))
    x2s = [x1 + _dot(jnp.concatenate(hs, axis=-1), wxo_ref[...]) for x1, hs in zip(x1s, heads)]
    for rows, x2 in zip(blocks, x2s):
        x2_out[rows, :] = x2
    for rows, x2 in zip(blocks, x2s):
        h3 = _rms(x2, D_MODEL) * nffn_ref[...]
        logits = _router_logits(h3, wg_ref, bg_ref[...])
        _, g_idx = _first_max_lane(jnp.where(lane < MOE_GROUPS, logits, -jnp.inf), lane_f)
        gsel_out[rows, :] = g_idx.astype(jnp.int32)


def _cross(x2d, a, r, w_out, norm_cross, w_q, q_norm, kx, vx, w_o, norm_ffn, w_group, b_group, B, S, tm):
    T = B * S
    per_b = S // tm
    M = kx.shape[1]
    woa = w_out[:MLA_HEADS * MLA_V].astype(BF16)
    wor = w_out[MLA_HEADS * MLA_V:].astype(BF16)
    wq = w_q.astype(BF16)
    wxo = w_o.astype(BF16)
    wg = _split_hi_lo(jnp.pad(w_group, ((0, 0), (0, LANES - MOE_GROUPS))))
    bg = jnp.pad(b_group, (0, LANES - MOE_GROUPS)).reshape(1, LANES)

    def full(arr):
        return pl.BlockSpec(arr.shape, lambda i: (0,) * arr.ndim)

    def rows(width):
        return pl.BlockSpec((tm, width), lambda i: (i, 0))

    ncross = norm_cross.reshape(1, -1)
    qn = q_norm.reshape(1, -1)
    nffn = norm_ffn.reshape(1, -1)
    mem_blk = pl.BlockSpec((1, M, D_MODEL), lambda i: (i // per_b, 0, 0))
    return pl.pallas_call(
        _cross_kernel,
        grid=(T // tm,),
        in_specs=[rows(D_MODEL), rows(MLA_HEADS * MLA_V), rows(HG_WIDTH), full(woa), full(wor), full(ncross),
                  full(wq), full(qn), mem_blk, mem_blk, full(wxo), full(nffn), full(wg), full(bg)],
        out_specs=(rows(D_MODEL), rows(1)),
        out_shape=(jax.ShapeDtypeStruct((T, D_MODEL), F32), jax.ShapeDtypeStruct((T, 1), jnp.int32)),
        compiler_params=_params(("parallel",)),
        name="cross",
    )(x2d, a, r, woa, wor, ncross, wq, qn, kx, vx, wxo, nffn, wg, bg)


def _moe_kernel(grp_ref, nvalid_ref,
                src_ref, src_next_ref, src_next2_ref, src_prev_ref,
                x_hbm, nffn_ref, wr_ref, br_ref, wg_ref, wu_ref, wd_ref,
                out_hbm, xbuf, ybuf, gsem, ssem):
    i = pl.program_id(0)
    n_blocks = pl.num_programs(0)
    tb = xbuf.shape[1]
    slot = i % MOE_SLOTS
    slot_next2 = (i + 2) % MOE_SLOTS
    slot_prev = (i + MOE_SLOTS - 1) % MOE_SLOTS

    def n_valid_at(j):
        inside = jnp.logical_and(j >= 0, j < n_blocks)
        return jnp.where(inside, nvalid_ref[jnp.clip(j, 0, n_blocks - 1)], 0)

    n_valid = nvalid_ref[i]
    n_prev = n_valid_at(i - 1)
    n_next2 = n_valid_at(i + 2)

    def gather_row(idx_ref, r, to_slot):
        tok = idx_ref[0, 0, r]
        pltpu.make_async_copy(x_hbm.at[pl.ds(tok, 1)], xbuf.at[to_slot, pl.ds(r, 1)], gsem.at[to_slot]).start()

    def scatter_row(idx_ref, r, from_slot):
        tok = idx_ref[0, 0, r]
        pltpu.make_async_copy(ybuf.at[from_slot, pl.ds(r, 1)], out_hbm.at[pl.ds(tok, 1)], ssem.at[from_slot]).start()

    def gather_all(idx_ref, to_slot):
        def body(r, carry):
            gather_row(idx_ref, r, to_slot)
            return carry
        lax.fori_loop(0, tb, body, 0)

    def scatter_some(idx_ref, from_slot, count):
        def body(r, carry):
            scatter_row(idx_ref, r, from_slot)
            return carry
        lax.fori_loop(0, count, body, 0)

    def wait_scatter(of_slot, count):
        whole = pl.multiple_of((count // SUBLANES) * SUBLANES, SUBLANES)

        @pl.when(whole > 0)
        def _():
            pltpu.make_async_copy(ybuf.at[of_slot, pl.ds(0, whole)], out_hbm.at[pl.ds(0, whole)],
                                  ssem.at[of_slot]).wait()

        def one(r, carry):
            pltpu.make_async_copy(ybuf.at[of_slot, pl.ds(0, 1)], out_hbm.at[pl.ds(0, 1)], ssem.at[of_slot]).wait()
            return carry
        lax.fori_loop(0, count - whole, one, 0)

    def evaluate(side_work):
        pltpu.make_async_copy(x_hbm.at[pl.ds(0, tb)], xbuf.at[slot], gsem.at[slot]).wait()
        x = xbuf[slot]
        h = _rms(x, D_MODEL) * nffn_ref[...]
        logits = _router_logits(h, wr_ref.at[0], br_ref[0])
        lane = lax.broadcasted_iota(jnp.int32, (1, LANES), 1)
        lane_f = lane.astype(F32)
        is_group = lane < MOE_GROUPS
        is_expert = jnp.logical_and(lane >= MOE_GROUPS, lane < MOE_GROUPS + MOE_PER_GROUP)
        g_logits = jnp.where(is_group, logits, -jnp.inf)
        g_max = jnp.max(g_logits, axis=-1, keepdims=True)
        g_weight = 1.0 / jnp.sum(jnp.exp(g_logits - g_max), axis=-1, keepdims=True)
        e_logits = jnp.where(is_expert, logits, -jnp.inf)
        e1, i1 = _first_max_lane(e_logits, lane_f)
        e2, i2 = _first_max_lane(jnp.where(lane_f == i1, -jnp.inf, e_logits), lane_f)
        t = jnp.exp(e2 - e1)
        w1 = g_weight / (1.0 + t)
        w2 = g_weight * t / (1.0 + t)

        hb = h.astype(BF16)
        pieces = []
        for e in range(MOE_PER_GROUP):
            side_work(e)
            sl = slice(e * MOE_HIDDEN, (e + 1) * MOE_HIDDEN)
            gate = _dot(hb, wg_ref[0, e])
            up = _dot(hb, wu_ref[0, e])
            lane_e = float(MOE_GROUPS + e)
            w_e = jnp.where(i1 == lane_e, w1, w2)
            chosen = jnp.logical_or(i1 == lane_e, i2 == lane_e)
            pieces.append(jnp.where(chosen, _silu(gate) * up * w_e, 0.0).astype(BF16))
        y = _dot(jnp.concatenate(pieces, axis=-1), wd_ref[0])
        ybuf[slot] = x + y

    @pl.when(i == 0)
    def _():
        @pl.when(n_valid > 0)
        def _():
            gather_all(src_ref, 0)

        @pl.when(n_valid_at(1) > 0)
        def _():
            gather_all(src_next_ref, 1)

    wait_scatter(slot, n_valid_at(i - MOE_SLOTS))

    steady = jnp.logical_and(jnp.logical_and(n_valid > 0, n_next2 > 0), n_prev == tb)
    share = tb // MOE_PER_GROUP

    @pl.when(steady)
    def _():
        def side_work(e):
            for r in range(e * share, (e + 1) * share):
                gather_row(src_next2_ref, r, slot_next2)
                scatter_row(src_prev_ref, r, slot_prev)
        evaluate(side_work)

    @pl.when(jnp.logical_not(steady))
    def _():
        @pl.when(n_next2 > 0)
        def _():
            gather_all(src_next2_ref, slot_next2)
        scatter_some(src_prev_ref, slot_prev, n_prev)

        @pl.when(n_valid > 0)
        def _():
            evaluate(lambda e: None)

    @pl.when(i == n_blocks - 1)
    def _():
        wait_scatter((i + MOE_SLOTS - 2) % MOE_SLOTS, n_valid_at(i - 2))
        wait_scatter(slot_prev, n_prev)
        scatter_some(src_ref, slot, n_valid)
        wait_scatter(slot, n_valid)


def _moe(x2, gsel, norm_ffn, w_group, b_group, w_expert, b_expert, w_gate, w_up, w_down, tb):
    T = x2.shape[0]
    G, E, H = MOE_GROUPS, MOE_PER_GROUP, MOE_HIDDEN
    n_blocks = T // tb + G

    g = gsel.reshape(T)
    order = jnp.argsort(g, stable=True).astype(jnp.int32)
    counts = jnp.sum(g[:, None] == jnp.arange(G, dtype=jnp.int32)[None, :], axis=0).astype(jnp.int32)
    blocks_per_group = (counts + tb - 1) // tb
    block_end = jnp.cumsum(blocks_per_group)
    block_start = block_end - blocks_per_group
    token_start = jnp.cumsum(counts) - counts
    bi = jnp.arange(n_blocks, dtype=jnp.int32)
    block_group = jnp.minimum(jnp.searchsorted(block_end, bi, side='right'), G - 1).astype(jnp.int32)
    first_row = (bi - block_start[block_group]) * tb
    n_valid = jnp.clip(counts[block_group] - first_row, 0, tb)
    n_valid = jnp.where(bi < block_end[G - 1], n_valid, 0).astype(jnp.int32)
    r = jnp.arange(tb, dtype=jnp.int32)[None, :]
    pos = token_start[block_group][:, None] + first_row[:, None] + r
    src = jnp.where(r < n_valid[:, None], order[jnp.clip(pos, 0, T - 1)], 0).astype(jnp.int32)
    src = src.reshape(n_blocks, 1, tb)

    w_e = w_expert.reshape(D_MODEL, G, E).transpose(1, 0, 2)
    w_r = jnp.concatenate([jnp.broadcast_to(w_group[None], (G, D_MODEL, G)), w_e,
                           jnp.zeros((G, D_MODEL, LANES - G - E), F32)], axis=-1)
    wr = _split_hi_lo(w_r)
    b_r = jnp.concatenate([jnp.broadcast_to(b_group[None], (G, G)), b_expert.reshape(G, E),
                           jnp.zeros((G, LANES - G - E), F32)], axis=-1).reshape(G, 1, LANES)
    wg = w_gate.reshape(G, E, D_MODEL, H).astype(BF16)
    wu = w_up.reshape(G, E, D_MODEL, H).astype(BF16)
    wd = w_down.reshape(G, E * H, D_MODEL).astype(BF16)
    nffn = norm_ffn.reshape(1, -1)

    def by_group(shape):
        return pl.BlockSpec((1,) + shape, lambda i, grp, nv: (grp[i], 0, 0))

    smem_rows = lambda f: pl.BlockSpec((1, 1, tb), f, memory_space=pltpu.SMEM)
    grid_spec = pltpu.PrefetchScalarGridSpec(
        num_scalar_prefetch=2,
        grid=(n_blocks,),
        in_specs=[smem_rows(lambda i, grp, nv: (i, 0, 0)),
                  smem_rows(lambda i, grp, nv: (jnp.minimum(i + 1, n_blocks - 1), 0, 0)),
                  smem_rows(lambda i, grp, nv: (jnp.minimum(i + 2, n_blocks - 1), 0, 0)),
                  smem_rows(lambda i, grp, nv: (jnp.maximum(i - 1, 0), 0, 0)),
                  pl.BlockSpec(memory_space=pl.ANY),
                  pl.BlockSpec((1, D_MODEL), lambda i, grp, nv: (0, 0)),
                  by_group((D_MODEL, 2 * LANES)), by_group((1, LANES)),
                  pl.BlockSpec((1, E, D_MODEL, H), lambda i, grp, nv: (grp[i], 0, 0, 0)),
                  pl.BlockSpec((1, E, D_MODEL, H), lambda i, grp, nv: (grp[i], 0, 0, 0)),
                  by_group((E * H, D_MODEL))],
        out_specs=pl.BlockSpec(memory_space=pl.ANY),
        scratch_shapes=[pltpu.VMEM((MOE_SLOTS, tb, D_MODEL), F32), pltpu.VMEM((MOE_SLOTS, tb, D_MODEL), F32),
                        pltpu.SemaphoreType.DMA((MOE_SLOTS,)), pltpu.SemaphoreType.DMA((MOE_SLOTS,))])
    return pl.pallas_call(
        _moe_kernel,
        grid_spec=grid_spec,
        out_shape=jax.ShapeDtypeStruct((T, D_MODEL), F32),
        compiler_params=_params(("arbitrary",)),
        name="experts",
    )(block_group, n_valid, src, src, src, src, x2, nffn, wr, b_r, wg, wu, wd)


def _pick(n, pref):
    t = min(pref, n)
    while n % t:
        t //= 2
    return t


def _tiles(B, S):
    T = B * S
    return dict(
        inproj_rows=_pick(T, 512),
        mla_q_rows=_pick(S, 512),
        mla_k_rows=_pick(S, 1024),
        cross_rows=_pick(S, 2 * CROSS_SUB_ROWS),
        moe_rows=_pick(T, 256),
    )


def kernel(x, mem, positions, norm_mix, w_in, mla_q_a_norm, mla_w_q_up, mla_kv_a_norm, mla_w_kv_up, mla_q_norm, mla_k_norm, hg_lb_logits, hg_o_norm, w_out, norm_cross, norm_mem, x_w_q, x_w_kv, x_q_norm, x_k_norm, x_w_o, norm_ffn, moe_w_group, moe_b_group, moe_w_expert, moe_b_expert, moe_w_gate, moe_w_up, moe_w_down):
    B, S, D = x.shape
    assert D == D_MODEL and w_in.shape[0] == 1 and S % HG_CHUNK == 0
    T = B * S
    x2d = x.reshape(T, D)
    pos2d = positions.reshape(T, 1).astype(jnp.int32)
    tiles = _tiles(B, S)

    q, k, v, hq, hff, hfb, hi, hg = _inproj(
        x2d, pos2d, norm_mix[0], w_in[0], mla_q_a_norm[0], mla_w_q_up[0], mla_kv_a_norm[0], mla_w_kv_up[0],
        mla_q_norm[0], mla_k_norm[0], tm=tiles["inproj_rows"])
    a = _mla_attention(q, k, v, B, S, tq=tiles["mla_q_rows"], tk=tiles["mla_k_rows"])
    r = _hgrn(hq, hff, hfb, hi, hg, hg_lb_logits, hg_o_norm[0], B, S)
    kx, vx = _memkv(mem, norm_mem[0], x_w_kv[0], x_k_norm[0])
    x2, gsel = _cross(x2d, a, r, w_out[0], norm_cross[0], x_w_q[0], x_q_norm[0], kx, vx, x_w_o[0], norm_ffn[0],
                      moe_w_group[0], moe_b_group[0], B, S, tm=tiles["cross_rows"])
    out = _moe(x2, gsel, norm_ffn[0], moe_w_group[0], moe_b_group[0], moe_w_expert[0], moe_b_expert[0],
               moe_w_gate[0], moe_w_up[0], moe_w_down[0], tb=tiles["moe_rows"])
    return out.reshape(B, S, D)
```

```python
import functools
import math

import jax
import jax.numpy as jnp
from jax import lax
from jax.experimental import pallas as pl
from jax.experimental.pallas import tpu as pltpu

F32 = jnp.float32
BF16 = jnp.bfloat16

LANES = 128
SUBLANES = 8
VMEM_LIMIT_BYTES = 56 * 1024 * 1024

NORM_EPS = 1e-6
LOG2E = math.log2(math.e)

D_MODEL = 1024
MLA_HEADS = 8
MLA_NOPE = 64
MLA_ROPE = 32
MLA_QK = MLA_NOPE + MLA_ROPE
MLA_V = 64
MLA_Q_RANK = 192
MLA_Q_RANK_PAD = 256
MLA_KV_RANK = 128
MLA_HEADS_PER_STEP = 4
ROPE_BASE = 10000.0
CROSS_SUB_ROWS = 256
HG_HEADS = 4
HG_DIM = 128
HG_WIDTH = HG_HEADS * HG_DIM
HG_CHUNK = 128
HG_BAND = 4
HG_HEADS_PER_STEP = 2
X_HEADS = 4
X_HEAD_DIM = D_MODEL // X_HEADS
MOE_GROUPS = 8
MOE_PER_GROUP = 8
MOE_HIDDEN = 256
MOE_GROUP_HIDDEN = MOE_PER_GROUP * MOE_HIDDEN
MOE_SLOTS = 3
MOE_ROW_WIDTH = D_MODEL + LANES
IN_SIZES = (MLA_Q_RANK, MLA_KV_RANK, MLA_ROPE, HG_WIDTH, HG_WIDTH, HG_WIDTH, HG_WIDTH, HG_WIDTH)

NT_DIMS = (((1,), (1,)), ((), ()))
TN_DIMS = (((0,), (0,)), ((), ()))


def _rms(x, n):
    return x * lax.rsqrt(jnp.sum(x * x, axis=-1, keepdims=True) * (1.0 / n) + NORM_EPS)


def _silu(x):
    return x / (1.0 + jnp.exp(-x))


def _split3(x):
    a = x.astype(BF16)
    r = x - a.astype(F32)
    b = r.astype(BF16)
    c = (r - b.astype(F32)).astype(BF16)
    return a, b, c


def _dot(a, b):
    return jnp.dot(a, b, preferred_element_type=F32)


def _params(sem):
    return pltpu.CompilerParams(dimension_semantics=sem, vmem_limit_bytes=VMEM_LIMIT_BYTES)


def _inproj_kernel(x_ref, pos_ref, nmix_ref, w_ref, qan_ref, wq_ref, kvan_ref, wk_ref, wv_ref,
                   qn_ref, kn_ref, invf_ref,
                   q_out, k_out, v_out, hq_out, hff_out, hfb_out, hi_out, hg_out):
    h = (_rms(x_ref[...], D_MODEL) * nmix_ref[...]).astype(BF16)

    p = _dot(h, w_ref[:, 0:512])
    c_q = p[:, 0:MLA_Q_RANK_PAD]
    cqn = (_rms(c_q, MLA_Q_RANK) * qan_ref[...]).astype(BF16)
    q = _dot(cqn, wq_ref[...])
    c_kv = p[:, 256:384]
    ckvn = (_rms(c_kv, MLA_KV_RANK) * kvan_ref[...]).astype(BF16)
    k_nope = _dot(ckvn, wk_ref[...])
    v_out[...] = _dot(ckvn, wv_ref[...]).astype(BF16)
    k_rope = p[:, 384:512]

    ang = pos_ref[...].astype(F32) * invf_ref[...]
    cos = jnp.cos(ang)
    sin = jnp.sin(ang)
    lane = lax.broadcasted_iota(jnp.int32, (1, LANES), 1)
    half = MLA_ROPE // 2
    sin_lo = jnp.where(lane < MLA_NOPE + half, -sin, 0.0)
    sin_hi = jnp.where(lane >= MLA_NOPE + half, sin, 0.0)

    def rope(t):
        return t * cos + pltpu.roll(t, LANES - half, 1) * sin_lo + pltpu.roll(t, half, 1) * sin_hi

    q_scale = MLA_QK ** -0.5 * LOG2E
    mixer_outs = (hq_out, hff_out, hfb_out, hi_out, hg_out)
    for hd in range(MLA_HEADS):
        sl = slice(hd * LANES, (hd + 1) * LANES)
        qh = _rms(q[:, sl], MLA_QK) * qn_ref[...]
        q_out[:, sl] = (rope(qh) * q_scale).astype(BF16)
        kh = _rms(k_nope[:, sl] + k_rope, MLA_QK) * kn_ref[...]
        k_out[:, sl] = rope(kh).astype(BF16)
        if hd < len(mixer_outs):
            out = mixer_outs[hd]
            c0 = 512 + hd * HG_WIDTH
            out[...] = _dot(h, w_ref[:, c0:c0 + HG_WIDTH]).astype(out.dtype)


def _inproj(x2d, pos2d, norm_mix, w_in, q_a_norm, w_q_up, kv_a_norm, w_kv_up, q_norm, k_norm, tm):
    T = x2d.shape[0]
    c0 = 0
    cols = []
    for size in IN_SIZES:
        cols.append(w_in[:, c0:c0 + size])
        c0 += size
    w_cq, w_ckv, w_kr, w_hq, w_hff, w_hfb, w_hi, w_hg = cols
    w_cq = jnp.pad(w_cq, ((0, 0), (0, MLA_Q_RANK_PAD - MLA_Q_RANK)))
    w_kr = jnp.pad(w_kr, ((0, 0), (MLA_NOPE, LANES - MLA_QK)))
    w_big = jnp.concatenate([w_cq, w_ckv, w_kr, w_hq, w_hff, w_hfb, w_hi, w_hg], axis=1).astype(BF16)
    n_big = w_big.shape[1]

    qan = jnp.pad(q_a_norm, (0, MLA_Q_RANK_PAD - MLA_Q_RANK)).reshape(1, -1)
    wq = w_q_up.reshape(MLA_Q_RANK, MLA_HEADS, MLA_QK)
    wq = jnp.pad(wq, ((0, MLA_Q_RANK_PAD - MLA_Q_RANK), (0, 0), (0, LANES - MLA_QK)))
    wq = wq.reshape(MLA_Q_RANK_PAD, MLA_HEADS * LANES).astype(BF16)
    wkv = w_kv_up.reshape(MLA_KV_RANK, MLA_HEADS, MLA_NOPE + MLA_V)
    wk = jnp.pad(wkv[:, :, :MLA_NOPE], ((0, 0), (0, 0), (0, LANES - MLA_NOPE)))
    wk = wk.reshape(MLA_KV_RANK, MLA_HEADS * LANES).astype(BF16)
    wv = wkv[:, :, MLA_NOPE:].reshape(MLA_KV_RANK, MLA_HEADS * MLA_V).astype(BF16)
    qn = jnp.pad(q_norm, (0, LANES - MLA_QK)).reshape(1, LANES)
    kn = jnp.pad(k_norm, (0, LANES - MLA_QK)).reshape(1, LANES)
    half = MLA_ROPE // 2
    inv_freq = 1.0 / (ROPE_BASE ** (jnp.arange(half, dtype=F32) / half))
    invf = jnp.concatenate([jnp.zeros((MLA_NOPE,), F32), inv_freq, inv_freq,
                            jnp.zeros((LANES - MLA_QK,), F32)]).reshape(1, LANES)

    def full(a):
        return pl.BlockSpec(a.shape, lambda i: (0,) * a.ndim)

    def rows(width):
        return pl.BlockSpec((tm, width), lambda i: (i, 0))

    nmix = norm_mix.reshape(1, -1)
    kvan = kv_a_norm.reshape(1, -1)
    qk_w = MLA_HEADS * LANES
    v_w = MLA_HEADS * MLA_V
    out_shape = (
        jax.ShapeDtypeStruct((T, qk_w), BF16), jax.ShapeDtypeStruct((T, qk_w), BF16),
        jax.ShapeDtypeStruct((T, v_w), BF16),
        jax.ShapeDtypeStruct((T, HG_WIDTH), BF16), jax.ShapeDtypeStruct((T, HG_WIDTH), F32),
        jax.ShapeDtypeStruct((T, HG_WIDTH), F32), jax.ShapeDtypeStruct((T, HG_WIDTH), BF16),
        jax.ShapeDtypeStruct((T, HG_WIDTH), BF16))
    return pl.pallas_call(
        _inproj_kernel,
        grid=(T // tm,),
        in_specs=[rows(D_MODEL), rows(1), full(nmix), full(w_big), full(qan), full(wq), full(kvan),
                  full(wk), full(wv), full(qn), full(kn), full(invf)],
        out_specs=(rows(qk_w), rows(qk_w), rows(v_w), rows(HG_WIDTH), rows(HG_WIDTH), rows(HG_WIDTH),
                   rows(HG_WIDTH), rows(HG_WIDTH)),
        out_shape=out_shape,
        compiler_params=_params(("parallel",)),
        name="inproj",
    )(x2d, pos2d, nmix, w_big, qan, wq, kvan, wk, wv, qn, kn, invf)


def _mla_kernel(q_ref, k_ref, v_ref, o_ref, *, tk):
    heads = MLA_HEADS_PER_STEP
    n_chunks = k_ref.shape[0] // tk
    nsub = tk // LANES
    lane = lax.broadcasted_iota(jnp.int32, (1, LANES), 1)
    own = [lane < MLA_V, lane >= MLA_V]
    m = [None] * heads
    acc = [None] * heads
    for c in range(n_chunks):
        rows = slice(c * tk, (c + 1) * tk)
        scores = [lax.dot_general(q_ref[:, j * LANES:(j + 1) * LANES], k_ref[rows, j * LANES:(j + 1) * LANES],
                                  NT_DIMS, preferred_element_type=F32) for j in range(heads)]
        for j, s in enumerate(scores):
            vv = v_ref[rows, (j // 2) * LANES:(j // 2 + 1) * LANES]
            blk_max = s[:, 0:LANES]
            for i in range(1, nsub):
                blk_max = jnp.maximum(blk_max, s[:, i * LANES:(i + 1) * LANES])
            m_new = jnp.max(blk_max, axis=-1, keepdims=True)
            if c > 0:
                m_new = jnp.maximum(m[j], m_new)
            p = jnp.exp2((s - m_new).astype(BF16))
            pv = _dot(p, jnp.where(own[j % 2], vv, jnp.ones_like(vv)))
            acc[j] = pv if c == 0 else acc[j] * jnp.exp2(m[j] - m_new) + pv
            m[j] = m_new
    for pair in range(heads // 2):
        a0, a1 = acc[2 * pair], acc[2 * pair + 1]
        o0 = a0 / a0[:, MLA_V:MLA_V + 1]
        o1 = a1 / a1[:, 0:1]
        o_ref[:, pair * LANES:(pair + 1) * LANES] = jnp.where(own[0], o0, o1).astype(BF16)


def _mla_attention(q, k, v, B, S, tq, tk):
    T = B * S
    nq = S // tq
    heads = MLA_HEADS_PER_STEP
    return pl.pallas_call(
        functools.partial(_mla_kernel, tk=tk),
        grid=(B, MLA_HEADS // heads, nq),
        in_specs=[pl.BlockSpec((tq, heads * LANES), lambda b, h, i: (b * nq + i, h)),
                  pl.BlockSpec((S, heads * LANES), lambda b, h, i: (b, h)),
                  pl.BlockSpec((S, heads * MLA_V), lambda b, h, i: (b, h))],
        out_specs=pl.BlockSpec((tq, heads * MLA_V), lambda b, h, i: (b * nq + i, h)),
        out_shape=jax.ShapeDtypeStruct((T, MLA_HEADS * MLA_V), BF16),
        compiler_params=_params(("parallel", "parallel", "arbitrary")),
        name="mla_attention",
    )(q, k, v)


def _hgrn_pair_codes(rev):
    C = HG_CHUNK
    row = lax.broadcasted_iota(jnp.int32, (C, C), 0)
    col = lax.broadcasted_iota(jnp.int32, (C, C), 1)
    dist = (col - row) if rev else (row - col)
    code = jnp.full((C, C), -1, jnp.int32)
    m, level = C // 2, HG_BAND
    levels = []
    while m >= HG_BAND:
        levels.append(m)
        m //= 2
    for j, m in enumerate(levels):
        same = (row // (2 * m)) == (col // (2 * m))
        code = jnp.where(same, HG_BAND + len(levels) - 1 - j, code)
    code = jnp.where((row // HG_BAND) == (col // HG_BAND), dist, code)
    return jnp.where(dist < 0, -1, code)


def _hgrn_chunks(chains):
    C = HG_CHUNK
    row = lax.broadcasted_iota(jnp.int32, (C, 1), 0)
    col = lax.broadcasted_iota(jnp.int32, (1, C), 1)
    n = len(chains)

    kks, fs, bs = [], [], []
    for q, z, v, lb_row, state, code, rev in chains:
        kk = (1.0 - lb_row) / (1.0 + jnp.exp(z))
        f = 1.0 - kk
        g = jnp.log2(f)
        tri = jnp.where((col >= row) if rev else (col <= row), 1.0, 0.0).astype(BF16)
        g1, g2, g3 = _split3(g)
        kks.append(kk)
        fs.append(f)
        bs.append(_dot(tri, g1) + _dot(tri, g2) + _dot(tri, g3))

    outs, states = [], []
    for (q, z, v, lb_row, state, code, rev), kk, b in zip(chains, kks, bs):
        q_hat = (q * jnp.exp2(b)).astype(BF16)
        outs.append(lax.dot_general(q_hat, state.astype(BF16), NT_DIMS, preferred_element_type=F32))
        b_end = b[0:1, :] if rev else b[C - 1:C, :]
        k_hat = (kk * jnp.exp2(b_end - b)).astype(BF16)
        states.append(state * jnp.exp2(b_end) + lax.dot_general(v, k_hat, TN_DIMS, preferred_element_type=F32))

    attns = []
    for (q, z, v, lb_row, state, code, rev), kk, f in zip(chains, kks, fs):
        step = (C - 1) if rev else 1
        u = kk
        attn = jnp.where(code == 0, jnp.sum(q * u, axis=-1, keepdims=True), 0.0)
        for d in range(1, HG_BAND):
            u = f * pltpu.roll(u, step, 0)
            attn = jnp.where(code == d, jnp.sum(q * u, axis=-1, keepdims=True), attn)
        attns.append(attn)

    m, level = HG_BAND, HG_BAND
    while m < C:
        for idx in range(n):
            q, z, v, lb_row, state, code, rev = chains[idx]
            b3 = bs[idx].reshape(C // (2 * m), 2 * m, HG_DIM)
            ref = b3[:, m:m + 1, :] if rev else b3[:, m - 1:m, :]
            e = jnp.exp2(-jnp.abs(b3 - ref)).reshape(C, HG_DIM)
            a_m = lax.dot_general((q * e).astype(BF16), (kks[idx] * e).astype(BF16), NT_DIMS,
                                  preferred_element_type=F32)
            attns[idx] = jnp.where(code == level, a_m, attns[idx])
        m, level = 2 * m, level + 1

    return [(o + _dot(attn.astype(BF16), ch[2]), st) for o, attn, ch, st in zip(outs, attns, chains, states)]


def _hgrn_kernel(hq_ref, hff_ref, hfb_ref, hi_ref, hg_ref, lbl_ref, onorm_ref, out_ref,
                 q_scr, of_scr, ob_scr, code_scr):
    C = HG_CHUNK
    n_chunks = hq_ref.shape[0] // C
    lg = lbl_ref[...]
    mx = jnp.maximum(lg[0], lg[1])
    e0 = jnp.exp(lg[0] - mx)
    lb = e0 / (e0 + jnp.exp(lg[1] - mx))
    q_scr[...] = _silu(hq_ref[...].astype(F32))
    code_scr[0] = _hgrn_pair_codes(False)
    code_scr[1] = _hgrn_pair_codes(True)

    def body(i, states):
        sf = pl.multiple_of(i * C, C)
        sb = pl.multiple_of((n_chunks - 1 - i) * C, C)
        chains = []
        for hd in range(HG_HEADS_PER_STEP):
            sl = slice(hd * HG_DIM, (hd + 1) * HG_DIM)
            chains.append((q_scr[pl.ds(sf, C), sl], hff_ref[pl.ds(sf, C), sl], hi_ref[pl.ds(sf, C), sl],
                           lb[0:1, sl], states[hd][0], code_scr[0], False))
            chains.append((q_scr[pl.ds(sb, C), sl], hfb_ref[pl.ds(sb, C), sl], hi_ref[pl.ds(sb, C), sl],
                           lb[1:2, sl], states[hd][1], code_scr[1], True))
        results = _hgrn_chunks(chains)
        new_states = []
        for hd in range(HG_HEADS_PER_STEP):
            sl = slice(hd * HG_DIM, (hd + 1) * HG_DIM)
            (o_f, st_f), (o_b, st_b) = results[2 * hd], results[2 * hd + 1]
            of_scr[pl.ds(sf, C), sl] = o_f
            ob_scr[pl.ds(sb, C), sl] = o_b
            new_states.append((st_f, st_b))
        return tuple(new_states)

    zero = jnp.zeros((HG_DIM, HG_DIM), F32)
    lax.fori_loop(0, n_chunks, body, ((zero, zero),) * HG_HEADS_PER_STEP)
    for hd in range(HG_HEADS_PER_STEP):
        sl = slice(hd * HG_DIM, (hd + 1) * HG_DIM)
        o = of_scr[:, sl] + ob_scr[:, sl]
        out_ref[:, sl] = ((_rms(o, HG_DIM) * onorm_ref[...]).astype(BF16)
                          * _silu(hg_ref[:, sl].astype(F32)).astype(BF16))


def _hgrn(hq, hff, hfb, hi, hg, lb_logits, o_norm, B, S):
    T = B * S
    width = HG_HEADS_PER_STEP * HG_DIM
    blk = pl.BlockSpec((S, width), lambda b, h: (b, h))
    n_layers = lb_logits.shape[0]
    return pl.pallas_call(
        _hgrn_kernel,
        grid=(B, HG_HEADS // HG_HEADS_PER_STEP),
        in_specs=[blk, blk, blk, blk, blk,
                  pl.BlockSpec((n_layers, 2, width), lambda b, h: (0, 0, h)),
                  pl.BlockSpec((1, HG_DIM), lambda b, h: (0, 0))],
        out_specs=blk,
        out_shape=jax.ShapeDtypeStruct((T, HG_WIDTH), BF16),
        scratch_shapes=[pltpu.VMEM((S, width), F32), pltpu.VMEM((S, width), F32), pltpu.VMEM((S, width), F32),
                        pltpu.VMEM((2, HG_CHUNK, HG_CHUNK), jnp.int32)],
        compiler_params=_params(("parallel", "parallel")),
        name="hgrn2",
    )(hq, hff, hfb, hi, hg, lb_logits, o_norm.reshape(1, HG_DIM))


def _memkv_kernel(mem_ref, nmem_ref, wkv_ref, kn_ref, k_out, v_out):
    hm = (_rms(mem_ref[0], D_MODEL) * nmem_ref[...]).astype(BF16)
    kv = _dot(hm, wkv_ref[...])
    for hd in range(X_HEADS):
        sl = slice(hd * X_HEAD_DIM, (hd + 1) * X_HEAD_DIM)
        k_out[0, :, sl] = (_rms(kv[:, sl], X_HEAD_DIM) * kn_ref[...]).astype(BF16)
    v_out[0] = kv[:, D_MODEL:].astype(BF16)


def _memkv(mem, norm_mem, w_kv, k_norm):
    B, M, _ = mem.shape
    wkv = w_kv.astype(BF16)
    blk = pl.BlockSpec((1, M, D_MODEL), lambda b: (b, 0, 0))
    return pl.pallas_call(
        _memkv_kernel,
        grid=(B,),
        in_specs=[blk, pl.BlockSpec((1, D_MODEL), lambda b: (0, 0)),
                  pl.BlockSpec(wkv.shape, lambda b: (0, 0)),
                  pl.BlockSpec((1, X_HEAD_DIM), lambda b: (0, 0))],
        out_specs=(blk, blk),
        out_shape=(jax.ShapeDtypeStruct((B, M, D_MODEL), BF16), jax.ShapeDtypeStruct((B, M, D_MODEL), BF16)),
        compiler_params=_params(("parallel",)),
        name="mem_kv",
    )(mem, norm_mem.reshape(1, -1), wkv, k_norm.reshape(1, -1))


def _router_logits(h, w_ref, bias):
    h_hi = h.astype(BF16)
    h_lo = (h - h_hi.astype(F32)).astype(BF16)
    both = _dot(h_hi, w_ref[...])
    return both[:, :LANES] + both[:, LANES:] + _dot(h_lo, w_ref[:, :LANES]) + bias


def _split_hi_lo(w):
    hi = w.astype(BF16)
    return jnp.concatenate([hi, (w - hi.astype(F32)).astype(BF16)], axis=-1)


def _first_max_lane(vals, lane_f):
    mx = jnp.max(vals, axis=-1, keepdims=True)
    idx = jnp.min(jnp.where(vals == mx, lane_f, float(LANES)), axis=-1, keepdims=True)
    return mx, idx


def _cross_kernel(x_ref, a_ref, r_ref, woa_ref, wor_ref, ncross_ref, wq_ref, qn_ref, kx_ref, vx_ref, wxo_ref,
                  nffn_ref, wg_ref, bg_ref, x2_out, key_out):
    tm = x_ref.shape[0]
    sub = min(CROSS_SUB_ROWS, tm)
    blocks = [slice(r0, r0 + sub) for r0 in range(0, tm, sub)]
    q_scale = X_HEAD_DIM ** -0.5 * LOG2E
    lane = lax.broadcasted_iota(jnp.int32, (1, LANES), 1)
    lane_f = lane.astype(F32)

    x1s = [x_ref[rows, :] + _dot(a_ref[rows, :], woa_ref[...]) + _dot(r_ref[rows, :], wor_ref[...])
           for rows in blocks]
    qxs = [_dot((_rms(x1, D_MODEL) * ncross_ref[...]).astype(BF16), wq_ref[...]) for x1 in x1s]
    heads = [[] for _ in blocks]
    for hd in range(X_HEADS):
        sl = slice(hd * X_HEAD_DIM, (hd + 1) * X_HEAD_DIM)
        for n, qx in enumerate(qxs):
            qh = (_rms(qx[:, sl], X_HEAD_DIM) * qn_ref[...] * q_scale).astype(BF16)
            s = lax.dot_general(qh, kx_ref[0, :, sl], NT_DIMS, preferred_element_type=F32)
            p = jnp.exp2(s - jnp.max(s, axis=-1, keepdims=True))
            o = _dot(p.astype(BF16), vx_ref[0, :, sl]) / jnp.sum(p, axis=-1, keepdims=True)
            heads[n].append(o.astype(BF16))
    x2s = [x1 + _dot(jnp.concatenate(hs, axis=-1), wxo_ref[...]) for x1, hs in zip(x1s, heads)]
    for rows, x2 in zip(blocks, x2s):
        x2_out[rows, 0:D_MODEL] = x2
    for rows, x2 in zip(blocks, x2s):
        h3 = _rms(x2, D_MODEL) * nffn_ref[...]
        logits = _router_logits(h3, wg_ref, bg_ref[...])
        g_logits = jnp.where(lane < MOE_GROUPS, logits, -jnp.inf)
        g_max, g_idx = _first_max_lane(g_logits, lane_f)
        g_weight = 1.0 / jnp.sum(jnp.exp(g_logits - g_max), axis=-1, keepdims=True)
        first = MOE_GROUPS + MOE_PER_GROUP * g_idx
        in_group = jnp.logical_and(lane_f >= first, lane_f < first + MOE_PER_GROUP)
        e_logits = jnp.where(in_group, logits, -jnp.inf)
        e1, i1 = _first_max_lane(e_logits, lane_f)
        e2, i2 = _first_max_lane(jnp.where(lane_f == i1, -jnp.inf, e_logits), lane_f)
        t = jnp.exp(e2 - e1)
        w1 = g_weight / (1.0 + t)
        w2 = g_weight * t / (1.0 + t)
        l1 = i1 - first
        l2 = i2 - first
        x2_out[rows, D_MODEL:D_MODEL + LANES] = jnp.where(lane_f == l1, w1, jnp.where(lane_f == l2, w2, 0.0))
        pair = jnp.minimum(l1, l2) * MOE_PER_GROUP + jnp.maximum(l1, l2)
        key_out[rows, :] = (g_idx * (MOE_PER_GROUP * MOE_PER_GROUP) + pair).astype(jnp.int32)


def _cross(x2d, a, r, w_out, norm_cross, w_q, q_norm, kx, vx, w_o, norm_ffn, w_group, b_group, w_expert, b_expert,
           B, S, tm):
    T = B * S
    per_b = S // tm
    M = kx.shape[1]
    woa = w_out[:MLA_HEADS * MLA_V].astype(BF16)
    wor = w_out[MLA_HEADS * MLA_V:].astype(BF16)
    wq = w_q.astype(BF16)
    wxo = w_o.astype(BF16)
    n_router = MOE_GROUPS + MOE_GROUPS * MOE_PER_GROUP
    wg = _split_hi_lo(jnp.pad(jnp.concatenate([w_group, w_expert], axis=1), ((0, 0), (0, LANES - n_router))))
    bg = jnp.pad(jnp.concatenate([b_group, b_expert]), (0, LANES - n_router)).reshape(1, LANES)

    def full(arr):
        return pl.BlockSpec(arr.shape, lambda i: (0,) * arr.ndim)

    def rows(width):
        return pl.BlockSpec((tm, width), lambda i: (i, 0))

    ncross = norm_cross.reshape(1, -1)
    qn = q_norm.reshape(1, -1)
    nffn = norm_ffn.reshape(1, -1)
    mem_blk = pl.BlockSpec((1, M, D_MODEL), lambda i: (i // per_b, 0, 0))
    return pl.pallas_call(
        _cross_kernel,
        grid=(T // tm,),
        in_specs=[rows(D_MODEL), rows(MLA_HEADS * MLA_V), rows(HG_WIDTH), full(woa), full(wor), full(ncross),
                  full(wq), full(qn), mem_blk, mem_blk, full(wxo), full(nffn), full(wg), full(bg)],
        out_specs=(rows(MOE_ROW_WIDTH), rows(1)),
        out_shape=(jax.ShapeDtypeStruct((T, MOE_ROW_WIDTH), F32), jax.ShapeDtypeStruct((T, 1), jnp.int32)),
        compiler_params=_params(("parallel",)),
        name="cross",
    )(x2d, a, r, woa, wor, ncross, wq, qn, kx, vx, wxo, nffn, wg, bg)


def _moe_kernel(grp_ref, nvalid_ref, active_ref,
                src_ref, src_next_ref, src_next2_ref, src_prev_ref,
                x_hbm, nffn_ref, wg_ref, wu_ref, wd_ref,
                out_hbm, xbuf, ybuf, hb_scr, y_scr, gsem, ssem):
    i = pl.program_id(0)
    n_blocks = pl.num_programs(0)
    tb = xbuf.shape[1]
    slot = i % MOE_SLOTS
    slot_next2 = (i + 2) % MOE_SLOTS
    slot_prev = (i + MOE_SLOTS - 1) % MOE_SLOTS

    def n_valid_at(j):
        inside = jnp.logical_and(j >= 0, j < n_blocks)
        return jnp.where(inside, nvalid_ref[jnp.clip(j, 0, n_blocks - 1)], 0)

    n_valid = nvalid_ref[i]
    n_prev = n_valid_at(i - 1)
    n_next2 = n_valid_at(i + 2)

    def gather_row(idx_ref, r, to_slot):
        tok = idx_ref[0, 0, r]
        pltpu.make_async_copy(x_hbm.at[pl.ds(tok, 1)], xbuf.at[to_slot, pl.ds(r, 1)], gsem.at[to_slot]).start()

    def scatter_row(idx_ref, r, from_slot):
        tok = idx_ref[0, 0, r]
        pltpu.make_async_copy(ybuf.at[from_slot, pl.ds(r, 1)], out_hbm.at[pl.ds(tok, 1)], ssem.at[from_slot]).start()

    active = active_ref[i]

    def gather_all(idx_ref, to_slot):
        def body(r, carry):
            gather_row(idx_ref, r, to_slot)
            return carry
        lax.fori_loop(0, tb, body, 0)

    def scatter_some(idx_ref, from_slot, count):
        def body(r, carry):
            scatter_row(idx_ref, r, from_slot)
            return carry
        lax.fori_loop(0, count, body, 0)

    def wait_scatter(of_slot, count):
        whole = pl.multiple_of((count // SUBLANES) * SUBLANES, SUBLANES)

        @pl.when(whole > 0)
        def _():
            pltpu.make_async_copy(ybuf.at[of_slot, pl.ds(0, whole)], out_hbm.at[pl.ds(0, whole)],
                                  ssem.at[of_slot]).wait()

        def one(r, carry):
            pltpu.make_async_copy(ybuf.at[of_slot, pl.ds(0, 1)], out_hbm.at[pl.ds(0, 1)], ssem.at[of_slot]).wait()
            return carry
        lax.fori_loop(0, count - whole, one, 0)

    def evaluate(side_work):
        pltpu.make_async_copy(x_hbm.at[pl.ds(0, tb)], xbuf.at[slot], gsem.at[slot]).wait()
        side_work(0)
        x = xbuf[slot, :, 0:D_MODEL]
        hb_scr[...] = (_rms(x, D_MODEL) * nffn_ref[...]).astype(BF16)
        y_scr[...] = jnp.zeros(y_scr.shape, F32)
        for e in range(MOE_PER_GROUP):
            @pl.when(jnp.bitwise_and(jnp.right_shift(active, e), 1) == 1)
            def _(e=e):
                hb = hb_scr[...]
                gate = _dot(hb, wg_ref[0, e])
                up = _dot(hb, wu_ref[0, e])
                w_e = xbuf[slot, :, D_MODEL:MOE_ROW_WIDTH][:, e:e + 1]
                act = jnp.where(w_e != 0.0, _silu(gate) * up * w_e, 0.0).astype(BF16)
                y_scr[...] += _dot(act, wd_ref[0, e * MOE_HIDDEN:(e + 1) * MOE_HIDDEN, :])
        side_work(1)
        ybuf[slot] = xbuf[slot, :, 0:D_MODEL] + y_scr[...]

    @pl.when(i == 0)
    def _():
        @pl.when(n_valid > 0)
        def _():
            gather_all(src_ref, 0)

        @pl.when(n_valid_at(1) > 0)
        def _():
            gather_all(src_next_ref, 1)

    wait_scatter(slot, n_valid_at(i - MOE_SLOTS))

    steady = jnp.logical_and(jnp.logical_and(n_valid > 0, n_next2 > 0), n_prev == tb)
    share = tb // 2

    @pl.when(steady)
    def _():
        def side_work(part):
            for r in range(part * share, (part + 1) * share):
                gather_row(src_next2_ref, r, slot_next2)
                scatter_row(src_prev_ref, r, slot_prev)
        evaluate(side_work)

    @pl.when(jnp.logical_not(steady))
    def _():
        @pl.when(n_next2 > 0)
        def _():
            gather_all(src_next2_ref, slot_next2)
        scatter_some(src_prev_ref, slot_prev, n_prev)

        @pl.when(n_valid > 0)
        def _():
            evaluate(lambda part: None)

    @pl.when(i == n_blocks - 1)
    def _():
        wait_scatter((i + MOE_SLOTS - 2) % MOE_SLOTS, n_valid_at(i - 2))
        wait_scatter(slot_prev, n_prev)
        scatter_some(src_ref, slot, n_valid)
        wait_scatter(slot, n_valid)


def _moe(x2, key, norm_ffn, w_gate, w_up, w_down, tb):
    T = x2.shape[0]
    G, E, H = MOE_GROUPS, MOE_PER_GROUP, MOE_HIDDEN
    n_blocks = T // tb + G

    key = key.reshape(T)
    g = key // (E * E)
    order = jnp.argsort(key, stable=True).astype(jnp.int32)
    counts = jnp.sum(g[:, None] == jnp.arange(G, dtype=jnp.int32)[None, :], axis=0).astype(jnp.int32)
    blocks_per_group = (counts + tb - 1) // tb
    block_end = jnp.cumsum(blocks_per_group)
    block_start = block_end - blocks_per_group
    token_start = jnp.cumsum(counts) - counts
    bi = jnp.arange(n_blocks, dtype=jnp.int32)
    block_group = jnp.minimum(jnp.searchsorted(block_end, bi, side='right'), G - 1).astype(jnp.int32)
    first_row = (bi - block_start[block_group]) * tb
    n_valid = jnp.clip(counts[block_group] - first_row, 0, tb)
    n_valid = jnp.where(bi < block_end[G - 1], n_valid, 0).astype(jnp.int32)
    r = jnp.arange(tb, dtype=jnp.int32)[None, :]
    pos = token_start[block_group][:, None] + first_row[:, None] + r
    valid = r < n_valid[:, None]
    src = jnp.where(valid, order[jnp.clip(pos, 0, T - 1)], 0).astype(jnp.int32)
    pair = key[src] % (E * E)
    used = jnp.left_shift(1, pair // E) | jnp.left_shift(1, pair % E)
    active = lax.reduce(jnp.where(valid, used, 0).astype(jnp.int32), jnp.int32(0), lax.bitwise_or, (1,))
    src = src.reshape(n_blocks, 1, tb)

    wg = w_gate.reshape(G, E, D_MODEL, H).astype(BF16)
    wu = w_up.reshape(G, E, D_MODEL, H).astype(BF16)
    wd = w_down.reshape(G, E * H, D_MODEL).astype(BF16)
    nffn = norm_ffn.reshape(1, -1)

    smem_rows = lambda f: pl.BlockSpec((1, 1, tb), f, memory_space=pltpu.SMEM)
    grid_spec = pltpu.PrefetchScalarGridSpec(
        num_scalar_prefetch=3,
        grid=(n_blocks,),
        in_specs=[smem_rows(lambda i, grp, nv, act: (i, 0, 0)),
                  smem_rows(lambda i, grp, nv, act: (jnp.minimum(i + 1, n_blocks - 1), 0, 0)),
                  smem_rows(lambda i, grp, nv, act: (jnp.minimum(i + 2, n_blocks - 1), 0, 0)),
                  smem_rows(lambda i, grp, nv, act: (jnp.maximum(i - 1, 0), 0, 0)),
                  pl.BlockSpec(memory_space=pl.ANY),
                  pl.BlockSpec((1, D_MODEL), lambda i, grp, nv, act: (0, 0)),
                  pl.BlockSpec((1, E, D_MODEL, H), lambda i, grp, nv, act: (grp[i], 0, 0, 0)),
                  pl.BlockSpec((1, E, D_MODEL, H), lambda i, grp, nv, act: (grp[i], 0, 0, 0)),
                  pl.BlockSpec((1, E * H, D_MODEL), lambda i, grp, nv, act: (grp[i], 0, 0))],
        out_specs=pl.BlockSpec(memory_space=pl.ANY),
        scratch_shapes=[pltpu.VMEM((MOE_SLOTS, tb, MOE_ROW_WIDTH), F32), pltpu.VMEM((MOE_SLOTS, tb, D_MODEL), F32),
                        pltpu.VMEM((tb, D_MODEL), BF16), pltpu.VMEM((tb, D_MODEL), F32),
                        pltpu.SemaphoreType.DMA((MOE_SLOTS,)), pltpu.SemaphoreType.DMA((MOE_SLOTS,))])
    return pl.pallas_call(
        _moe_kernel,
        grid_spec=grid_spec,
        out_shape=jax.ShapeDtypeStruct((T, D_MODEL), F32),
        compiler_params=_params(("arbitrary",)),
        name="experts",
    )(block_group, n_valid, active, src, src, src, src, x2, nffn, wg, wu, wd)


def _pick(n, pref):
    t = min(pref, n)
    while n % t:
        t //= 2
    return t


def _tiles(B, S):
    T = B * S
    return dict(
        inproj_rows=_pick(T, 512),
        mla_q_rows=_pick(S, 512),
        mla_k_rows=_pick(S, 1024),
        cross_rows=_pick(S, 2 * CROSS_SUB_ROWS),
        moe_rows=_pick(T, 256),
    )


def kernel(x, mem, positions, norm_mix, w_in, mla_q_a_norm, mla_w_q_up, mla_kv_a_norm, mla_w_kv_up, mla_q_norm, mla_k_norm, hg_lb_logits, hg_o_norm, w_out, norm_cross, norm_mem, x_w_q, x_w_kv, x_q_norm, x_k_norm, x_w_o, norm_ffn, moe_w_group, moe_b_group, moe_w_expert, moe_b_expert, moe_w_gate, moe_w_up, moe_w_down):
    B, S, D = x.shape
    assert D == D_MODEL and w_in.shape[0] == 1 and S % HG_CHUNK == 0
    T = B * S
    x2d = x.reshape(T, D)
    pos2d = positions.reshape(T, 1).astype(jnp.int32)
    tiles = _tiles(B, S)

    q, k, v, hq, hff, hfb, hi, hg = _inproj(
        x2d, pos2d, norm_mix[0], w_in[0], mla_q_a_norm[0], mla_w_q_up[0], mla_kv_a_norm[0], mla_w_kv_up[0],
        mla_q_norm[0], mla_k_norm[0], tm=tiles["inproj_rows"])
    a = _mla_attention(q, k, v, B, S, tq=tiles["mla_q_rows"], tk=tiles["mla_k_rows"])
    r = _hgrn(hq, hff, hfb, hi, hg, hg_lb_logits, hg_o_norm[0], B, S)
    kx, vx = _memkv(mem, norm_mem[0], x_w_kv[0], x_k_norm[0])
    x2, key = _cross(x2d, a, r, w_out[0], norm_cross[0], x_w_q[0], x_q_norm[0], kx, vx, x_w_o[0], norm_ffn[0],
                     moe_w_group[0], moe_b_group[0], moe_w_expert[0], moe_b_expert[0], B, S, tm=tiles["cross_rows"])
    out = _moe(x2, key, norm_ffn[0], moe_w_gate[0], moe_w_up[0], moe_w_down[0], tb=tiles["moe_rows"])
    return out.reshape(B, S, D)
```

```python
import functools
import math

import jax
import jax.numpy as jnp
from jax import lax
from jax.experimental import pallas as pl
from jax.experimental.pallas import tpu as pltpu

F32 = jnp.float32
BF16 = jnp.bfloat16

LANES = 128
SUBLANES = 8
VMEM_LIMIT_BYTES = 56 * 1024 * 1024

NORM_EPS = 1e-6
LOG2E = math.log2(math.e)

D_MODEL = 1024
MLA_HEADS = 8
MLA_NOPE = 64
MLA_ROPE = 32
MLA_QK = MLA_NOPE + MLA_ROPE
MLA_V = 64
MLA_Q_RANK = 192
MLA_Q_RANK_PAD = 256
MLA_KV_RANK = 128
MLA_HEADS_PER_STEP = 4
ROPE_BASE = 10000.0
CROSS_SUB_ROWS = 256
HG_HEADS = 4
HG_DIM = 128
HG_WIDTH = HG_HEADS * HG_DIM
HG_CHUNK = 128
HG_BAND = 4
HG_HEADS_PER_STEP = 2
X_HEADS = 4
X_HEAD_DIM = D_MODEL // X_HEADS
MOE_GROUPS = 8
MOE_PER_GROUP = 8
MOE_HIDDEN = 256
MOE_GROUP_HIDDEN = MOE_PER_GROUP * MOE_HIDDEN
MOE_SLOTS = 3
MOE_ROW_WIDTH = D_MODEL + LANES
IN_SIZES = (MLA_Q_RANK, MLA_KV_RANK, MLA_ROPE, HG_WIDTH, HG_WIDTH, HG_WIDTH, HG_WIDTH, HG_WIDTH)

NT_DIMS = (((1,), (1,)), ((), ()))
TN_DIMS = (((0,), (0,)), ((), ()))


def _rms(x, n):
    return x * lax.rsqrt(jnp.sum(x * x, axis=-1, keepdims=True) * (1.0 / n) + NORM_EPS)


def _silu(x):
    return x / (1.0 + jnp.exp(-x))


def _split3(x):
    a = x.astype(BF16)
    r = x - a.astype(F32)
    b = r.astype(BF16)
    c = (r - b.astype(F32)).astype(BF16)
    return a, b, c


def _dot(a, b):
    return jnp.dot(a, b, preferred_element_type=F32)


def _params(sem):
    return pltpu.CompilerParams(dimension_semantics=sem, vmem_limit_bytes=VMEM_LIMIT_BYTES)


def _inproj_kernel(x_ref, pos_ref, nmix_ref, w_ref, qan_ref, wq_ref, kvan_ref, wk_ref, wv_ref,
                   qn_ref, kn_ref, invf_ref,
                   q_out, k_out, v_out, hq_out, hff_out, hfb_out, hi_out, hg_out):
    h = (_rms(x_ref[...], D_MODEL) * nmix_ref[...]).astype(BF16)

    p = _dot(h, w_ref[:, 0:512])
    c_q = p[:, 0:MLA_Q_RANK_PAD]
    cqn = (_rms(c_q, MLA_Q_RANK) * qan_ref[...]).astype(BF16)
    q = _dot(cqn, wq_ref[...])
    c_kv = p[:, 256:384]
    ckvn = (_rms(c_kv, MLA_KV_RANK) * kvan_ref[...]).astype(BF16)
    k_nope = _dot(ckvn, wk_ref[...])
    v_out[...] = _dot(ckvn, wv_ref[...]).astype(BF16)
    k_rope = p[:, 384:512]

    ang = pos_ref[...].astype(F32) * invf_ref[...]
    cos = jnp.cos(ang)
    sin = jnp.sin(ang)
    lane = lax.broadcasted_iota(jnp.int32, (1, LANES), 1)
    half = MLA_ROPE // 2
    sin_lo = jnp.where(lane < MLA_NOPE + half, -sin, 0.0)
    sin_hi = jnp.where(lane >= MLA_NOPE + half, sin, 0.0)

    def rope(t):
        return t * cos + pltpu.roll(t, LANES - half, 1) * sin_lo + pltpu.roll(t, half, 1) * sin_hi

    q_scale = MLA_QK ** -0.5 * LOG2E
    mixer_outs = (hq_out, hff_out, hfb_out, hi_out, hg_out)
    for hd in range(MLA_HEADS):
        sl = slice(hd * LANES, (hd + 1) * LANES)
        qh = _rms(q[:, sl], MLA_QK) * qn_ref[...]
        q_out[:, sl] = (rope(qh) * q_scale).astype(BF16)
        kh = _rms(k_nope[:, sl] + k_rope, MLA_QK) * kn_ref[...]
        k_out[:, sl] = rope(kh).astype(BF16)
        if hd < len(mixer_outs):
            out = mixer_outs[hd]
            c0 = 512 + hd * HG_WIDTH
            out[...] = _dot(h, w_ref[:, c0:c0 + HG_WIDTH]).astype(out.dtype)


def _inproj(x2d, pos2d, norm_mix, w_in, q_a_norm, w_q_up, kv_a_norm, w_kv_up, q_norm, k_norm, tm):
    T = x2d.shape[0]
    c0 = 0
    cols = []
    for size in IN_SIZES:
        cols.append(w_in[:, c0:c0 + size])
        c0 += size
    w_cq, w_ckv, w_kr, w_hq, w_hff, w_hfb, w_hi, w_hg = cols
    w_cq = jnp.pad(w_cq, ((0, 0), (0, MLA_Q_RANK_PAD - MLA_Q_RANK)))
    w_kr = jnp.pad(w_kr, ((0, 0), (MLA_NOPE, LANES - MLA_QK)))
    w_big = jnp.concatenate([w_cq, w_ckv, w_kr, w_hq, w_hff, w_hfb, w_hi, w_hg], axis=1).astype(BF16)
    n_big = w_big.shape[1]

    qan = jnp.pad(q_a_norm, (0, MLA_Q_RANK_PAD - MLA_Q_RANK)).reshape(1, -1)
    wq = w_q_up.reshape(MLA_Q_RANK, MLA_HEADS, MLA_QK)
    wq = jnp.pad(wq, ((0, MLA_Q_RANK_PAD - MLA_Q_RANK), (0, 0), (0, LANES - MLA_QK)))
    wq = wq.reshape(MLA_Q_RANK_PAD, MLA_HEADS * LANES).astype(BF16)
    wkv = w_kv_up.reshape(MLA_KV_RANK, MLA_HEADS, MLA_NOPE + MLA_V)
    wk = jnp.pad(wkv[:, :, :MLA_NOPE], ((0, 0), (0, 0), (0, LANES - MLA_NOPE)))
    wk = wk.reshape(MLA_KV_RANK, MLA_HEADS * LANES).astype(BF16)
    wv = wkv[:, :, MLA_NOPE:].reshape(MLA_KV_RANK, MLA_HEADS * MLA_V).astype(BF16)
    qn = jnp.pad(q_norm, (0, LANES - MLA_QK)).reshape(1, LANES)
    kn = jnp.pad(k_norm, (0, LANES - MLA_QK)).reshape(1, LANES)
    half = MLA_ROPE // 2
    inv_freq = 1.0 / (ROPE_BASE ** (jnp.arange(half, dtype=F32) / half))
    invf = jnp.concatenate([jnp.zeros((MLA_NOPE,), F32), inv_freq, inv_freq,
                            jnp.zeros((LANES - MLA_QK,), F32)]).reshape(1, LANES)

    def full(a):
        return pl.BlockSpec(a.shape, lambda i: (0,) * a.ndim)

    def rows(width):
        return pl.BlockSpec((tm, width), lambda i: (i, 0))

    nmix = norm_mix.reshape(1, -1)
    kvan = kv_a_norm.reshape(1, -1)
    qk_w = MLA_HEADS * LANES
    v_w = MLA_HEADS * MLA_V
    out_shape = (
        jax.ShapeDtypeStruct((T, qk_w), BF16), jax.ShapeDtypeStruct((T, qk_w), BF16),
        jax.ShapeDtypeStruct((T, v_w), BF16),
        jax.ShapeDtypeStruct((T, HG_WIDTH), BF16), jax.ShapeDtypeStruct((T, HG_WIDTH), F32),
        jax.ShapeDtypeStruct((T, HG_WIDTH), F32), jax.ShapeDtypeStruct((T, HG_WIDTH), BF16),
        jax.ShapeDtypeStruct((T, HG_WIDTH), BF16))
    return pl.pallas_call(
        _inproj_kernel,
        grid=(T // tm,),
        in_specs=[rows(D_MODEL), rows(1), full(nmix), full(w_big), full(qan), full(wq), full(kvan),
                  full(wk), full(wv), full(qn), full(kn), full(invf)],
        out_specs=(rows(qk_w), rows(qk_w), rows(v_w), rows(HG_WIDTH), rows(HG_WIDTH), rows(HG_WIDTH),
                   rows(HG_WIDTH), rows(HG_WIDTH)),
        out_shape=out_shape,
        compiler_params=_params(("parallel",)),
        name="inproj",
    )(x2d, pos2d, nmix, w_big, qan, wq, kvan, wk, wv, qn, kn, invf)


def _mla_kernel(q_ref, k_ref, v_ref, o_ref, *, tk):
    heads = MLA_HEADS_PER_STEP
    n_chunks = k_ref.shape[0] // tk
    nsub = tk // LANES
    lane = lax.broadcasted_iota(jnp.int32, (1, LANES), 1)
    own = [lane < MLA_V, lane >= MLA_V]
    m = [None] * heads
    acc = [None] * heads
    for c in range(n_chunks):
        rows = slice(c * tk, (c + 1) * tk)
        scores = [lax.dot_general(q_ref[:, j * LANES:(j + 1) * LANES], k_ref[rows, j * LANES:(j + 1) * LANES],
                                  NT_DIMS, preferred_element_type=F32) for j in range(heads)]
        for j, s in enumerate(scores):
            vv = v_ref[rows, (j // 2) * LANES:(j // 2 + 1) * LANES]
            blk_max = s[:, 0:LANES]
            for i in range(1, nsub):
                blk_max = jnp.maximum(blk_max, s[:, i * LANES:(i + 1) * LANES])
            m_new = jnp.max(blk_max, axis=-1, keepdims=True)
            if c > 0:
                m_new = jnp.maximum(m[j], m_new)
            p = jnp.exp2((s - m_new).astype(BF16))
            pv = _dot(p, jnp.where(own[j % 2], vv, jnp.ones_like(vv)))
            acc[j] = pv if c == 0 else acc[j] * jnp.exp2(m[j] - m_new) + pv
            m[j] = m_new
    for pair in range(heads // 2):
        a0, a1 = acc[2 * pair], acc[2 * pair + 1]
        o0 = a0 / a0[:, MLA_V:MLA_V + 1]
        o1 = a1 / a1[:, 0:1]
        o_ref[:, pair * LANES:(pair + 1) * LANES] = jnp.where(own[0], o0, o1).astype(BF16)


def _mla_attention(q, k, v, B, S, tq, tk):
    T = B * S
    nq = S // tq
    heads = MLA_HEADS_PER_STEP
    return pl.pallas_call(
        functools.partial(_mla_kernel, tk=tk),
        grid=(B, MLA_HEADS // heads, nq),
        in_specs=[pl.BlockSpec((tq, heads * LANES), lambda b, h, i: (b * nq + i, h)),
                  pl.BlockSpec((S, heads * LANES), lambda b, h, i: (b, h)),
                  pl.BlockSpec((S, heads * MLA_V), lambda b, h, i: (b, h))],
        out_specs=pl.BlockSpec((tq, heads * MLA_V), lambda b, h, i: (b * nq + i, h)),
        out_shape=jax.ShapeDtypeStruct((T, MLA_HEADS * MLA_V), BF16),
        compiler_params=_params(("parallel", "parallel", "arbitrary")),
        name="mla_attention",
    )(q, k, v)


def _hgrn_pair_codes(rev):
    C = HG_CHUNK
    row = lax.broadcasted_iota(jnp.int32, (C, C), 0)
    col = lax.broadcasted_iota(jnp.int32, (C, C), 1)
    dist = (col - row) if rev else (row - col)
    code = jnp.full((C, C), -1, jnp.int32)
    m, level = C // 2, HG_BAND
    levels = []
    while m >= HG_BAND:
        levels.append(m)
        m //= 2
    for j, m in enumerate(levels):
        same = (row // (2 * m)) == (col // (2 * m))
        code = jnp.where(same, HG_BAND + len(levels) - 1 - j, code)
    code = jnp.where((row // HG_BAND) == (col // HG_BAND), dist, code)
    return jnp.where(dist < 0, -1, code)


def _hgrn_chunks(chains):
    C = HG_CHUNK
    row = lax.broadcasted_iota(jnp.int32, (C, 1), 0)
    col = lax.broadcasted_iota(jnp.int32, (1, C), 1)
    n = len(chains)

    kks, fs, bs = [], [], []
    for q, z, v, lb_row, state, code, rev in chains:
        kk = (1.0 - lb_row) / (1.0 + jnp.exp(z))
        f = 1.0 - kk
        g = jnp.log2(f)
        tri = jnp.where((col >= row) if rev else (col <= row), 1.0, 0.0).astype(BF16)
        g1, g2, g3 = _split3(g)
        kks.append(kk)
        fs.append(f)
        bs.append(_dot(tri, g1) + _dot(tri, g2) + _dot(tri, g3))

    outs, states = [], []
    for (q, z, v, lb_row, state, code, rev), kk, b in zip(chains, kks, bs):
        q_hat = (q * jnp.exp2(b)).astype(BF16)
        outs.append(lax.dot_general(q_hat, state.astype(BF16), NT_DIMS, preferred_element_type=F32))
        b_end = b[0:1, :] if rev else b[C - 1:C, :]
        k_hat = (kk * jnp.exp2(b_end - b)).astype(BF16)
        states.append(state * jnp.exp2(b_end) + lax.dot_general(v, k_hat, TN_DIMS, preferred_element_type=F32))

    attns = []
    for (q, z, v, lb_row, state, code, rev), kk, f in zip(chains, kks, fs):
        step = (C - 1) if rev else 1
        u = kk
        attn = jnp.where(code == 0, jnp.sum(q * u, axis=-1, keepdims=True), 0.0)
        for d in range(1, HG_BAND):
            u = f * pltpu.roll(u, step, 0)
            attn = jnp.where(code == d, jnp.sum(q * u, axis=-1, keepdims=True), attn)
        attns.append(attn)

    m, level = HG_BAND, HG_BAND
    while m < C:
        for idx in range(n):
            q, z, v, lb_row, state, code, rev = chains[idx]
            b3 = bs[idx].reshape(C // (2 * m), 2 * m, HG_DIM)
            ref = b3[:, m:m + 1, :] if rev else b3[:, m - 1:m, :]
            e = jnp.exp2(-jnp.abs(b3 - ref)).reshape(C, HG_DIM)
            a_m = lax.dot_general((q * e).astype(BF16), (kks[idx] * e).astype(BF16), NT_DIMS,
                                  preferred_element_type=F32)
            attns[idx] = jnp.where(code == level, a_m, attns[idx])
        m, level = 2 * m, level + 1

    return [(o + _dot(attn.astype(BF16), ch[2]), st) for o, attn, ch, st in zip(outs, attns, chains, states)]


def _hgrn_kernel(hq_ref, hff_ref, hfb_ref, hi_ref, hg_ref, lbl_ref, onorm_ref, out_ref,
                 q_scr, of_scr, ob_scr, code_scr):
    C = HG_CHUNK
    n_chunks = hq_ref.shape[0] // C
    lg = lbl_ref[...]
    mx = jnp.maximum(lg[0], lg[1])
    e0 = jnp.exp(lg[0] - mx)
    lb = e0 / (e0 + jnp.exp(lg[1] - mx))
    q_scr[...] = _silu(hq_ref[...].astype(F32))
    code_scr[0] = _hgrn_pair_codes(False)
    code_scr[1] = _hgrn_pair_codes(True)

    def body(i, states):
        sf = pl.multiple_of(i * C, C)
        sb = pl.multiple_of((n_chunks - 1 - i) * C, C)
        chains = []
        for hd in range(HG_HEADS_PER_STEP):
            sl = slice(hd * HG_DIM, (hd + 1) * HG_DIM)
            chains.append((q_scr[pl.ds(sf, C), sl], hff_ref[pl.ds(sf, C), sl], hi_ref[pl.ds(sf, C), sl],
                           lb[0:1, sl], states[hd][0], code_scr[0], False))
            chains.append((q_scr[pl.ds(sb, C), sl], hfb_ref[pl.ds(sb, C), sl], hi_ref[pl.ds(sb, C), sl],
                           lb[1:2, sl], states[hd][1], code_scr[1], True))
        results = _hgrn_chunks(chains)
        new_states = []
        for hd in range(HG_HEADS_PER_STEP):
            sl = slice(hd * HG_DIM, (hd + 1) * HG_DIM)
            (o_f, st_f), (o_b, st_b) = results[2 * hd], results[2 * hd + 1]
            of_scr[pl.ds(sf, C), sl] = o_f
            ob_scr[pl.ds(sb, C), sl] = o_b
            new_states.append((st_f, st_b))
        return tuple(new_states)

    zero = jnp.zeros((HG_DIM, HG_DIM), F32)
    lax.fori_loop(0, n_chunks, body, ((zero, zero),) * HG_HEADS_PER_STEP)
    for hd in range(HG_HEADS_PER_STEP):
        sl = slice(hd * HG_DIM, (hd + 1) * HG_DIM)
        o = of_scr[:, sl] + ob_scr[:, sl]
        out_ref[:, sl] = ((_rms(o, HG_DIM) * onorm_ref[...]).astype(BF16)
                          * _silu(hg_ref[:, sl].astype(F32)).astype(BF16))


def _hgrn(hq, hff, hfb, hi, hg, lb_logits, o_norm, B, S):
    T = B * S
    width = HG_HEADS_PER_STEP * HG_DIM
    blk = pl.BlockSpec((S, width), lambda b, h: (b, h))
    n_layers = lb_logits.shape[0]
    return pl.pallas_call(
        _hgrn_kernel,
        grid=(B, HG_HEADS // HG_HEADS_PER_STEP),
        in_specs=[blk, blk, blk, blk, blk,
                  pl.BlockSpec((n_layers, 2, width), lambda b, h: (0, 0, h)),
                  pl.BlockSpec((1, HG_DIM), lambda b, h: (0, 0))],
        out_specs=blk,
        out_shape=jax.ShapeDtypeStruct((T, HG_WIDTH), BF16),
        scratch_shapes=[pltpu.VMEM((S, width), F32), pltpu.VMEM((S, width), F32), pltpu.VMEM((S, width), F32),
                        pltpu.VMEM((2, HG_CHUNK, HG_CHUNK), jnp.int32)],
        compiler_params=_params(("parallel", "parallel")),
        name="hgrn2",
    )(hq, hff, hfb, hi, hg, lb_logits, o_norm.reshape(1, HG_DIM))


def _memkv_kernel(mem_ref, nmem_ref, wkv_ref, kn_ref, k_out, v_out):
    hm = (_rms(mem_ref[0], D_MODEL) * nmem_ref[...]).astype(BF16)
    kv = _dot(hm, wkv_ref[...])
    for hd in range(X_HEADS):
        sl = slice(hd * X_HEAD_DIM, (hd + 1) * X_HEAD_DIM)
        k_out[0, :, sl] = (_rms(kv[:, sl], X_HEAD_DIM) * kn_ref[...]).astype(BF16)
    v_out[0] = kv[:, D_MODEL:].astype(BF16)


def _memkv(mem, norm_mem, w_kv, k_norm):
    B, M, _ = mem.shape
    wkv = w_kv.astype(BF16)
    blk = pl.BlockSpec((1, M, D_MODEL), lambda b: (b, 0, 0))
    return pl.pallas_call(
        _memkv_kernel,
        grid=(B,),
        in_specs=[blk, pl.BlockSpec((1, D_MODEL), lambda b: (0, 0)),
                  pl.BlockSpec(wkv.shape, lambda b: (0, 0)),
                  pl.BlockSpec((1, X_HEAD_DIM), lambda b: (0, 0))],
        out_specs=(blk, blk),
        out_shape=(jax.ShapeDtypeStruct((B, M, D_MODEL), BF16), jax.ShapeDtypeStruct((B, M, D_MODEL), BF16)),
        compiler_params=_params(("parallel",)),
        name="mem_kv",
    )(mem, norm_mem.reshape(1, -1), wkv, k_norm.reshape(1, -1))


def _router_logits(h, w_ref, bias):
    h_hi = h.astype(BF16)
    h_lo = (h - h_hi.astype(F32)).astype(BF16)
    both = _dot(h_hi, w_ref[...])
    return both[:, :LANES] + both[:, LANES:] + _dot(h_lo, w_ref[:, :LANES]) + bias


def _split_hi_lo(w):
    hi = w.astype(BF16)
    return jnp.concatenate([hi, (w - hi.astype(F32)).astype(BF16)], axis=-1)


def _first_max_lane(vals, lane_f):
    mx = jnp.max(vals, axis=-1, keepdims=True)
    idx = jnp.min(jnp.where(vals == mx, lane_f, float(LANES)), axis=-1, keepdims=True)
    return mx, idx


def _cross_kernel(x_ref, a_ref, r_ref, woa_ref, wor_ref, ncross_ref, wq_ref, qn_ref, kx_ref, vx_ref, wxo_ref,
                  nffn_ref, wg_ref, bg_ref, x2_out, key_out):
    tm = x_ref.shape[0]
    sub = min(CROSS_SUB_ROWS, tm)
    blocks = [slice(r0, r0 + sub) for r0 in range(0, tm, sub)]
    q_scale = X_HEAD_DIM ** -0.5 * LOG2E
    lane = lax.broadcasted_iota(jnp.int32, (1, LANES), 1)
    lane_f = lane.astype(F32)

    x1s = [x_ref[rows, :] + _dot(a_ref[rows, :], woa_ref[...]) + _dot(r_ref[rows, :], wor_ref[...])
           for rows in blocks]
    qxs = [_dot((_rms(x1, D_MODEL) * ncross_ref[...]).astype(BF16), wq_ref[...]) for x1 in x1s]
    heads = [[] for _ in blocks]
    for hd in range(X_HEADS):
        sl = slice(hd * X_HEAD_DIM, (hd + 1) * X_HEAD_DIM)
        for n, qx in enumerate(qxs):
            qh = (_rms(qx[:, sl], X_HEAD_DIM) * qn_ref[...] * q_scale).astype(BF16)
            s = lax.dot_general(qh, kx_ref[0, :, sl], NT_DIMS, preferred_element_type=F32)
            p = jnp.exp2(s - jnp.max(s, axis=-1, keepdims=True))
            o = _dot(p.astype(BF16), vx_ref[0, :, sl]) / jnp.sum(p, axis=-1, keepdims=True)
            heads[n].append(o.astype(BF16))
    x2s = [x1 + _dot(jnp.concatenate(hs, axis=-1), wxo_ref[...]) for x1, hs in zip(x1s, heads)]
    for rows, x2 in zip(blocks, x2s):
        x2_out[rows, 0:D_MODEL] = x2
    for rows, x2 in zip(blocks, x2s):
        h3 = _rms(x2, D_MODEL) * nffn_ref[...]
        logits = _router_logits(h3, wg_ref, bg_ref[...])
        g_logits = jnp.where(lane < MOE_GROUPS, logits, -jnp.inf)
        g_max, g_idx = _first_max_lane(g_logits, lane_f)
        g_weight = 1.0 / jnp.sum(jnp.exp(g_logits - g_max), axis=-1, keepdims=True)
        first = MOE_GROUPS + MOE_PER_GROUP * g_idx
        in_group = jnp.logical_and(lane_f >= first, lane_f < first + MOE_PER_GROUP)
        e_logits = jnp.where(in_group, logits, -jnp.inf)
        e1, i1 = _first_max_lane(e_logits, lane_f)
        e2, i2 = _first_max_lane(jnp.where(lane_f == i1, -jnp.inf, e_logits), lane_f)
        t = jnp.exp(e2 - e1)
        w1 = g_weight / (1.0 + t)
        w2 = g_weight * t / (1.0 + t)
        l1 = i1 - first
        l2 = i2 - first
        x2_out[rows, D_MODEL:D_MODEL + LANES] = jnp.where(lane_f == l1, w1, jnp.where(lane_f == l2, w2, 0.0))
        pair = jnp.minimum(l1, l2) * MOE_PER_GROUP + jnp.maximum(l1, l2)
        key_out[rows, :] = (g_idx * (MOE_PER_GROUP * MOE_PER_GROUP) + pair).astype(jnp.int32)


def _cross(x2d, a, r, w_out, norm_cross, w_q, q_norm, kx, vx, w_o, norm_ffn, w_group, b_group, w_expert, b_expert,
           B, S, tm):
    T = B * S
    per_b = S // tm
    M = kx.shape[1]
    woa = w_out[:MLA_HEADS * MLA_V].astype(BF16)
    wor = w_out[MLA_HEADS * MLA_V:].astype(BF16)
    wq = w_q.astype(BF16)
    wxo = w_o.astype(BF16)
    n_router = MOE_GROUPS + MOE_GROUPS * MOE_PER_GROUP
    wg = _split_hi_lo(jnp.pad(jnp.concatenate([w_group, w_expert], axis=1), ((0, 0), (0, LANES - n_router))))
    bg = jnp.pad(jnp.concatenate([b_group, b_expert]), (0, LANES - n_router)).reshape(1, LANES)

    def full(arr):
        return pl.BlockSpec(arr.shape, lambda i: (0,) * arr.ndim)

    def rows(width):
        return pl.BlockSpec((tm, width), lambda i: (i, 0))

    ncross = norm_cross.reshape(1, -1)
    qn = q_norm.reshape(1, -1)
    nffn = norm_ffn.reshape(1, -1)
    mem_blk = pl.BlockSpec((1, M, D_MODEL), lambda i: (i // per_b, 0, 0))
    return pl.pallas_call(
        _cross_kernel,
        grid=(T // tm,),
        in_specs=[rows(D_MODEL), rows(MLA_HEADS * MLA_V), rows(HG_WIDTH), full(woa), full(wor), full(ncross),
                  full(wq), full(qn), mem_blk, mem_blk, full(wxo), full(nffn), full(wg), full(bg)],
        out_specs=(rows(MOE_ROW_WIDTH), rows(1)),
        out_shape=(jax.ShapeDtypeStruct((T, MOE_ROW_WIDTH), F32), jax.ShapeDtypeStruct((T, 1), jnp.int32)),
        compiler_params=_params(("parallel",)),
        name="cross",
    )(x2d, a, r, woa, wor, ncross, wq, qn, kx, vx, wxo, nffn, wg, bg)


def _moe_kernel(grp_ref, nvalid_ref, nactive_ref, elist_ref,
                src_ref, src_next_ref, src_next2_ref, src_prev_ref,
                x_hbm, nffn_ref, wg_ref, wu_ref, wd_ref,
                out_hbm, xbuf, ybuf, hb_scr, y_scr, gsem, ssem):
    i = pl.program_id(0)
    n_blocks = pl.num_programs(0)
    tb = xbuf.shape[1]
    slot = i % MOE_SLOTS
    slot_next2 = (i + 2) % MOE_SLOTS
    slot_prev = (i + MOE_SLOTS - 1) % MOE_SLOTS

    def n_valid_at(j):
        inside = jnp.logical_and(j >= 0, j < n_blocks)
        return jnp.where(inside, nvalid_ref[jnp.clip(j, 0, n_blocks - 1)], 0)

    n_valid = nvalid_ref[i]
    n_prev = n_valid_at(i - 1)
    n_next2 = n_valid_at(i + 2)

    def gather_row(idx_ref, r, to_slot):
        tok = idx_ref[0, 0, r]
        pltpu.make_async_copy(x_hbm.at[pl.ds(tok, 1)], xbuf.at[to_slot, pl.ds(r, 1)], gsem.at[to_slot]).start()

    def scatter_row(idx_ref, r, from_slot):
        tok = idx_ref[0, 0, r]
        pltpu.make_async_copy(ybuf.at[from_slot, pl.ds(r, 1)], out_hbm.at[pl.ds(tok, 1)], ssem.at[from_slot]).start()

    n_active = nactive_ref[i]

    def gather_all(idx_ref, to_slot):
        def body(r, carry):
            gather_row(idx_ref, r, to_slot)
            return carry
        lax.fori_loop(0, tb, body, 0)

    def scatter_some(idx_ref, from_slot, count):
        def body(r, carry):
            scatter_row(idx_ref, r, from_slot)
            return carry
        lax.fori_loop(0, count, body, 0)

    def wait_scatter(of_slot, count):
        whole = pl.multiple_of((count // SUBLANES) * SUBLANES, SUBLANES)

        @pl.when(whole > 0)
        def _():
            pltpu.make_async_copy(ybuf.at[of_slot, pl.ds(0, whole)], out_hbm.at[pl.ds(0, whole)],
                                  ssem.at[of_slot]).wait()

        def one(r, carry):
            pltpu.make_async_copy(ybuf.at[of_slot, pl.ds(0, 1)], out_hbm.at[pl.ds(0, 1)], ssem.at[of_slot]).wait()
            return carry
        lax.fori_loop(0, count - whole, one, 0)

    def evaluate(side_work):
        pltpu.make_async_copy(x_hbm.at[pl.ds(0, tb)], xbuf.at[slot], gsem.at[slot]).wait()
        x = xbuf[slot, :, 0:D_MODEL]
        hb_scr[...] = (_rms(x, D_MODEL) * nffn_ref[...]).astype(BF16)
        lane = lax.broadcasted_iota(jnp.int32, (1, LANES), 1)

        def expert(k, first):
            e = elist_ref[i * MOE_PER_GROUP + k]
            hb = hb_scr[...]
            gate = _dot(hb, wg_ref[0, e])
            up = _dot(hb, wu_ref[0, e])
            slab = xbuf[slot, :, D_MODEL:MOE_ROW_WIDTH]
            w_e = jnp.sum(jnp.where(lane == e, slab, 0.0), axis=-1, keepdims=True)
            act = jnp.where(w_e != 0.0, _silu(gate) * up * w_e, 0.0).astype(BF16)
            rows = pl.ds(pl.multiple_of(e * MOE_HIDDEN, MOE_HIDDEN), MOE_HIDDEN)
            y = _dot(act, wd_ref[0, rows, :])
            y_scr[...] = y if first else y_scr[...] + y

        side_work(0)
        expert(0, True)
        side_work(1)
        expert(1, False)
        for k in range(2, MOE_PER_GROUP):
            @pl.when(k < n_active)
            def _(k=k):
                expert(k, False)
        ybuf[slot] = xbuf[slot, :, 0:D_MODEL] + y_scr[...]

    @pl.when(i == 0)
    def _():
        @pl.when(n_valid > 0)
        def _():
            gather_all(src_ref, 0)

        @pl.when(n_valid_at(1) > 0)
        def _():
            gather_all(src_next_ref, 1)

    wait_scatter(slot, n_valid_at(i - MOE_SLOTS))

    steady = jnp.logical_and(jnp.logical_and(n_valid > 0, n_next2 > 0), n_prev == tb)
    share = tb // 2

    @pl.when(steady)
    def _():
        def side_work(part):
            for r in range(part * share, (part + 1) * share):
                gather_row(src_next2_ref, r, slot_next2)
                scatter_row(src_prev_ref, r, slot_prev)
        evaluate(side_work)

    @pl.when(jnp.logical_not(steady))
    def _():
        @pl.when(n_next2 > 0)
        def _():
            gather_all(src_next2_ref, slot_next2)
        scatter_some(src_prev_ref, slot_prev, n_prev)

        @pl.when(n_valid > 0)
        def _():
            evaluate(lambda part: None)

    @pl.when(i == n_blocks - 1)
    def _():
        wait_scatter((i + MOE_SLOTS - 2) % MOE_SLOTS, n_valid_at(i - 2))
        wait_scatter(slot_prev, n_prev)
        scatter_some(src_ref, slot, n_valid)
        wait_scatter(slot, n_valid)


def _moe(x2, key, norm_ffn, w_gate, w_up, w_down, tb):
    T = x2.shape[0]
    G, E, H = MOE_GROUPS, MOE_PER_GROUP, MOE_HIDDEN
    n_blocks = T // tb + G

    key = key.reshape(T)
    g = key // (E * E)
    order = jnp.argsort(key, stable=True).astype(jnp.int32)
    counts = jnp.sum(g[:, None] == jnp.arange(G, dtype=jnp.int32)[None, :], axis=0).astype(jnp.int32)
    blocks_per_group = (counts + tb - 1) // tb
    block_end = jnp.cumsum(blocks_per_group)
    block_start = block_end - blocks_per_group
    token_start = jnp.cumsum(counts) - counts
    bi = jnp.arange(n_blocks, dtype=jnp.int32)
    block_group = jnp.minimum(jnp.searchsorted(block_end, bi, side='right'), G - 1).astype(jnp.int32)
    first_row = (bi - block_start[block_group]) * tb
    n_valid = jnp.clip(counts[block_group] - first_row, 0, tb)
    n_valid = jnp.where(bi < block_end[G - 1], n_valid, 0).astype(jnp.int32)
    r = jnp.arange(tb, dtype=jnp.int32)[None, :]
    pos = token_start[block_group][:, None] + first_row[:, None] + r
    valid = r < n_valid[:, None]
    src = jnp.where(valid, order[jnp.clip(pos, 0, T - 1)], 0).astype(jnp.int32)
    pair = key[src] % (E * E)
    used = jnp.left_shift(1, pair // E) | jnp.left_shift(1, pair % E)
    active = lax.reduce(jnp.where(valid, used, 0).astype(jnp.int32), jnp.int32(0), lax.bitwise_or, (1,))
    is_used = jnp.bitwise_and(jnp.right_shift(active[:, None], jnp.arange(E, dtype=jnp.int32)[None, :]), 1)
    n_active = jnp.sum(is_used, axis=1).astype(jnp.int32)
    expert_list = jnp.argsort(1 - is_used, axis=1, stable=True).astype(jnp.int32).reshape(n_blocks * E)
    src = src.reshape(n_blocks, 1, tb)

    wg = w_gate.reshape(G, E, D_MODEL, H).astype(BF16)
    wu = w_up.reshape(G, E, D_MODEL, H).astype(BF16)
    wd = w_down.reshape(G, E * H, D_MODEL).astype(BF16)
    nffn = norm_ffn.reshape(1, -1)

    smem_rows = lambda f: pl.BlockSpec((1, 1, tb), f, memory_space=pltpu.SMEM)
    grid_spec = pltpu.PrefetchScalarGridSpec(
        num_scalar_prefetch=4,
        grid=(n_blocks,),
        in_specs=[smem_rows(lambda i, grp, nv, nact, elist: (i, 0, 0)),
                  smem_rows(lambda i, grp, nv, nact, elist: (jnp.minimum(i + 1, n_blocks - 1), 0, 0)),
                  smem_rows(lambda i, grp, nv, nact, elist: (jnp.minimum(i + 2, n_blocks - 1), 0, 0)),
                  smem_rows(lambda i, grp, nv, nact, elist: (jnp.maximum(i - 1, 0), 0, 0)),
                  pl.BlockSpec(memory_space=pl.ANY),
                  pl.BlockSpec((1, D_MODEL), lambda i, grp, nv, nact, elist: (0, 0)),
                  pl.BlockSpec((1, E, D_MODEL, H), lambda i, grp, nv, nact, elist: (grp[i], 0, 0, 0)),
                  pl.BlockSpec((1, E, D_MODEL, H), lambda i, grp, nv, nact, elist: (grp[i], 0, 0, 0)),
                  pl.BlockSpec((1, E * H, D_MODEL), lambda i, grp, nv, nact, elist: (grp[i], 0, 0))],
        out_specs=pl.BlockSpec(memory_space=pl.ANY),
        scratch_shapes=[pltpu.VMEM((MOE_SLOTS, tb, MOE_ROW_WIDTH), F32), pltpu.VMEM((MOE_SLOTS, tb, D_MODEL), F32),
                        pltpu.VMEM((tb, D_MODEL), BF16), pltpu.VMEM((tb, D_MODEL), F32),
                        pltpu.SemaphoreType.DMA((MOE_SLOTS,)), pltpu.SemaphoreType.DMA((MOE_SLOTS,))])
    return pl.pallas_call(
        _moe_kernel,
        grid_spec=grid_spec,
        out_shape=jax.ShapeDtypeStruct((T, D_MODEL), F32),
        compiler_params=_params(("arbitrary",)),
        name="experts",
    )(block_group, n_valid, n_active, expert_list, src, src, src, src, x2, nffn, wg, wu, wd)


def _pick(n, pref):
    t = min(pref, n)
    while n % t:
        t //= 2
    return t


def _tiles(B, S):
    T = B * S
    return dict(
        inproj_rows=_pick(T, 512),
        mla_q_rows=_pick(S, 512),
        mla_k_rows=_pick(S, 1024),
        cross_rows=_pick(S, 2 * CROSS_SUB_ROWS),
        moe_rows=_pick(T, 256),
    )


def kernel(x, mem, positions, norm_mix, w_in, mla_q_a_norm, mla_w_q_up, mla_kv_a_norm, mla_w_kv_up, mla_q_norm, mla_k_norm, hg_lb_logits, hg_o_norm, w_out, norm_cross, norm_mem, x_w_q, x_w_kv, x_q_norm, x_k_norm, x_w_o, norm_ffn, moe_w_group, moe_b_group, moe_w_expert, moe_b_expert, moe_w_gate, moe_w_up, moe_w_down):
    B, S, D = x.shape
    assert D == D_MODEL and w_in.shape[0] == 1 and S % HG_CHUNK == 0
    T = B * S
    x2d = x.reshape(T, D)
    pos2d = positions.reshape(T, 1).astype(jnp.int32)
    tiles = _tiles(B, S)

    q, k, v, hq, hff, hfb, hi, hg = _inproj(
        x2d, pos2d, norm_mix[0], w_in[0], mla_q_a_norm[0], mla_w_q_up[0], mla_kv_a_norm[0], mla_w_kv_up[0],
        mla_q_norm[0], mla_k_norm[0], tm=tiles["inproj_rows"])
    a = _mla_attention(q, k, v, B, S, tq=tiles["mla_q_rows"], tk=tiles["mla_k_rows"])
    r = _hgrn(hq, hff, hfb, hi, hg, hg_lb_logits, hg_o_norm[0], B, S)
    kx, vx = _memkv(mem, norm_mem[0], x_w_kv[0], x_k_norm[0])
    x2, key = _cross(x2d, a, r, w_out[0], norm_cross[0], x_w_q[0], x_q_norm[0], kx, vx, x_w_o[0], norm_ffn[0],
                     moe_w_group[0], moe_b_group[0], moe_w_expert[0], moe_b_expert[0], B, S, tm=tiles["cross_rows"])
    out = _moe(x2, key, norm_ffn[0], moe_w_gate[0], moe_w_up[0], moe_w_down[0], tb=tiles["moe_rows"])
    return out.reshape(B, S, D)
```

```python
import functools
import math

import jax
import jax.numpy as jnp
from jax import lax
from jax.experimental import pallas as pl
from jax.experimental.pallas import tpu as pltpu

F32 = jnp.float32
BF16 = jnp.bfloat16

LANES = 128
SUBLANES = 8
VMEM_LIMIT_BYTES = 56 * 1024 * 1024

NORM_EPS = 1e-6
LOG2E = math.log2(math.e)

D_MODEL = 1024
MLA_HEADS = 8
MLA_NOPE = 64
MLA_ROPE = 32
MLA_QK = MLA_NOPE + MLA_ROPE
MLA_V = 64
MLA_Q_RANK = 192
MLA_Q_RANK_PAD = 256
MLA_KV_RANK = 128
MLA_HEADS_PER_STEP = 4
ROPE_BASE = 10000.0
CROSS_SUB_ROWS = 256
HG_HEADS = 4
HG_DIM = 128
HG_WIDTH = HG_HEADS * HG_DIM
HG_CHUNK = 128
HG_BAND = 4
HG_HEADS_PER_STEP = 2
HG_MIN_GATE = 2.0 ** -100
X_HEADS = 4
X_HEAD_DIM = D_MODEL // X_HEADS
MOE_GROUPS = 8
MOE_PER_GROUP = 8
MOE_HIDDEN = 256
MOE_GROUP_HIDDEN = MOE_PER_GROUP * MOE_HIDDEN
MOE_SLOTS = 3
MOE_ROW_WIDTH = D_MODEL + LANES
IN_SIZES = (MLA_Q_RANK, MLA_KV_RANK, MLA_ROPE, HG_WIDTH, HG_WIDTH, HG_WIDTH, HG_WIDTH, HG_WIDTH)

NT_DIMS = (((1,), (1,)), ((), ()))
TN_DIMS = (((0,), (0,)), ((), ()))


def _rms(x, n):
    return x * lax.rsqrt(jnp.sum(x * x, axis=-1, keepdims=True) * (1.0 / n) + NORM_EPS)


def _silu(x):
    return x / (1.0 + jnp.exp(-x))


def _split3(x):
    a = x.astype(BF16)
    r = x - a.astype(F32)
    b = r.astype(BF16)
    c = (r - b.astype(F32)).astype(BF16)
    return a, b, c


def _dot(a, b):
    return jnp.dot(a, b, preferred_element_type=F32)


def _params(sem):
    return pltpu.CompilerParams(dimension_semantics=sem, vmem_limit_bytes=VMEM_LIMIT_BYTES)


def _inproj_kernel(x_ref, pos_ref, nmix_ref, w_ref, qan_ref, wq_ref, kvan_ref, wk_ref, wv_ref,
                   qn_ref, kn_ref, invf_ref,
                   q_out, k_out, v_out, hq_out, hff_out, hfb_out, hi_out, hg_out):
    h = (_rms(x_ref[...], D_MODEL) * nmix_ref[...]).astype(BF16)

    p = _dot(h, w_ref[:, 0:512])
    c_q = p[:, 0:MLA_Q_RANK_PAD]
    cqn = (_rms(c_q, MLA_Q_RANK) * qan_ref[...]).astype(BF16)
    q = _dot(cqn, wq_ref[...])
    c_kv = p[:, 256:384]
    ckvn = (_rms(c_kv, MLA_KV_RANK) * kvan_ref[...]).astype(BF16)
    k_nope = _dot(ckvn, wk_ref[...])
    v_out[...] = _dot(ckvn, wv_ref[...]).astype(BF16)
    k_rope = p[:, 384:512]

    ang = pos_ref[...].astype(F32) * invf_ref[...]
    cos = jnp.cos(ang)
    sin = jnp.sin(ang)
    lane = lax.broadcasted_iota(jnp.int32, (1, LANES), 1)
    half = MLA_ROPE // 2
    sin_lo = jnp.where(lane < MLA_NOPE + half, -sin, 0.0)
    sin_hi = jnp.where(lane >= MLA_NOPE + half, sin, 0.0)

    def rope(t):
        return t * cos + pltpu.roll(t, LANES - half, 1) * sin_lo + pltpu.roll(t, half, 1) * sin_hi

    q_scale = MLA_QK ** -0.5 * LOG2E
    mixer_outs = (hq_out, hff_out, hfb_out, hi_out, hg_out)
    for hd in range(MLA_HEADS):
        sl = slice(hd * LANES, (hd + 1) * LANES)
        qh = _rms(q[:, sl], MLA_QK) * qn_ref[...]
        q_out[:, sl] = (rope(qh) * q_scale).astype(BF16)
        kh = _rms(k_nope[:, sl] + k_rope, MLA_QK) * kn_ref[...]
        k_out[:, sl] = rope(kh).astype(BF16)
        if hd < len(mixer_outs):
            out = mixer_outs[hd]
            c0 = 512 + hd * HG_WIDTH
            out[...] = _dot(h, w_ref[:, c0:c0 + HG_WIDTH]).astype(out.dtype)


def _inproj(x2d, pos2d, norm_mix, w_in, q_a_norm, w_q_up, kv_a_norm, w_kv_up, q_norm, k_norm, tm):
    T = x2d.shape[0]
    c0 = 0
    cols = []
    for size in IN_SIZES:
        cols.append(w_in[:, c0:c0 + size])
        c0 += size
    w_cq, w_ckv, w_kr, w_hq, w_hff, w_hfb, w_hi, w_hg = cols
    w_cq = jnp.pad(w_cq, ((0, 0), (0, MLA_Q_RANK_PAD - MLA_Q_RANK)))
    w_kr = jnp.pad(w_kr, ((0, 0), (MLA_NOPE, LANES - MLA_QK)))
    w_big = jnp.concatenate([w_cq, w_ckv, w_kr, w_hq, w_hff, w_hfb, w_hi, w_hg], axis=1).astype(BF16)
    n_big = w_big.shape[1]

    qan = jnp.pad(q_a_norm, (0, MLA_Q_RANK_PAD - MLA_Q_RANK)).reshape(1, -1)
    wq = w_q_up.reshape(MLA_Q_RANK, MLA_HEADS, MLA_QK)
    wq = jnp.pad(wq, ((0, MLA_Q_RANK_PAD - MLA_Q_RANK), (0, 0), (0, LANES - MLA_QK)))
    wq = wq.reshape(MLA_Q_RANK_PAD, MLA_HEADS * LANES).astype(BF16)
    wkv = w_kv_up.reshape(MLA_KV_RANK, MLA_HEADS, MLA_NOPE + MLA_V)
    wk = jnp.pad(wkv[:, :, :MLA_NOPE], ((0, 0), (0, 0), (0, LANES - MLA_NOPE)))
    wk = wk.reshape(MLA_KV_RANK, MLA_HEADS * LANES).astype(BF16)
    wv = wkv[:, :, MLA_NOPE:].reshape(MLA_KV_RANK, MLA_HEADS * MLA_V).astype(BF16)
    qn = jnp.pad(q_norm, (0, LANES - MLA_QK)).reshape(1, LANES)
    kn = jnp.pad(k_norm, (0, LANES - MLA_QK)).reshape(1, LANES)
    half = MLA_ROPE // 2
    inv_freq = 1.0 / (ROPE_BASE ** (jnp.arange(half, dtype=F32) / half))
    invf = jnp.concatenate([jnp.zeros((MLA_NOPE,), F32), inv_freq, inv_freq,
                            jnp.zeros((LANES - MLA_QK,), F32)]).reshape(1, LANES)

    def full(a):
        return pl.BlockSpec(a.shape, lambda i: (0,) * a.ndim)

    def rows(width):
        return pl.BlockSpec((tm, width), lambda i: (i, 0))

    nmix = norm_mix.reshape(1, -1)
    kvan = kv_a_norm.reshape(1, -1)
    qk_w = MLA_HEADS * LANES
    v_w = MLA_HEADS * MLA_V
    out_shape = (
        jax.ShapeDtypeStruct((T, qk_w), BF16), jax.ShapeDtypeStruct((T, qk_w), BF16),
        jax.ShapeDtypeStruct((T, v_w), BF16),
        jax.ShapeDtypeStruct((T, HG_WIDTH), BF16), jax.ShapeDtypeStruct((T, HG_WIDTH), F32),
        jax.ShapeDtypeStruct((T, HG_WIDTH), F32), jax.ShapeDtypeStruct((T, HG_WIDTH), BF16),
        jax.ShapeDtypeStruct((T, HG_WIDTH), BF16))
    return pl.pallas_call(
        _inproj_kernel,
        grid=(T // tm,),
        in_specs=[rows(D_MODEL), rows(1), full(nmix), full(w_big), full(qan), full(wq), full(kvan),
                  full(wk), full(wv), full(qn), full(kn), full(invf)],
        out_specs=(rows(qk_w), rows(qk_w), rows(v_w), rows(HG_WIDTH), rows(HG_WIDTH), rows(HG_WIDTH),
                   rows(HG_WIDTH), rows(HG_WIDTH)),
        out_shape=out_shape,
        compiler_params=_params(("parallel",)),
        name="inproj",
    )(x2d, pos2d, nmix, w_big, qan, wq, kvan, wk, wv, qn, kn, invf)


def _mla_kernel(q_ref, k_ref, v_ref, o_ref, *, tk):
    heads = MLA_HEADS_PER_STEP
    n_chunks = k_ref.shape[0] // tk
    nsub = tk // LANES
    lane = lax.broadcasted_iota(jnp.int32, (1, LANES), 1)
    own = [lane < MLA_V, lane >= MLA_V]
    m = [None] * heads
    acc = [None] * heads
    for c in range(n_chunks):
        rows = slice(c * tk, (c + 1) * tk)
        scores = [lax.dot_general(q_ref[:, j * LANES:(j + 1) * LANES], k_ref[rows, j * LANES:(j + 1) * LANES],
                                  NT_DIMS, preferred_element_type=F32) for j in range(heads)]
        for j, s in enumerate(scores):
            vv = v_ref[rows, (j // 2) * LANES:(j // 2 + 1) * LANES]
            blk_max = s[:, 0:LANES]
            for i in range(1, nsub):
                blk_max = jnp.maximum(blk_max, s[:, i * LANES:(i + 1) * LANES])
            m_new = jnp.max(blk_max, axis=-1, keepdims=True)
            if c > 0:
                m_new = jnp.maximum(m[j], m_new)
            p = jnp.exp2((s - m_new).astype(BF16))
            pv = _dot(p, jnp.where(own[j % 2], vv, jnp.ones_like(vv)))
            acc[j] = pv if c == 0 else acc[j] * jnp.exp2(m[j] - m_new) + pv
            m[j] = m_new
    for pair in range(heads // 2):
        a0, a1 = acc[2 * pair], acc[2 * pair + 1]
        o0 = a0 / a0[:, MLA_V:MLA_V + 1]
        o1 = a1 / a1[:, 0:1]
        o_ref[:, pair * LANES:(pair + 1) * LANES] = jnp.where(own[0], o0, o1).astype(BF16)


def _mla_attention(q, k, v, B, S, tq, tk):
    T = B * S
    nq = S // tq
    heads = MLA_HEADS_PER_STEP
    return pl.pallas_call(
        functools.partial(_mla_kernel, tk=tk),
        grid=(B, MLA_HEADS // heads, nq),
        in_specs=[pl.BlockSpec((tq, heads * LANES), lambda b, h, i: (b * nq + i, h)),
                  pl.BlockSpec((S, heads * LANES), lambda b, h, i: (b, h)),
                  pl.BlockSpec((S, heads * MLA_V), lambda b, h, i: (b, h))],
        out_specs=pl.BlockSpec((tq, heads * MLA_V), lambda b, h, i: (b * nq + i, h)),
        out_shape=jax.ShapeDtypeStruct((T, MLA_HEADS * MLA_V), BF16),
        compiler_params=_params(("parallel", "parallel", "arbitrary")),
        name="mla_attention",
    )(q, k, v)


def _hgrn_pair_codes(rev):
    C = HG_CHUNK
    row = lax.broadcasted_iota(jnp.int32, (C, C), 0)
    col = lax.broadcasted_iota(jnp.int32, (C, C), 1)
    dist = (col - row) if rev else (row - col)
    code = jnp.full((C, C), -1, jnp.int32)
    m, level = C // 2, HG_BAND
    levels = []
    while m >= HG_BAND:
        levels.append(m)
        m //= 2
    for j, m in enumerate(levels):
        same = (row // (2 * m)) == (col // (2 * m))
        code = jnp.where(same, HG_BAND + len(levels) - 1 - j, code)
    code = jnp.where((row // HG_BAND) == (col // HG_BAND), dist, code)
    return jnp.where(dist < 0, -1, code)


def _hgrn_chunks(chains):
    C = HG_CHUNK
    row = lax.broadcasted_iota(jnp.int32, (C, 1), 0)
    col = lax.broadcasted_iota(jnp.int32, (1, C), 1)
    n = len(chains)

    kks, fs, bs = [], [], []
    for q, z, v, lb_row, state, code, rev in chains:
        kk = (1.0 - lb_row) / (1.0 + jnp.exp(z))
        f = jnp.maximum(1.0 - kk, HG_MIN_GATE)
        g = jnp.log2(f)
        tri = jnp.where((col >= row) if rev else (col <= row), 1.0, 0.0).astype(BF16)
        g1, g2, g3 = _split3(g)
        kks.append(kk)
        fs.append(f)
        bs.append(_dot(tri, g1) + _dot(tri, g2) + _dot(tri, g3))

    outs, states = [], []
    for (q, z, v, lb_row, state, code, rev), kk, b in zip(chains, kks, bs):
        q_hat = (q * jnp.exp2(b)).astype(BF16)
        outs.append(lax.dot_general(q_hat, state.astype(BF16), NT_DIMS, preferred_element_type=F32))
        b_end = b[0:1, :] if rev else b[C - 1:C, :]
        k_hat = (kk * jnp.exp2(b_end - b)).astype(BF16)
        states.append(state * jnp.exp2(b_end) + lax.dot_general(v, k_hat, TN_DIMS, preferred_element_type=F32))

    attns = []
    for (q, z, v, lb_row, state, code, rev), kk, f in zip(chains, kks, fs):
        step = (C - 1) if rev else 1
        u = kk
        attn = jnp.where(code == 0, jnp.sum(q * u, axis=-1, keepdims=True), 0.0)
        for d in range(1, HG_BAND):
            u = f * pltpu.roll(u, step, 0)
            attn = jnp.where(code == d, jnp.sum(q * u, axis=-1, keepdims=True), attn)
        attns.append(attn)

    m, level = HG_BAND, HG_BAND
    while m < C:
        for idx in range(n):
            q, z, v, lb_row, state, code, rev = chains[idx]
            b3 = bs[idx].reshape(C // (2 * m), 2 * m, HG_DIM)
            ref = b3[:, m:m + 1, :] if rev else b3[:, m - 1:m, :]
            e = jnp.exp2(-jnp.abs(b3 - ref)).reshape(C, HG_DIM)
            a_m = lax.dot_general((q * e).astype(BF16), (kks[idx] * e).astype(BF16), NT_DIMS,
                                  preferred_element_type=F32)
            attns[idx] = jnp.where(code == level, a_m, attns[idx])
        m, level = 2 * m, level + 1

    return [(o + _dot(attn.astype(BF16), ch[2]), st) for o, attn, ch, st in zip(outs, attns, chains, states)]


def _hgrn_kernel(hq_ref, hff_ref, hfb_ref, hi_ref, hg_ref, lbl_ref, onorm_ref, out_ref,
                 q_scr, of_scr, ob_scr, code_scr):
    C = HG_CHUNK
    n_chunks = hq_ref.shape[0] // C
    lg = lbl_ref[...]
    mx = jnp.maximum(lg[0], lg[1])
    e0 = jnp.exp(lg[0] - mx)
    lb = e0 / (e0 + jnp.exp(lg[1] - mx))
    q_scr[...] = _silu(hq_ref[...].astype(F32))
    code_scr[0] = _hgrn_pair_codes(False)
    code_scr[1] = _hgrn_pair_codes(True)

    def body(i, states):
        sf = pl.multiple_of(i * C, C)
        sb = pl.multiple_of((n_chunks - 1 - i) * C, C)
        chains = []
        for hd in range(HG_HEADS_PER_STEP):
            sl = slice(hd * HG_DIM, (hd + 1) * HG_DIM)
            chains.append((q_scr[pl.ds(sf, C), sl], hff_ref[pl.ds(sf, C), sl], hi_ref[pl.ds(sf, C), sl],
                           lb[0:1, sl], states[hd][0], code_scr[0], False))
            chains.append((q_scr[pl.ds(sb, C), sl], hfb_ref[pl.ds(sb, C), sl], hi_ref[pl.ds(sb, C), sl],
                           lb[1:2, sl], states[hd][1], code_scr[1], True))
        results = _hgrn_chunks(chains)
        new_states = []
        for hd in range(HG_HEADS_PER_STEP):
            sl = slice(hd * HG_DIM, (hd + 1) * HG_DIM)
            (o_f, st_f), (o_b, st_b) = results[2 * hd], results[2 * hd + 1]
            of_scr[pl.ds(sf, C), sl] = o_f
            ob_scr[pl.ds(sb, C), sl] = o_b
            new_states.append((st_f, st_b))
        return tuple(new_states)

    zero = jnp.zeros((HG_DIM, HG_DIM), F32)
    lax.fori_loop(0, n_chunks, body, ((zero, zero),) * HG_HEADS_PER_STEP)
    for hd in range(HG_HEADS_PER_STEP):
        sl = slice(hd * HG_DIM, (hd + 1) * HG_DIM)
        o = of_scr[:, sl] + ob_scr[:, sl]
        out_ref[:, sl] = ((_rms(o, HG_DIM) * onorm_ref[...]).astype(BF16)
                          * _silu(hg_ref[:, sl].astype(F32)).astype(BF16))


def _hgrn(hq, hff, hfb, hi, hg, lb_logits, o_norm, B, S):
    T = B * S
    width = HG_HEADS_PER_STEP * HG_DIM
    blk = pl.BlockSpec((S, width), lambda b, h: (b, h))
    n_layers = lb_logits.shape[0]
    return pl.pallas_call(
        _hgrn_kernel,
        grid=(B, HG_HEADS // HG_HEADS_PER_STEP),
        in_specs=[blk, blk, blk, blk, blk,
                  pl.BlockSpec((n_layers, 2, width), lambda b, h: (0, 0, h)),
                  pl.BlockSpec((1, HG_DIM), lambda b, h: (0, 0))],
        out_specs=blk,
        out_shape=jax.ShapeDtypeStruct((T, HG_WIDTH), BF16),
        scratch_shapes=[pltpu.VMEM((S, width), F32), pltpu.VMEM((S, width), F32), pltpu.VMEM((S, width), F32),
                        pltpu.VMEM((2, HG_CHUNK, HG_CHUNK), jnp.int32)],
        compiler_params=_params(("parallel", "parallel")),
        name="hgrn2",
    )(hq, hff, hfb, hi, hg, lb_logits, o_norm.reshape(1, HG_DIM))


def _memkv_kernel(mem_ref, nmem_ref, wkv_ref, kn_ref, k_out, v_out):
    hm = (_rms(mem_ref[0], D_MODEL) * nmem_ref[...]).astype(BF16)
    kv = _dot(hm, wkv_ref[...])
    for hd in range(X_HEADS):
        sl = slice(hd * X_HEAD_DIM, (hd + 1) * X_HEAD_DIM)
        k_out[0, :, sl] = (_rms(kv[:, sl], X_HEAD_DIM) * kn_ref[...]).astype(BF16)
    v_out[0] = kv[:, D_MODEL:].astype(BF16)


def _memkv(mem, norm_mem, w_kv, k_norm):
    B, M, _ = mem.shape
    wkv = w_kv.astype(BF16)
    blk = pl.BlockSpec((1, M, D_MODEL), lambda b: (b, 0, 0))
    return pl.pallas_call(
        _memkv_kernel,
        grid=(B,),
        in_specs=[blk, pl.BlockSpec((1, D_MODEL), lambda b: (0, 0)),
                  pl.BlockSpec(wkv.shape, lambda b: (0, 0)),
                  pl.BlockSpec((1, X_HEAD_DIM), lambda b: (0, 0))],
        out_specs=(blk, blk),
        out_shape=(jax.ShapeDtypeStruct((B, M, D_MODEL), BF16), jax.ShapeDtypeStruct((B, M, D_MODEL), BF16)),
        compiler_params=_params(("parallel",)),
        name="mem_kv",
    )(mem, norm_mem.reshape(1, -1), wkv, k_norm.reshape(1, -1))


def _router_logits(h, w_ref, bias):
    h_hi = h.astype(BF16)
    h_lo = (h - h_hi.astype(F32)).astype(BF16)
    both = _dot(h_hi, w_ref[...])
    return both[:, :LANES] + both[:, LANES:] + _dot(h_lo, w_ref[:, :LANES]) + bias


def _split_hi_lo(w):
    hi = w.astype(BF16)
    return jnp.concatenate([hi, (w - hi.astype(F32)).astype(BF16)], axis=-1)


def _first_max_lane(vals, lane_f):
    mx = jnp.max(vals, axis=-1, keepdims=True)
    idx = jnp.min(jnp.where(vals == mx, lane_f, float(LANES)), axis=-1, keepdims=True)
    return mx, idx


def _cross_kernel(x_ref, a_ref, r_ref, woa_ref, wor_ref, ncross_ref, wq_ref, qn_ref, kx_ref, vx_ref, wxo_ref,
                  nffn_ref, wg_ref, bg_ref, x2_out, key_out):
    tm = x_ref.shape[0]
    sub = min(CROSS_SUB_ROWS, tm)
    blocks = [slice(r0, r0 + sub) for r0 in range(0, tm, sub)]
    q_scale = X_HEAD_DIM ** -0.5 * LOG2E
    lane = lax.broadcasted_iota(jnp.int32, (1, LANES), 1)
    lane_f = lane.astype(F32)

    x1s = [x_ref[rows, :] + _dot(a_ref[rows, :], woa_ref[...]) + _dot(r_ref[rows, :], wor_ref[...])
           for rows in blocks]
    qxs = [_dot((_rms(x1, D_MODEL) * ncross_ref[...]).astype(BF16), wq_ref[...]) for x1 in x1s]
    heads = [[] for _ in blocks]
    for hd in range(X_HEADS):
        sl = slice(hd * X_HEAD_DIM, (hd + 1) * X_HEAD_DIM)
        for n, qx in enumerate(qxs):
            qh = (_rms(qx[:, sl], X_HEAD_DIM) * qn_ref[...] * q_scale).astype(BF16)
            s = lax.dot_general(qh, kx_ref[0, :, sl], NT_DIMS, preferred_element_type=F32)
            p = jnp.exp2(s - jnp.max(s, axis=-1, keepdims=True))
            o = _dot(p.astype(BF16), vx_ref[0, :, sl]) / jnp.sum(p, axis=-1, keepdims=True)
            heads[n].append(o.astype(BF16))
    x2s = [x1 + _dot(jnp.concatenate(hs, axis=-1), wxo_ref[...]) for x1, hs in zip(x1s, heads)]
    for rows, x2 in zip(blocks, x2s):
        x2_out[rows, 0:D_MODEL] = x2
    for rows, x2 in zip(blocks, x2s):
        h3 = _rms(x2, D_MODEL) * nffn_ref[...]
        logits = _router_logits(h3, wg_ref, bg_ref[...])
        g_logits = jnp.where(lane < MOE_GROUPS, logits, -jnp.inf)
        g_max, g_idx = _first_max_lane(g_logits, lane_f)
        g_weight = 1.0 / jnp.sum(jnp.exp(g_logits - g_max), axis=-1, keepdims=True)
        first = MOE_GROUPS + MOE_PER_GROUP * g_idx
        in_group = jnp.logical_and(lane_f >= first, lane_f < first + MOE_PER_GROUP)
        e_logits = jnp.where(in_group, logits, -jnp.inf)
        e1, i1 = _first_max_lane(e_logits, lane_f)
        e2, i2 = _first_max_lane(jnp.where(lane_f == i1, -jnp.inf, e_logits), lane_f)
        t = jnp.exp(e2 - e1)
        w1 = g_weight / (1.0 + t)
        w2 = g_weight * t / (1.0 + t)
        l1 = i1 - first
        l2 = i2 - first
        x2_out[rows, D_MODEL:D_MODEL + LANES] = jnp.where(lane_f == l1, w1, jnp.where(lane_f == l2, w2, 0.0))
        pair = jnp.minimum(l1, l2) * MOE_PER_GROUP + jnp.maximum(l1, l2)
        key_out[rows, :] = (g_idx * (MOE_PER_GROUP * MOE_PER_GROUP) + pair).astype(jnp.int32)


def _cross(x2d, a, r, w_out, norm_cross, w_q, q_norm, kx, vx, w_o, norm_ffn, w_group, b_group, w_expert, b_expert,
           B, S, tm):
    T = B * S
    per_b = S // tm
    M = kx.shape[1]
    woa = w_out[:MLA_HEADS * MLA_V].astype(BF16)
    wor = w_out[MLA_HEADS * MLA_V:].astype(BF16)
    wq = w_q.astype(BF16)
    wxo = w_o.astype(BF16)
    n_router = MOE_GROUPS + MOE_GROUPS * MOE_PER_GROUP
    wg = _split_hi_lo(jnp.pad(jnp.concatenate([w_group, w_expert], axis=1), ((0, 0), (0, LANES - n_router))))
    bg = jnp.pad(jnp.concatenate([b_group, b_expert]), (0, LANES - n_router)).reshape(1, LANES)

    def full(arr):
        return pl.BlockSpec(arr.shape, lambda i: (0,) * arr.ndim)

    def rows(width):
        return pl.BlockSpec((tm, width), lambda i: (i, 0))

    ncross = norm_cross.reshape(1, -1)
    qn = q_norm.reshape(1, -1)
    nffn = norm_ffn.reshape(1, -1)
    mem_blk = pl.BlockSpec((1, M, D_MODEL), lambda i: (i // per_b, 0, 0))
    return pl.pallas_call(
        _cross_kernel,
        grid=(T // tm,),
        in_specs=[rows(D_MODEL), rows(MLA_HEADS * MLA_V), rows(HG_WIDTH), full(woa), full(wor), full(ncross),
                  full(wq), full(qn), mem_blk, mem_blk, full(wxo), full(nffn), full(wg), full(bg)],
        out_specs=(rows(MOE_ROW_WIDTH), rows(1)),
        out_shape=(jax.ShapeDtypeStruct((T, MOE_ROW_WIDTH), F32), jax.ShapeDtypeStruct((T, 1), jnp.int32)),
        compiler_params=_params(("parallel",)),
        name="cross",
    )(x2d, a, r, woa, wor, ncross, wq, qn, kx, vx, wxo, nffn, wg, bg)


def _moe_kernel(grp_ref, nvalid_ref, nactive_ref, elist_ref,
                src_ref, src_next_ref, src_next2_ref, src_prev_ref,
                x_hbm, nffn_ref, wg_ref, wu_ref, wd_ref,
                out_hbm, xbuf, ybuf, hb_scr, y_scr, gsem, ssem):
    i = pl.program_id(0)
    n_blocks = pl.num_programs(0)
    tb = xbuf.shape[1]
    slot = i % MOE_SLOTS
    slot_next2 = (i + 2) % MOE_SLOTS
    slot_prev = (i + MOE_SLOTS - 1) % MOE_SLOTS

    def n_valid_at(j):
        inside = jnp.logical_and(j >= 0, j < n_blocks)
        return jnp.where(inside, nvalid_ref[jnp.clip(j, 0, n_blocks - 1)], 0)

    n_valid = nvalid_ref[i]
    n_prev = n_valid_at(i - 1)
    n_next2 = n_valid_at(i + 2)

    def gather_row(idx_ref, r, to_slot):
        tok = idx_ref[0, 0, r]
        pltpu.make_async_copy(x_hbm.at[pl.ds(tok, 1)], xbuf.at[to_slot, pl.ds(r, 1)], gsem.at[to_slot]).start()

    def scatter_row(idx_ref, r, from_slot):
        tok = idx_ref[0, 0, r]
        pltpu.make_async_copy(ybuf.at[from_slot, pl.ds(r, 1)], out_hbm.at[pl.ds(tok, 1)], ssem.at[from_slot]).start()

    n_active = nactive_ref[i]

    def gather_all(idx_ref, to_slot):
        def body(r, carry):
            gather_row(idx_ref, r, to_slot)
            return carry
        lax.fori_loop(0, tb, body, 0)

    def scatter_some(idx_ref, from_slot, count):
        def body(r, carry):
            scatter_row(idx_ref, r, from_slot)
            return carry
        lax.fori_loop(0, count, body, 0)

    def wait_scatter(of_slot, count):
        whole = pl.multiple_of((count // SUBLANES) * SUBLANES, SUBLANES)

        @pl.when(whole > 0)
        def _():
            pltpu.make_async_copy(ybuf.at[of_slot, pl.ds(0, whole)], out_hbm.at[pl.ds(0, whole)],
                                  ssem.at[of_slot]).wait()

        def one(r, carry):
            pltpu.make_async_copy(ybuf.at[of_slot, pl.ds(0, 1)], out_hbm.at[pl.ds(0, 1)], ssem.at[of_slot]).wait()
            return carry
        lax.fori_loop(0, count - whole, one, 0)

    def evaluate(side_work):
        pltpu.make_async_copy(x_hbm.at[pl.ds(0, tb)], xbuf.at[slot], gsem.at[slot]).wait()
        x = xbuf[slot, :, 0:D_MODEL]
        hb_scr[...] = (_rms(x, D_MODEL) * nffn_ref[...]).astype(BF16)
        lane = lax.broadcasted_iota(jnp.int32, (1, LANES), 1)

        def expert(k, first):
            e = elist_ref[i * MOE_PER_GROUP + k]
            hb = hb_scr[...]
            gate = _dot(hb, wg_ref[0, e])
            up = _dot(hb, wu_ref[0, e])
            slab = xbuf[slot, :, D_MODEL:MOE_ROW_WIDTH]
            w_e = jnp.sum(jnp.where(lane == e, slab, 0.0), axis=-1, keepdims=True)
            act = jnp.where(w_e != 0.0, _silu(gate) * up * w_e, 0.0).astype(BF16)
            rows = pl.ds(pl.multiple_of(e * MOE_HIDDEN, MOE_HIDDEN), MOE_HIDDEN)
            y = _dot(act, wd_ref[0, rows, :])
            y_scr[...] = y if first else y_scr[...] + y

        side_work(0)
        expert(0, True)
        side_work(1)
        expert(1, False)
        for k in range(2, MOE_PER_GROUP):
            @pl.when(k < n_active)
            def _(k=k):
                expert(k, False)
        ybuf[slot] = xbuf[slot, :, 0:D_MODEL] + y_scr[...]

    @pl.when(i == 0)
    def _():
        @pl.when(n_valid > 0)
        def _():
            gather_all(src_ref, 0)

        @pl.when(n_valid_at(1) > 0)
        def _():
            gather_all(src_next_ref, 1)

    wait_scatter(slot, n_valid_at(i - MOE_SLOTS))

    steady = jnp.logical_and(jnp.logical_and(n_valid > 0, n_next2 > 0), n_prev == tb)
    share = tb // 2

    @pl.when(steady)
    def _():
        def side_work(part):
            for r in range(part * share, (part + 1) * share):
                gather_row(src_next2_ref, r, slot_next2)
                scatter_row(src_prev_ref, r, slot_prev)
        evaluate(side_work)

    @pl.when(jnp.logical_not(steady))
    def _():
        @pl.when(n_next2 > 0)
        def _():
            gather_all(src_next2_ref, slot_next2)
        scatter_some(src_prev_ref, slot_prev, n_prev)

        @pl.when(n_valid > 0)
        def _():
            evaluate(lambda part: None)

    @pl.when(i == n_blocks - 1)
    def _():
        wait_scatter((i + MOE_SLOTS - 2) % MOE_SLOTS, n_valid_at(i - 2))
        wait_scatter(slot_prev, n_prev)
        scatter_some(src_ref, slot, n_valid)
        wait_scatter(slot, n_valid)


def _moe(x2, key, norm_ffn, w_gate, w_up, w_down, tb):
    T = x2.shape[0]
    G, E, H = MOE_GROUPS, MOE_PER_GROUP, MOE_HIDDEN
    n_blocks = T // tb + G

    key = key.reshape(T)
    g = key // (E * E)
    order = jnp.argsort(key, stable=True).astype(jnp.int32)
    counts = jnp.sum(g[:, None] == jnp.arange(G, dtype=jnp.int32)[None, :], axis=0).astype(jnp.int32)
    blocks_per_group = (counts + tb - 1) // tb
    block_end = jnp.cumsum(blocks_per_group)
    block_start = block_end - blocks_per_group
    token_start = jnp.cumsum(counts) - counts
    bi = jnp.arange(n_blocks, dtype=jnp.int32)
    block_group = jnp.minimum(jnp.sum(bi[:, None] >= block_end[None, :], axis=1), G - 1).astype(jnp.int32)
    first_row = (bi - block_start[block_group]) * tb
    n_valid = jnp.clip(counts[block_group] - first_row, 0, tb)
    n_valid = jnp.where(bi < block_end[G - 1], n_valid, 0).astype(jnp.int32)
    r = jnp.arange(tb, dtype=jnp.int32)[None, :]
    pos = token_start[block_group][:, None] + first_row[:, None] + r
    valid = r < n_valid[:, None]
    src = jnp.where(valid, order[jnp.clip(pos, 0, T - 1)], 0).astype(jnp.int32)
    pair = key[src] % (E * E)
    used = jnp.left_shift(1, pair // E) | jnp.left_shift(1, pair % E)
    active = lax.reduce(jnp.where(valid, used, 0).astype(jnp.int32), jnp.int32(0), lax.bitwise_or, (1,))
    is_used = jnp.bitwise_and(jnp.right_shift(active[:, None], jnp.arange(E, dtype=jnp.int32)[None, :]), 1)
    n_active = jnp.sum(is_used, axis=1).astype(jnp.int32)
    expert_list = jnp.argsort(1 - is_used, axis=1, stable=True).astype(jnp.int32).reshape(n_blocks * E)
    src = src.reshape(n_blocks, 1, tb)

    wg = w_gate.reshape(G, E, D_MODEL, H).astype(BF16)
    wu = w_up.reshape(G, E, D_MODEL, H).astype(BF16)
    wd = w_down.reshape(G, E * H, D_MODEL).astype(BF16)
    nffn = norm_ffn.reshape(1, -1)

    smem_rows = lambda f: pl.BlockSpec((1, 1, tb), f, memory_space=pltpu.SMEM)
    grid_spec = pltpu.PrefetchScalarGridSpec(
        num_scalar_prefetch=4,
        grid=(n_blocks,),
        in_specs=[smem_rows(lambda i, grp, nv, nact, elist: (i, 0, 0)),
                  smem_rows(lambda i, grp, nv, nact, elist: (jnp.minimum(i + 1, n_blocks - 1), 0, 0)),
                  smem_rows(lambda i, grp, nv, nact, elist: (jnp.minimum(i + 2, n_blocks - 1), 0, 0)),
                  smem_rows(lambda i, grp, nv, nact, elist: (jnp.maximum(i - 1, 0), 0, 0)),
                  pl.BlockSpec(memory_space=pl.ANY),
                  pl.BlockSpec((1, D_MODEL), lambda i, grp, nv, nact, elist: (0, 0)),
                  pl.BlockSpec((1, E, D_MODEL, H), lambda i, grp, nv, nact, elist: (grp[i], 0, 0, 0)),
                  pl.BlockSpec((1, E, D_MODEL, H), lambda i, grp, nv, nact, elist: (grp[i], 0, 0, 0)),
                  pl.BlockSpec((1, E * H, D_MODEL), lambda i, grp, nv, nact, elist: (grp[i], 0, 0))],
        out_specs=pl.BlockSpec(memory_space=pl.ANY),
        scratch_shapes=[pltpu.VMEM((MOE_SLOTS, tb, MOE_ROW_WIDTH), F32), pltpu.VMEM((MOE_SLOTS, tb, D_MODEL), F32),
                        pltpu.VMEM((tb, D_MODEL), BF16), pltpu.VMEM((tb, D_MODEL), F32),
                        pltpu.SemaphoreType.DMA((MOE_SLOTS,)), pltpu.SemaphoreType.DMA((MOE_SLOTS,))])
    return pl.pallas_call(
        _moe_kernel,
        grid_spec=grid_spec,
        out_shape=jax.ShapeDtypeStruct((T, D_MODEL), F32),
        compiler_params=_params(("arbitrary",)),
        name="experts",
    )(block_group, n_valid, n_active, expert_list, src, src, src, src, x2, nffn, wg, wu, wd)


def _pick(n, pref):
    t = min(pref, n)
    while n % t:
        t //= 2
    return t


def _tiles(B, S):
    T = B * S
    return dict(
        inproj_rows=_pick(T, 512),
        mla_q_rows=_pick(S, 512),
        mla_k_rows=_pick(S, 1024),
        cross_rows=_pick(S, 2 * CROSS_SUB_ROWS),
        moe_rows=_pick(T, 256),
    )


def kernel(x, mem, positions, norm_mix, w_in, mla_q_a_norm, mla_w_q_up, mla_kv_a_norm, mla_w_kv_up, mla_q_norm, mla_k_norm, hg_lb_logits, hg_o_norm, w_out, norm_cross, norm_mem, x_w_q, x_w_kv, x_q_norm, x_k_norm, x_w_o, norm_ffn, moe_w_group, moe_b_group, moe_w_expert, moe_b_expert, moe_w_gate, moe_w_up, moe_w_down):
    B, S, D = x.shape
    assert D == D_MODEL and w_in.shape[0] == 1 and S % HG_CHUNK == 0
    T = B * S
    x2d = x.reshape(T, D)
    pos2d = positions.reshape(T, 1).astype(jnp.int32)
    tiles = _tiles(B, S)

    q, k, v, hq, hff, hfb, hi, hg = _inproj(
        x2d, pos2d, norm_mix[0], w_in[0], mla_q_a_norm[0], mla_w_q_up[0], mla_kv_a_norm[0], mla_w_kv_up[0],
        mla_q_norm[0], mla_k_norm[0], tm=tiles["inproj_rows"])
    a = _mla_attention(q, k, v, B, S, tq=tiles["mla_q_rows"], tk=tiles["mla_k_rows"])
    r = _hgrn(hq, hff, hfb, hi, hg, hg_lb_logits, hg_o_norm[0], B, S)
    kx, vx = _memkv(mem, norm_mem[0], x_w_kv[0], x_k_norm[0])
    x2, key = _cross(x2d, a, r, w_out[0], norm_cross[0], x_w_q[0], x_q_norm[0], kx, vx, x_w_o[0], norm_ffn[0],
                     moe_w_group[0], moe_b_group[0], moe_w_expert[0], moe_b_expert[0], B, S, tm=tiles["cross_rows"])
    out = _moe(x2, key, norm_ffn[0], moe_w_gate[0], moe_w_up[0], moe_w_down[0], tb=tiles["moe_rows"])
    return out.reshape(B, S, D)
```

```python
import functools
import math

import jax
import jax.numpy as jnp
from jax import lax
from jax.experimental import pallas as pl
from jax.experimental.pallas import tpu as pltpu

F32 = jnp.float32
BF16 = jnp.bfloat16

LANES = 128
SUBLANES = 8
VMEM_LIMIT_BYTES = 56 * 1024 * 1024

NORM_EPS = 1e-6
LOG2E = math.log2(math.e)

D_MODEL = 1024
MLA_HEADS = 8
MLA_NOPE = 64
MLA_ROPE = 32
MLA_QK = MLA_NOPE + MLA_ROPE
MLA_V = 64
MLA_Q_RANK = 192
MLA_Q_RANK_PAD = 256
MLA_KV_RANK = 128
MLA_HEADS_PER_STEP = 4
ROPE_BASE = 10000.0
CROSS_SUB_ROWS = 256
HG_HEADS = 4
HG_DIM = 128
HG_WIDTH = HG_HEADS * HG_DIM
HG_CHUNK = 128
HG_BAND = 4
HG_HEADS_PER_STEP = 2
HG_CHUNKS_PER_TRIP = 2
HG_MIN_GATE = 2.0 ** -100
X_HEADS = 4
X_HEAD_DIM = D_MODEL // X_HEADS
MOE_GROUPS = 8
MOE_PER_GROUP = 8
MOE_HIDDEN = 256
MOE_GROUP_HIDDEN = MOE_PER_GROUP * MOE_HIDDEN
MOE_SLOTS = 3
MOE_ROW_WIDTH = D_MODEL + LANES
IN_SIZES = (MLA_Q_RANK, MLA_KV_RANK, MLA_ROPE, HG_WIDTH, HG_WIDTH, HG_WIDTH, HG_WIDTH, HG_WIDTH)

NT_DIMS = (((1,), (1,)), ((), ()))
TN_DIMS = (((0,), (0,)), ((), ()))


def _rms(x, n):
    return x * lax.rsqrt(jnp.sum(x * x, axis=-1, keepdims=True) * (1.0 / n) + NORM_EPS)


def _silu(x):
    return x / (1.0 + jnp.exp(-x))


def _split3(x):
    a = x.astype(BF16)
    r = x - a.astype(F32)
    b = r.astype(BF16)
    c = (r - b.astype(F32)).astype(BF16)
    return a, b, c


def _dot(a, b):
    return jnp.dot(a, b, preferred_element_type=F32)


def _params(sem):
    return pltpu.CompilerParams(dimension_semantics=sem, vmem_limit_bytes=VMEM_LIMIT_BYTES)


def _inproj_kernel(x_ref, pos_ref, nmix_ref, w_ref, qan_ref, wq_ref, kvan_ref, wk_ref, wv_ref,
                   qn_ref, kn_ref, invf_ref,
                   q_out, k_out, v_out, hq_out, hff_out, hfb_out, hi_out, hg_out):
    h = (_rms(x_ref[...], D_MODEL) * nmix_ref[...]).astype(BF16)

    p = _dot(h, w_ref[:, 0:512])
    c_q = p[:, 0:MLA_Q_RANK_PAD]
    cqn = (_rms(c_q, MLA_Q_RANK) * qan_ref[...]).astype(BF16)
    q = _dot(cqn, wq_ref[...])
    c_kv = p[:, 256:384]
    ckvn = (_rms(c_kv, MLA_KV_RANK) * kvan_ref[...]).astype(BF16)
    k_nope = _dot(ckvn, wk_ref[...])
    v_out[...] = _dot(ckvn, wv_ref[...]).astype(BF16)
    k_rope = p[:, 384:512]

    ang = pos_ref[...].astype(F32) * invf_ref[...]
    cos = jnp.cos(ang)
    sin = jnp.sin(ang)
    lane = lax.broadcasted_iota(jnp.int32, (1, LANES), 1)
    half = MLA_ROPE // 2
    sin_lo = jnp.where(lane < MLA_NOPE + half, -sin, 0.0)
    sin_hi = jnp.where(lane >= MLA_NOPE + half, sin, 0.0)

    def rope(t):
        return t * cos + pltpu.roll(t, LANES - half, 1) * sin_lo + pltpu.roll(t, half, 1) * sin_hi

    q_scale = MLA_QK ** -0.5 * LOG2E
    mixer_outs = (hq_out, hff_out, hfb_out, hi_out, hg_out)
    for hd in range(MLA_HEADS):
        sl = slice(hd * LANES, (hd + 1) * LANES)
        qh = _rms(q[:, sl], MLA_QK) * qn_ref[...]
        q_out[:, sl] = (rope(qh) * q_scale).astype(BF16)
        kh = _rms(k_nope[:, sl] + k_rope, MLA_QK) * kn_ref[...]
        k_out[:, sl] = rope(kh).astype(BF16)
        if hd < len(mixer_outs):
            out = mixer_outs[hd]
            c0 = 512 + hd * HG_WIDTH
            out[...] = _dot(h, w_ref[:, c0:c0 + HG_WIDTH]).astype(out.dtype)


def _inproj(x2d, pos2d, norm_mix, w_in, q_a_norm, w_q_up, kv_a_norm, w_kv_up, q_norm, k_norm, tm):
    T = x2d.shape[0]
    c0 = 0
    cols = []
    for size in IN_SIZES:
        cols.append(w_in[:, c0:c0 + size])
        c0 += size
    w_cq, w_ckv, w_kr, w_hq, w_hff, w_hfb, w_hi, w_hg = cols
    w_cq = jnp.pad(w_cq, ((0, 0), (0, MLA_Q_RANK_PAD - MLA_Q_RANK)))
    w_kr = jnp.pad(w_kr, ((0, 0), (MLA_NOPE, LANES - MLA_QK)))
    w_big = jnp.concatenate([w_cq, w_ckv, w_kr, w_hq, w_hff, w_hfb, w_hi, w_hg], axis=1).astype(BF16)
    n_big = w_big.shape[1]

    qan = jnp.pad(q_a_norm, (0, MLA_Q_RANK_PAD - MLA_Q_RANK)).reshape(1, -1)
    wq = w_q_up.reshape(MLA_Q_RANK, MLA_HEADS, MLA_QK)
    wq = jnp.pad(wq, ((0, MLA_Q_RANK_PAD - MLA_Q_RANK), (0, 0), (0, LANES - MLA_QK)))
    wq = wq.reshape(MLA_Q_RANK_PAD, MLA_HEADS * LANES).astype(BF16)
    wkv = w_kv_up.reshape(MLA_KV_RANK, MLA_HEADS, MLA_NOPE + MLA_V)
    wk = jnp.pad(wkv[:, :, :MLA_NOPE], ((0, 0), (0, 0), (0, LANES - MLA_NOPE)))
    wk = wk.reshape(MLA_KV_RANK, MLA_HEADS * LANES).astype(BF16)
    wv = wkv[:, :, MLA_NOPE:].reshape(MLA_KV_RANK, MLA_HEADS * MLA_V).astype(BF16)
    qn = jnp.pad(q_norm, (0, LANES - MLA_QK)).reshape(1, LANES)
    kn = jnp.pad(k_norm, (0, LANES - MLA_QK)).reshape(1, LANES)
    half = MLA_ROPE // 2
    inv_freq = 1.0 / (ROPE_BASE ** (jnp.arange(half, dtype=F32) / half))
    invf = jnp.concatenate([jnp.zeros((MLA_NOPE,), F32), inv_freq, inv_freq,
                            jnp.zeros((LANES - MLA_QK,), F32)]).reshape(1, LANES)

    def full(a):
        return pl.BlockSpec(a.shape, lambda i: (0,) * a.ndim)

    def rows(width):
        return pl.BlockSpec((tm, width), lambda i: (i, 0))

    nmix = norm_mix.reshape(1, -1)
    kvan = kv_a_norm.reshape(1, -1)
    qk_w = MLA_HEADS * LANES
    v_w = MLA_HEADS * MLA_V
    out_shape = (
        jax.ShapeDtypeStruct((T, qk_w), BF16), jax.ShapeDtypeStruct((T, qk_w), BF16),
        jax.ShapeDtypeStruct((T, v_w), BF16),
        jax.ShapeDtypeStruct((T, HG_WIDTH), BF16), jax.ShapeDtypeStruct((T, HG_WIDTH), F32),
        jax.ShapeDtypeStruct((T, HG_WIDTH), F32), jax.ShapeDtypeStruct((T, HG_WIDTH), BF16),
        jax.ShapeDtypeStruct((T, HG_WIDTH), BF16))
    return pl.pallas_call(
        _inproj_kernel,
        grid=(T // tm,),
        in_specs=[rows(D_MODEL), rows(1), full(nmix), full(w_big), full(qan), full(wq), full(kvan),
                  full(wk), full(wv), full(qn), full(kn), full(invf)],
        out_specs=(rows(qk_w), rows(qk_w), rows(v_w), rows(HG_WIDTH), rows(HG_WIDTH), rows(HG_WIDTH),
                   rows(HG_WIDTH), rows(HG_WIDTH)),
        out_shape=out_shape,
        compiler_params=_params(("parallel",)),
        name="inproj",
    )(x2d, pos2d, nmix, w_big, qan, wq, kvan, wk, wv, qn, kn, invf)


def _mla_kernel(q_ref, k_ref, v_ref, o_ref, *, tk):
    heads = MLA_HEADS_PER_STEP
    n_chunks = k_ref.shape[0] // tk
    nsub = tk // LANES
    lane = lax.broadcasted_iota(jnp.int32, (1, LANES), 1)
    own = [lane < MLA_V, lane >= MLA_V]
    m = [None] * heads
    acc = [None] * heads
    for c in range(n_chunks):
        rows = slice(c * tk, (c + 1) * tk)
        scores = [lax.dot_general(q_ref[:, j * LANES:(j + 1) * LANES], k_ref[rows, j * LANES:(j + 1) * LANES],
                                  NT_DIMS, preferred_element_type=F32) for j in range(heads)]
        for j, s in enumerate(scores):
            vv = v_ref[rows, (j // 2) * LANES:(j // 2 + 1) * LANES]
            blk_max = s[:, 0:LANES]
            for i in range(1, nsub):
                blk_max = jnp.maximum(blk_max, s[:, i * LANES:(i + 1) * LANES])
            m_new = jnp.max(blk_max, axis=-1, keepdims=True)
            if c > 0:
                m_new = jnp.maximum(m[j], m_new)
            p = jnp.exp2((s - m_new).astype(BF16))
            pv = _dot(p, jnp.where(own[j % 2], vv, jnp.ones_like(vv)))
            acc[j] = pv if c == 0 else acc[j] * jnp.exp2(m[j] - m_new) + pv
            m[j] = m_new
    for pair in range(heads // 2):
        a0, a1 = acc[2 * pair], acc[2 * pair + 1]
        o0 = a0 / a0[:, MLA_V:MLA_V + 1]
        o1 = a1 / a1[:, 0:1]
        o_ref[:, pair * LANES:(pair + 1) * LANES] = jnp.where(own[0], o0, o1).astype(BF16)


def _mla_attention(q, k, v, B, S, tq, tk):
    T = B * S
    nq = S // tq
    heads = MLA_HEADS_PER_STEP
    return pl.pallas_call(
        functools.partial(_mla_kernel, tk=tk),
        grid=(B, MLA_HEADS // heads, nq),
        in_specs=[pl.BlockSpec((tq, heads * LANES), lambda b, h, i: (b * nq + i, h)),
                  pl.BlockSpec((S, heads * LANES), lambda b, h, i: (b, h)),
                  pl.BlockSpec((S, heads * MLA_V), lambda b, h, i: (b, h))],
        out_specs=pl.BlockSpec((tq, heads * MLA_V), lambda b, h, i: (b * nq + i, h)),
        out_shape=jax.ShapeDtypeStruct((T, MLA_HEADS * MLA_V), BF16),
        compiler_params=_params(("parallel", "parallel", "arbitrary")),
        name="mla_attention",
    )(q, k, v)


def _hgrn_pair_codes(rev):
    C = HG_CHUNK
    row = lax.broadcasted_iota(jnp.int32, (C, C), 0)
    col = lax.broadcasted_iota(jnp.int32, (C, C), 1)
    dist = (col - row) if rev else (row - col)
    code = jnp.full((C, C), -1, jnp.int32)
    m, level = C // 2, HG_BAND
    levels = []
    while m >= HG_BAND:
        levels.append(m)
        m //= 2
    for j, m in enumerate(levels):
        same = (row // (2 * m)) == (col // (2 * m))
        code = jnp.where(same, HG_BAND + len(levels) - 1 - j, code)
    code = jnp.where((row // HG_BAND) == (col // HG_BAND), dist, code)
    return jnp.where(dist < 0, -1, code)


def _hgrn_chunks(chains):
    C = HG_CHUNK
    row = lax.broadcasted_iota(jnp.int32, (C, 1), 0)
    col = lax.broadcasted_iota(jnp.int32, (1, C), 1)
    n = len(chains)

    kks, fs, bs = [], [], []
    for q, z, v, lb_row, state, code, rev in chains:
        kk = (1.0 - lb_row) / (1.0 + jnp.exp(z))
        f = jnp.maximum(1.0 - kk, HG_MIN_GATE)
        g = jnp.log2(f)
        tri = jnp.where((col >= row) if rev else (col <= row), 1.0, 0.0).astype(BF16)
        g1, g2, g3 = _split3(g)
        kks.append(kk)
        fs.append(f)
        bs.append(_dot(tri, g1) + _dot(tri, g2) + _dot(tri, g3))

    outs, states = [], []
    for (q, z, v, lb_row, state, code, rev), kk, b in zip(chains, kks, bs):
        if isinstance(state, int):
            state = states[state]
        q_hat = (q * jnp.exp2(b)).astype(BF16)
        outs.append(lax.dot_general(q_hat, state.astype(BF16), NT_DIMS, preferred_element_type=F32))
        b_end = b[0:1, :] if rev else b[C - 1:C, :]
        k_hat = (kk * jnp.exp2(b_end - b)).astype(BF16)
        states.append(state * jnp.exp2(b_end) + lax.dot_general(v, k_hat, TN_DIMS, preferred_element_type=F32))

    attns = []
    for (q, z, v, lb_row, state, code, rev), kk, f in zip(chains, kks, fs):
        step = (C - 1) if rev else 1
        u = kk
        attn = jnp.where(code == 0, jnp.sum(q * u, axis=-1, keepdims=True), 0.0)
        for d in range(1, HG_BAND):
            u = f * pltpu.roll(u, step, 0)
            attn = jnp.where(code == d, jnp.sum(q * u, axis=-1, keepdims=True), attn)
        attns.append(attn)

    m, level = HG_BAND, HG_BAND
    while m < C:
        for idx in range(n):
            q, z, v, lb_row, state, code, rev = chains[idx]
            b3 = bs[idx].reshape(C // (2 * m), 2 * m, HG_DIM)
            ref = b3[:, m:m + 1, :] if rev else b3[:, m - 1:m, :]
            e = jnp.exp2(-jnp.abs(b3 - ref)).reshape(C, HG_DIM)
            a_m = lax.dot_general((q * e).astype(BF16), (kks[idx] * e).astype(BF16), NT_DIMS,
                                  preferred_element_type=F32)
            attns[idx] = jnp.where(code == level, a_m, attns[idx])
        m, level = 2 * m, level + 1

    return [(o + _dot(attn.astype(BF16), ch[2]), st) for o, attn, ch, st in zip(outs, attns, chains, states)]


def _hgrn_kernel(hq_ref, hff_ref, hfb_ref, hi_ref, hg_ref, lbl_ref, onorm_ref, out_ref,
                 q_scr, of_scr, ob_scr, code_scr):
    C = HG_CHUNK
    n_chunks = hq_ref.shape[0] // C
    lg = lbl_ref[...]
    mx = jnp.maximum(lg[0], lg[1])
    e0 = jnp.exp(lg[0] - mx)
    lb = e0 / (e0 + jnp.exp(lg[1] - mx))
    q_scr[...] = _silu(hq_ref[...].astype(F32))
    code_scr[0] = _hgrn_pair_codes(False)
    code_scr[1] = _hgrn_pair_codes(True)

    per_trip = HG_CHUNKS_PER_TRIP if n_chunks % HG_CHUNKS_PER_TRIP == 0 else 1

    def body(i, states):
        chains, dest = [], []
        for hd in range(HG_HEADS_PER_STEP):
            sl = slice(hd * HG_DIM, (hd + 1) * HG_DIM)
            for u in range(per_trip):
                sf = pl.multiple_of((i * per_trip + u) * C, C)
                sb = pl.multiple_of((n_chunks - 1 - (i * per_trip + u)) * C, C)
                state_f = states[hd][0] if u == 0 else len(chains) - 2
                chains.append((q_scr[pl.ds(sf, C), sl], hff_ref[pl.ds(sf, C), sl], hi_ref[pl.ds(sf, C), sl],
                               lb[0:1, sl], state_f, code_scr[0], False))
                dest.append((of_scr, sf, sl))
                state_b = states[hd][1] if u == 0 else len(chains) - 2
                chains.append((q_scr[pl.ds(sb, C), sl], hfb_ref[pl.ds(sb, C), sl], hi_ref[pl.ds(sb, C), sl],
                               lb[1:2, sl], state_b, code_scr[1], True))
                dest.append((ob_scr, sb, sl))
        results = _hgrn_chunks(chains)
        for (o, _), (scr, start, sl) in zip(results, dest):
            scr[pl.ds(start, C), sl] = o
        last = 2 * per_trip
        return tuple((results[hd * last + last - 2][1], results[hd * last + last - 1][1])
                     for hd in range(HG_HEADS_PER_STEP))

    zero = jnp.zeros((HG_DIM, HG_DIM), F32)
    lax.fori_loop(0, n_chunks // per_trip, body, ((zero, zero),) * HG_HEADS_PER_STEP)
    for hd in range(HG_HEADS_PER_STEP):
        sl = slice(hd * HG_DIM, (hd + 1) * HG_DIM)
        o = of_scr[:, sl] + ob_scr[:, sl]
        out_ref[:, sl] = ((_rms(o, HG_DIM) * onorm_ref[...]).astype(BF16)
                          * _silu(hg_ref[:, sl].astype(F32)).astype(BF16))


def _hgrn(hq, hff, hfb, hi, hg, lb_logits, o_norm, B, S):
    T = B * S
    width = HG_HEADS_PER_STEP * HG_DIM
    blk = pl.BlockSpec((S, width), lambda b, h: (b, h))
    n_layers = lb_logits.shape[0]
    return pl.pallas_call(
        _hgrn_kernel,
        grid=(B, HG_HEADS // HG_HEADS_PER_STEP),
        in_specs=[blk, blk, blk, blk, blk,
                  pl.BlockSpec((n_layers, 2, width), lambda b, h: (0, 0, h)),
                  pl.BlockSpec((1, HG_DIM), lambda b, h: (0, 0))],
        out_specs=blk,
        out_shape=jax.ShapeDtypeStruct((T, HG_WIDTH), BF16),
        scratch_shapes=[pltpu.VMEM((S, width), F32), pltpu.VMEM((S, width), F32), pltpu.VMEM((S, width), F32),
                        pltpu.VMEM((2, HG_CHUNK, HG_CHUNK), jnp.int32)],
        compiler_params=_params(("parallel", "parallel")),
        name="hgrn2",
    )(hq, hff, hfb, hi, hg, lb_logits, o_norm.reshape(1, HG_DIM))


def _memkv_kernel(mem_ref, nmem_ref, wkv_ref, kn_ref, k_out, v_out):
    hm = (_rms(mem_ref[0], D_MODEL) * nmem_ref[...]).astype(BF16)
    kv = _dot(hm, wkv_ref[...])
    for hd in range(X_HEADS):
        sl = slice(hd * X_HEAD_DIM, (hd + 1) * X_HEAD_DIM)
        k_out[0, :, sl] = (_rms(kv[:, sl], X_HEAD_DIM) * kn_ref[...]).astype(BF16)
    v_out[0] = kv[:, D_MODEL:].astype(BF16)


def _memkv(mem, norm_mem, w_kv, k_norm):
    B, M, _ = mem.shape
    wkv = w_kv.astype(BF16)
    blk = pl.BlockSpec((1, M, D_MODEL), lambda b: (b, 0, 0))
    return pl.pallas_call(
        _memkv_kernel,
        grid=(B,),
        in_specs=[blk, pl.BlockSpec((1, D_MODEL), lambda b: (0, 0)),
                  pl.BlockSpec(wkv.shape, lambda b: (0, 0)),
                  pl.BlockSpec((1, X_HEAD_DIM), lambda b: (0, 0))],
        out_specs=(blk, blk),
        out_shape=(jax.ShapeDtypeStruct((B, M, D_MODEL), BF16), jax.ShapeDtypeStruct((B, M, D_MODEL), BF16)),
        compiler_params=_params(("parallel",)),
        name="mem_kv",
    )(mem, norm_mem.reshape(1, -1), wkv, k_norm.reshape(1, -1))


def _router_logits(h, w_ref, bias):
    h_hi = h.astype(BF16)
    h_lo = (h - h_hi.astype(F32)).astype(BF16)
    both = _dot(h_hi, w_ref[...])
    return both[:, :LANES] + both[:, LANES:] + _dot(h_lo, w_ref[:, :LANES]) + bias


def _split_hi_lo(w):
    hi = w.astype(BF16)
    return jnp.concatenate([hi, (w - hi.astype(F32)).astype(BF16)], axis=-1)


def _first_max_lane(vals, lane_f):
    mx = jnp.max(vals, axis=-1, keepdims=True)
    idx = jnp.min(jnp.where(vals == mx, lane_f, float(LANES)), axis=-1, keepdims=True)
    return mx, idx


def _cross_kernel(x_ref, a_ref, r_ref, woa_ref, wor_ref, ncross_ref, wq_ref, qn_ref, kx_ref, vx_ref, wxo_ref,
                  nffn_ref, wg_ref, bg_ref, x2_out, key_out):
    tm = x_ref.shape[0]
    sub = min(CROSS_SUB_ROWS, tm)
    blocks = [slice(r0, r0 + sub) for r0 in range(0, tm, sub)]
    q_scale = X_HEAD_DIM ** -0.5 * LOG2E
    lane = lax.broadcasted_iota(jnp.int32, (1, LANES), 1)
    lane_f = lane.astype(F32)

    x1s = [x_ref[rows, :] + _dot(a_ref[rows, :], woa_ref[...]) + _dot(r_ref[rows, :], wor_ref[...])
           for rows in blocks]
    qxs = [_dot((_rms(x1, D_MODEL) * ncross_ref[...]).astype(BF16), wq_ref[...]) for x1 in x1s]
    heads = [[] for _ in blocks]
    for hd in range(X_HEADS):
        sl = slice(hd * X_HEAD_DIM, (hd + 1) * X_HEAD_DIM)
        for n, qx in enumerate(qxs):
            qh = (_rms(qx[:, sl], X_HEAD_DIM) * qn_ref[...] * q_scale).astype(BF16)
            s = lax.dot_general(qh, kx_ref[0, :, sl], NT_DIMS, preferred_element_type=F32)
            p = jnp.exp2(s - jnp.max(s, axis=-1, keepdims=True))
            o = _dot(p.astype(BF16), vx_ref[0, :, sl]) / jnp.sum(p, axis=-1, keepdims=True)
            heads[n].append(o.astype(BF16))
    x2s = [x1 + _dot(jnp.concatenate(hs, axis=-1), wxo_ref[...]) for x1, hs in zip(x1s, heads)]
    for rows, x2 in zip(blocks, x2s):
        x2_out[rows, 0:D_MODEL] = x2
    for rows, x2 in zip(blocks, x2s):
        h3 = _rms(x2, D_MODEL) * nffn_ref[...]
        logits = _router_logits(h3, wg_ref, bg_ref[...])
        g_logits = jnp.where(lane < MOE_GROUPS, logits, -jnp.inf)
        g_max, g_idx = _first_max_lane(g_logits, lane_f)
        g_weight = 1.0 / jnp.sum(jnp.exp(g_logits - g_max), axis=-1, keepdims=True)
        first = MOE_GROUPS + MOE_PER_GROUP * g_idx
        in_group = jnp.logical_and(lane_f >= first, lane_f < first + MOE_PER_GROUP)
        e_logits = jnp.where(in_group, logits, -jnp.inf)
        e1, i1 = _first_max_lane(e_logits, lane_f)
        e2, i2 = _first_max_lane(jnp.where(lane_f == i1, -jnp.inf, e_logits), lane_f)
        t = jnp.exp(e2 - e1)
        w1 = g_weight / (1.0 + t)
        w2 = g_weight * t / (1.0 + t)
        l1 = i1 - first
        l2 = i2 - first
        x2_out[rows, D_MODEL:D_MODEL + LANES] = jnp.where(lane_f == l1, w1, jnp.where(lane_f == l2, w2, 0.0))
        pair = jnp.minimum(l1, l2) * MOE_PER_GROUP + jnp.maximum(l1, l2)
        key_out[rows, :] = (g_idx * (MOE_PER_GROUP * MOE_PER_GROUP) + pair).astype(jnp.int32)


def _cross(x2d, a, r, w_out, norm_cross, w_q, q_norm, kx, vx, w_o, norm_ffn, w_group, b_group, w_expert, b_expert,
           B, S, tm):
    T = B * S
    per_b = S // tm
    M = kx.shape[1]
    woa = w_out[:MLA_HEADS * MLA_V].astype(BF16)
    wor = w_out[MLA_HEADS * MLA_V:].astype(BF16)
    wq = w_q.astype(BF16)
    wxo = w_o.astype(BF16)
    n_router = MOE_GROUPS + MOE_GROUPS * MOE_PER_GROUP
    wg = _split_hi_lo(jnp.pad(jnp.concatenate([w_group, w_expert], axis=1), ((0, 0), (0, LANES - n_router))))
    bg = jnp.pad(jnp.concatenate([b_group, b_expert]), (0, LANES - n_router)).reshape(1, LANES)

    def full(arr):
        return pl.BlockSpec(arr.shape, lambda i: (0,) * arr.ndim)

    def rows(width):
        return pl.BlockSpec((tm, width), lambda i: (i, 0))

    ncross = norm_cross.reshape(1, -1)
    qn = q_norm.reshape(1, -1)
    nffn = norm_ffn.reshape(1, -1)
    mem_blk = pl.BlockSpec((1, M, D_MODEL), lambda i: (i // per_b, 0, 0))
    return pl.pallas_call(
        _cross_kernel,
        grid=(T // tm,),
        in_specs=[rows(D_MODEL), rows(MLA_HEADS * MLA_V), rows(HG_WIDTH), full(woa), full(wor), full(ncross),
                  full(wq), full(qn), mem_blk, mem_blk, full(wxo), full(nffn), full(wg), full(bg)],
        out_specs=(rows(MOE_ROW_WIDTH), rows(1)),
        out_shape=(jax.ShapeDtypeStruct((T, MOE_ROW_WIDTH), F32), jax.ShapeDtypeStruct((T, 1), jnp.int32)),
        compiler_params=_params(("parallel",)),
        name="cross",
    )(x2d, a, r, woa, wor, ncross, wq, qn, kx, vx, wxo, nffn, wg, bg)


def _moe_kernel(grp_ref, nvalid_ref, nactive_ref, elist_ref,
                src_ref, src_next_ref, src_next2_ref, src_prev_ref,
                x_hbm, nffn_ref, wg_ref, wu_ref, wd_ref,
                out_hbm, xbuf, ybuf, hb_scr, y_scr, gsem, ssem):
    i = pl.program_id(0)
    n_blocks = pl.num_programs(0)
    tb = xbuf.shape[1]
    slot = i % MOE_SLOTS
    slot_next2 = (i + 2) % MOE_SLOTS
    slot_prev = (i + MOE_SLOTS - 1) % MOE_SLOTS

    def n_valid_at(j):
        inside = jnp.logical_and(j >= 0, j < n_blocks)
        return jnp.where(inside, nvalid_ref[jnp.clip(j, 0, n_blocks - 1)], 0)

    n_valid = nvalid_ref[i]
    n_prev = n_valid_at(i - 1)
    n_next2 = n_valid_at(i + 2)

    def gather_row(idx_ref, r, to_slot):
        tok = idx_ref[0, 0, r]
        pltpu.make_async_copy(x_hbm.at[pl.ds(tok, 1)], xbuf.at[to_slot, pl.ds(r, 1)], gsem.at[to_slot]).start()

    def scatter_row(idx_ref, r, from_slot):
        tok = idx_ref[0, 0, r]
        pltpu.make_async_copy(ybuf.at[from_slot, pl.ds(r, 1)], out_hbm.at[pl.ds(tok, 1)], ssem.at[from_slot]).start()

    n_active = nactive_ref[i]

    def gather_all(idx_ref, to_slot):
        def body(r, carry):
            gather_row(idx_ref, r, to_slot)
            return carry
        lax.fori_loop(0, tb, body, 0)

    def scatter_some(idx_ref, from_slot, count):
        def body(r, carry):
            scatter_row(idx_ref, r, from_slot)
            return carry
        lax.fori_loop(0, count, body, 0)

    def wait_scatter(of_slot, count):
        whole = pl.multiple_of((count // SUBLANES) * SUBLANES, SUBLANES)

        @pl.when(whole > 0)
        def _():
            pltpu.make_async_copy(ybuf.at[of_slot, pl.ds(0, whole)], out_hbm.at[pl.ds(0, whole)],
                                  ssem.at[of_slot]).wait()

        def one(r, carry):
            pltpu.make_async_copy(ybuf.at[of_slot, pl.ds(0, 1)], out_hbm.at[pl.ds(0, 1)], ssem.at[of_slot]).wait()
            return carry
        lax.fori_loop(0, count - whole, one, 0)

    def evaluate(side_work):
        pltpu.make_async_copy(x_hbm.at[pl.ds(0, tb)], xbuf.at[slot], gsem.at[slot]).wait()
        x = xbuf[slot, :, 0:D_MODEL]
        hb_scr[...] = (_rms(x, D_MODEL) * nffn_ref[...]).astype(BF16)
        lane = lax.broadcasted_iota(jnp.int32, (1, LANES), 1)

        def expert(k, first):
            e = elist_ref[i * MOE_PER_GROUP + k]
            hb = hb_scr[...]
            gate = _dot(hb, wg_ref[0, e])
            up = _dot(hb, wu_ref[0, e])
            slab = xbuf[slot, :, D_MODEL:MOE_ROW_WIDTH]
            w_e = jnp.sum(jnp.where(lane == e, slab, 0.0), axis=-1, keepdims=True)
            act = jnp.where(w_e != 0.0, _silu(gate) * up * w_e, 0.0).astype(BF16)
            rows = pl.ds(pl.multiple_of(e * MOE_HIDDEN, MOE_HIDDEN), MOE_HIDDEN)
            y = _dot(act, wd_ref[0, rows, :])
            y_scr[...] = y if first else y_scr[...] + y

        side_work(0)
        expert(0, True)
        side_work(1)
        expert(1, False)
        for k in range(2, MOE_PER_GROUP):
            @pl.when(k < n_active)
            def _(k=k):
                expert(k, False)
        ybuf[slot] = xbuf[slot, :, 0:D_MODEL] + y_scr[...]

    @pl.when(i == 0)
    def _():
        @pl.when(n_valid > 0)
        def _():
            gather_all(src_ref, 0)

        @pl.when(n_valid_at(1) > 0)
        def _():
            gather_all(src_next_ref, 1)

    wait_scatter(slot, n_valid_at(i - MOE_SLOTS))

    steady = jnp.logical_and(jnp.logical_and(n_valid > 0, n_next2 > 0), n_prev == tb)
    share = tb // 2

    @pl.when(steady)
    def _():
        def side_work(part):
            for r in range(part * share, (part + 1) * share):
                gather_row(src_next2_ref, r, slot_next2)
                scatter_row(src_prev_ref, r, slot_prev)
        evaluate(side_work)

    @pl.when(jnp.logical_not(steady))
    def _():
        @pl.when(n_next2 > 0)
        def _():
            gather_all(src_next2_ref, slot_next2)
        scatter_some(src_prev_ref, slot_prev, n_prev)

        @pl.when(n_valid > 0)
        def _():
            evaluate(lambda part: None)

    @pl.when(i == n_blocks - 1)
    def _():
        wait_scatter((i + MOE_SLOTS - 2) % MOE_SLOTS, n_valid_at(i - 2))
        wait_scatter(slot_prev, n_prev)
        scatter_some(src_ref, slot, n_valid)
        wait_scatter(slot, n_valid)


def _moe(x2, key, norm_ffn, w_gate, w_up, w_down, tb):
    T = x2.shape[0]
    G, E, H = MOE_GROUPS, MOE_PER_GROUP, MOE_HIDDEN
    n_blocks = T // tb + G

    key = key.reshape(T)
    g = key // (E * E)
    order = jnp.argsort(key, stable=True).astype(jnp.int32)
    counts = jnp.sum(g[:, None] == jnp.arange(G, dtype=jnp.int32)[None, :], axis=0).astype(jnp.int32)
    blocks_per_group = (counts + tb - 1) // tb
    block_end = jnp.cumsum(blocks_per_group)
    block_start = block_end - blocks_per_group
    token_start = jnp.cumsum(counts) - counts
    bi = jnp.arange(n_blocks, dtype=jnp.int32)
    block_group = jnp.minimum(jnp.sum(bi[:, None] >= block_end[None, :], axis=1), G - 1).astype(jnp.int32)
    first_row = (bi - block_start[block_group]) * tb
    n_valid = jnp.clip(counts[block_group] - first_row, 0, tb)
    n_valid = jnp.where(bi < block_end[G - 1], n_valid, 0).astype(jnp.int32)
    r = jnp.arange(tb, dtype=jnp.int32)[None, :]
    pos = token_start[block_group][:, None] + first_row[:, None] + r
    valid = r < n_valid[:, None]
    src = jnp.where(valid, order[jnp.clip(pos, 0, T - 1)], 0).astype(jnp.int32)
    pair = key[src] % (E * E)
    used = jnp.left_shift(1, pair // E) | jnp.left_shift(1, pair % E)
    active = lax.reduce(jnp.where(valid, used, 0).astype(jnp.int32), jnp.int32(0), lax.bitwise_or, (1,))
    is_used = jnp.bitwise_and(jnp.right_shift(active[:, None], jnp.arange(E, dtype=jnp.int32)[None, :]), 1)
    n_active = jnp.sum(is_used, axis=1).astype(jnp.int32)
    expert_list = jnp.argsort(1 - is_used, axis=1, stable=True).astype(jnp.int32).reshape(n_blocks * E)
    src = src.reshape(n_blocks, 1, tb)

    wg = w_gate.reshape(G, E, D_MODEL, H).astype(BF16)
    wu = w_up.reshape(G, E, D_MODEL, H).astype(BF16)
    wd = w_down.reshape(G, E * H, D_MODEL).astype(BF16)
    nffn = norm_ffn.reshape(1, -1)

    smem_rows = lambda f: pl.BlockSpec((1, 1, tb), f, memory_space=pltpu.SMEM)
    grid_spec = pltpu.PrefetchScalarGridSpec(
        num_scalar_prefetch=4,
        grid=(n_blocks,),
        in_specs=[smem_rows(lambda i, grp, nv, nact, elist: (i, 0, 0)),
                  smem_rows(lambda i, grp, nv, nact, elist: (jnp.minimum(i + 1, n_blocks - 1), 0, 0)),
                  smem_rows(lambda i, grp, nv, nact, elist: (jnp.minimum(i + 2, n_blocks - 1), 0, 0)),
                  smem_rows(lambda i, grp, nv, nact, elist: (jnp.maximum(i - 1, 0), 0, 0)),
                  pl.BlockSpec(memory_space=pl.ANY),
                  pl.BlockSpec((1, D_MODEL), lambda i, grp, nv, nact, elist: (0, 0)),
                  pl.BlockSpec((1, E, D_MODEL, H), lambda i, grp, nv, nact, elist: (grp[i], 0, 0, 0)),
                  pl.BlockSpec((1, E, D_MODEL, H), lambda i, grp, nv, nact, elist: (grp[i], 0, 0, 0)),
                  pl.BlockSpec((1, E * H, D_MODEL), lambda i, grp, nv, nact, elist: (grp[i], 0, 0))],
        out_specs=pl.BlockSpec(memory_space=pl.ANY),
        scratch_shapes=[pltpu.VMEM((MOE_SLOTS, tb, MOE_ROW_WIDTH), F32), pltpu.VMEM((MOE_SLOTS, tb, D_MODEL), F32),
                        pltpu.VMEM((tb, D_MODEL), BF16), pltpu.VMEM((tb, D_MODEL), F32),
                        pltpu.SemaphoreType.DMA((MOE_SLOTS,)), pltpu.SemaphoreType.DMA((MOE_SLOTS,))])
    return pl.pallas_call(
        _moe_kernel,
        grid_spec=grid_spec,
        out_shape=jax.ShapeDtypeStruct((T, D_MODEL), F32),
        compiler_params=_params(("arbitrary",)),
        name="experts",
    )(block_group, n_valid, n_active, expert_list, src, src, src, src, x2, nffn, wg, wu, wd)


def _pick(n, pref):
    t = min(pref, n)
    while n % t:
        t //= 2
    return t


def _tiles(B, S):
    T = B * S
    return dict(
        inproj_rows=_pick(T, 512),
        mla_q_rows=_pick(S, 512),
        mla_k_rows=_pick(S, 1024),
        cross_rows=_pick(S, 2 * CROSS_SUB_ROWS),
        moe_rows=_pick(T, 256),
    )


def kernel(x, mem, positions, norm_mix, w_in, mla_q_a_norm, mla_w_q_up, mla_kv_a_norm, mla_w_kv_up, mla_q_norm, mla_k_norm, hg_lb_logits, hg_o_norm, w_out, norm_cross, norm_mem, x_w_q, x_w_kv, x_q_norm, x_k_norm, x_w_o, norm_ffn, moe_w_group, moe_b_group, moe_w_expert, moe_b_expert, moe_w_gate, moe_w_up, moe_w_down):
    B, S, D = x.shape
    assert D == D_MODEL and w_in.shape[0] == 1 and S % HG_CHUNK == 0
    T = B * S
    x2d = x.reshape(T, D)
    pos2d = positions.reshape(T, 1).astype(jnp.int32)
    tiles = _tiles(B, S)

    q, k, v, hq, hff, hfb, hi, hg = _inproj(
        x2d, pos2d, norm_mix[0], w_in[0], mla_q_a_norm[0], mla_w_q_up[0], mla_kv_a_norm[0], mla_w_kv_up[0],
        mla_q_norm[0], mla_k_norm[0], tm=tiles["inproj_rows"])
    a = _mla_attention(q, k, v, B, S, tq=tiles["mla_q_rows"], tk=tiles["mla_k_rows"])
    r = _hgrn(hq, hff, hfb, hi, hg, hg_lb_logits, hg_o_norm[0], B, S)
    kx, vx = _memkv(mem, norm_mem[0], x_w_kv[0], x_k_norm[0])
    x2, key = _cross(x2d, a, r, w_out[0], norm_cross[0], x_w_q[0], x_q_norm[0], kx, vx, x_w_o[0], norm_ffn[0],
                     moe_w_group[0], moe_b_group[0], moe_w_expert[0], moe_b_expert[0], B, S, tm=tiles["cross_rows"])
    out = _moe(x2, key, norm_ffn[0], moe_w_gate[0], moe_w_up[0], moe_w_down[0], tb=tiles["moe_rows"])
    return out.reshape(B, S, D)
```

```python
import functools
import math

import jax
import jax.numpy as jnp
from jax import lax
from jax.experimental import pallas as pl
from jax.experimental.pallas import tpu as pltpu

F32 = jnp.float32
BF16 = jnp.bfloat16

LANES = 128
SUBLANES = 8
VMEM_LIMIT_BYTES = 56 * 1024 * 1024

NORM_EPS = 1e-6
LOG2E = math.log2(math.e)

D_MODEL = 1024
MLA_HEADS = 8
MLA_NOPE = 64
MLA_ROPE = 32
MLA_QK = MLA_NOPE + MLA_ROPE
MLA_V = 64
MLA_Q_RANK = 192
MLA_Q_RANK_PAD = 256
MLA_KV_RANK = 128
MLA_HEADS_PER_STEP = 4
ROPE_BASE = 10000.0
CROSS_SUB_ROWS = 256
HG_HEADS = 4
HG_DIM = 128
HG_WIDTH = HG_HEADS * HG_DIM
HG_CHUNK = 128
HG_BAND = 4
HG_HEADS_PER_STEP = 2
HG_CHUNKS_PER_TRIP = 2
HG_MIN_GATE = 2.0 ** -100
X_HEADS = 4
X_HEAD_DIM = D_MODEL // X_HEADS
MOE_GROUPS = 8
MOE_PER_GROUP = 8
MOE_HIDDEN = 256
MOE_GROUP_HIDDEN = MOE_PER_GROUP * MOE_HIDDEN
MOE_SLOTS = 3
MOE_ROW_WIDTH = D_MODEL + LANES
IN_SIZES = (MLA_Q_RANK, MLA_KV_RANK, MLA_ROPE, HG_WIDTH, HG_WIDTH, HG_WIDTH, HG_WIDTH, HG_WIDTH)

NT_DIMS = (((1,), (1,)), ((), ()))
TN_DIMS = (((0,), (0,)), ((), ()))


def _rms(x, n):
    return x * lax.rsqrt(jnp.sum(x * x, axis=-1, keepdims=True) * (1.0 / n) + NORM_EPS)


def _silu(x):
    return x / (1.0 + jnp.exp(-x))


def _split3(x):
    a = x.astype(BF16)
    r = x - a.astype(F32)
    b = r.astype(BF16)
    c = (r - b.astype(F32)).astype(BF16)
    return a, b, c


def _dot(a, b):
    return jnp.dot(a, b, preferred_element_type=F32)


def _params(sem):
    return pltpu.CompilerParams(dimension_semantics=sem, vmem_limit_bytes=VMEM_LIMIT_BYTES)


def _inproj_kernel(x_ref, pos_ref, nmix_ref, w_ref, qan_ref, wq_ref, kvan_ref, wk_ref, wv_ref,
                   qn_ref, kn_ref, invf_ref,
                   q_out, k_out, v_out, hq_out, hff_out, hfb_out, hi_out, hg_out):
    h = (_rms(x_ref[...], D_MODEL) * nmix_ref[...]).astype(BF16)

    p = _dot(h, w_ref[:, 0:512])
    c_q = p[:, 0:MLA_Q_RANK_PAD]
    cqn = (_rms(c_q, MLA_Q_RANK) * qan_ref[...]).astype(BF16)
    q = _dot(cqn, wq_ref[...])
    c_kv = p[:, 256:384]
    ckvn = (_rms(c_kv, MLA_KV_RANK) * kvan_ref[...]).astype(BF16)
    k_nope = _dot(ckvn, wk_ref[...])
    v_out[...] = _dot(ckvn, wv_ref[...]).astype(BF16)
    k_rope = p[:, 384:512]

    ang = pos_ref[...].astype(F32) * invf_ref[...]
    cos = jnp.cos(ang)
    sin = jnp.sin(ang)
    lane = lax.broadcasted_iota(jnp.int32, (1, LANES), 1)
    half = MLA_ROPE // 2
    sin_lo = jnp.where(lane < MLA_NOPE + half, -sin, 0.0)
    sin_hi = jnp.where(lane >= MLA_NOPE + half, sin, 0.0)

    def rope(t):
        return t * cos + pltpu.roll(t, LANES - half, 1) * sin_lo + pltpu.roll(t, half, 1) * sin_hi

    q_scale = MLA_QK ** -0.5 * LOG2E
    mixer_outs = (hq_out, hff_out, hfb_out, hi_out, hg_out)
    for hd in range(MLA_HEADS):
        sl = slice(hd * LANES, (hd + 1) * LANES)
        qh = _rms(q[:, sl], MLA_QK) * qn_ref[...]
        q_out[:, sl] = (rope(qh) * q_scale).astype(BF16)
        kh = _rms(k_nope[:, sl] + k_rope, MLA_QK) * kn_ref[...]
        k_out[:, sl] = rope(kh).astype(BF16)
        if hd < len(mixer_outs):
            out = mixer_outs[hd]
            c0 = 512 + hd * HG_WIDTH
            out[...] = _dot(h, w_ref[:, c0:c0 + HG_WIDTH]).astype(out.dtype)


def _inproj(x2d, pos2d, norm_mix, w_in, q_a_norm, w_q_up, kv_a_norm, w_kv_up, q_norm, k_norm, tm):
    T = x2d.shape[0]
    c0 = 0
    cols = []
    for size in IN_SIZES:
        cols.append(w_in[:, c0:c0 + size])
        c0 += size
    w_cq, w_ckv, w_kr, w_hq, w_hff, w_hfb, w_hi, w_hg = cols
    w_cq = jnp.pad(w_cq, ((0, 0), (0, MLA_Q_RANK_PAD - MLA_Q_RANK)))
    w_kr = jnp.pad(w_kr, ((0, 0), (MLA_NOPE, LANES - MLA_QK)))
    w_big = jnp.concatenate([w_cq, w_ckv, w_kr, w_hq, w_hff, w_hfb, w_hi, w_hg], axis=1).astype(BF16)
    n_big = w_big.shape[1]

    qan = jnp.pad(q_a_norm, (0, MLA_Q_RANK_PAD - MLA_Q_RANK)).reshape(1, -1)
    wq = w_q_up.reshape(MLA_Q_RANK, MLA_HEADS, MLA_QK)
    wq = jnp.pad(wq, ((0, MLA_Q_RANK_PAD - MLA_Q_RANK), (0, 0), (0, LANES - MLA_QK)))
    wq = wq.reshape(MLA_Q_RANK_PAD, MLA_HEADS * LANES).astype(BF16)
    wkv = w_kv_up.reshape(MLA_KV_RANK, MLA_HEADS, MLA_NOPE + MLA_V)
    wk = jnp.pad(wkv[:, :, :MLA_NOPE], ((0, 0), (0, 0), (0, LANES - MLA_NOPE)))
    wk = wk.reshape(MLA_KV_RANK, MLA_HEADS * LANES).astype(BF16)
    wv = wkv[:, :, MLA_NOPE:].reshape(MLA_KV_RANK, MLA_HEADS * MLA_V).astype(BF16)
    qn = jnp.pad(q_norm, (0, LANES - MLA_QK)).reshape(1, LANES)
    kn = jnp.pad(k_norm, (0, LANES - MLA_QK)).reshape(1, LANES)
    half = MLA_ROPE // 2
    inv_freq = 1.0 / (ROPE_BASE ** (jnp.arange(half, dtype=F32) / half))
    invf = jnp.concatenate([jnp.zeros((MLA_NOPE,), F32), inv_freq, inv_freq,
                            jnp.zeros((LANES - MLA_QK,), F32)]).reshape(1, LANES)

    def full(a):
        return pl.BlockSpec(a.shape, lambda i: (0,) * a.ndim)

    def rows(width):
        return pl.BlockSpec((tm, width), lambda i: (i, 0))

    nmix = norm_mix.reshape(1, -1)
    kvan = kv_a_norm.reshape(1, -1)
    qk_w = MLA_HEADS * LANES
    v_w = MLA_HEADS * MLA_V
    out_shape = (
        jax.ShapeDtypeStruct((T, qk_w), BF16), jax.ShapeDtypeStruct((T, qk_w), BF16),
        jax.ShapeDtypeStruct((T, v_w), BF16),
        jax.ShapeDtypeStruct((T, HG_WIDTH), BF16), jax.ShapeDtypeStruct((T, HG_WIDTH), F32),
        jax.ShapeDtypeStruct((T, HG_WIDTH), F32), jax.ShapeDtypeStruct((T, HG_WIDTH), BF16),
        jax.ShapeDtypeStruct((T, HG_WIDTH), BF16))
    return pl.pallas_call(
        _inproj_kernel,
        grid=(T // tm,),
        in_specs=[rows(D_MODEL), rows(1), full(nmix), full(w_big), full(qan), full(wq), full(kvan),
                  full(wk), full(wv), full(qn), full(kn), full(invf)],
        out_specs=(rows(qk_w), rows(qk_w), rows(v_w), rows(HG_WIDTH), rows(HG_WIDTH), rows(HG_WIDTH),
                   rows(HG_WIDTH), rows(HG_WIDTH)),
        out_shape=out_shape,
        compiler_params=_params(("parallel",)),
        name="inproj",
    )(x2d, pos2d, nmix, w_big, qan, wq, kvan, wk, wv, qn, kn, invf)


def _mla_kernel(q_ref, k_ref, v_ref, o_ref, *, tk):
    heads = MLA_HEADS_PER_STEP
    n_chunks = k_ref.shape[0] // tk
    nsub = tk // LANES
    lane = lax.broadcasted_iota(jnp.int32, (1, LANES), 1)
    own = [lane < MLA_V, lane >= MLA_V]
    m = [None] * heads
    acc = [None] * heads
    for c in range(n_chunks):
        rows = slice(c * tk, (c + 1) * tk)
        scores = [lax.dot_general(q_ref[:, j * LANES:(j + 1) * LANES], k_ref[rows, j * LANES:(j + 1) * LANES],
                                  NT_DIMS, preferred_element_type=F32) for j in range(heads)]
        for j, s in enumerate(scores):
            vv = v_ref[rows, (j // 2) * LANES:(j // 2 + 1) * LANES]
            blk_max = s[:, 0:LANES]
            for i in range(1, nsub):
                blk_max = jnp.maximum(blk_max, s[:, i * LANES:(i + 1) * LANES])
            m_new = jnp.max(blk_max, axis=-1, keepdims=True)
            if c > 0:
                m_new = jnp.maximum(m[j], m_new)
            p = jnp.exp2((s - m_new).astype(BF16))
            pv = _dot(p, jnp.where(own[j % 2], vv, jnp.ones_like(vv)))
            acc[j] = pv if c == 0 else acc[j] * jnp.exp2(m[j] - m_new) + pv
            m[j] = m_new
    for pair in range(heads // 2):
        a0, a1 = acc[2 * pair], acc[2 * pair + 1]
        o0 = a0 / a0[:, MLA_V:MLA_V + 1]
        o1 = a1 / a1[:, 0:1]
        o_ref[:, pair * LANES:(pair + 1) * LANES] = jnp.where(own[0], o0, o1).astype(BF16)


def _mla_attention(q, k, v, B, S, tq, tk):
    T = B * S
    nq = S // tq
    heads = MLA_HEADS_PER_STEP
    return pl.pallas_call(
        functools.partial(_mla_kernel, tk=tk),
        grid=(B, MLA_HEADS // heads, nq),
        in_specs=[pl.BlockSpec((tq, heads * LANES), lambda b, h, i: (b * nq + i, h)),
                  pl.BlockSpec((S, heads * LANES), lambda b, h, i: (b, h)),
                  pl.BlockSpec((S, heads * MLA_V), lambda b, h, i: (b, h))],
        out_specs=pl.BlockSpec((tq, heads * MLA_V), lambda b, h, i: (b * nq + i, h)),
        out_shape=jax.ShapeDtypeStruct((T, MLA_HEADS * MLA_V), BF16),
        compiler_params=_params(("parallel", "parallel", "arbitrary")),
        name="mla_attention",
    )(q, k, v)


def _hgrn_pair_codes(rev):
    C = HG_CHUNK
    row = lax.broadcasted_iota(jnp.int32, (C, C), 0)
    col = lax.broadcasted_iota(jnp.int32, (C, C), 1)
    dist = (col - row) if rev else (row - col)
    code = jnp.full((C, C), -1, jnp.int32)
    m, level = C // 2, HG_BAND
    levels = []
    while m >= HG_BAND:
        levels.append(m)
        m //= 2
    for j, m in enumerate(levels):
        same = (row // (2 * m)) == (col // (2 * m))
        code = jnp.where(same, HG_BAND + len(levels) - 1 - j, code)
    code = jnp.where((row // HG_BAND) == (col // HG_BAND), dist, code)
    return jnp.where(dist < 0, -1, code)


def _hgrn_chunks(chains):
    C = HG_CHUNK
    row = lax.broadcasted_iota(jnp.int32, (C, 1), 0)
    col = lax.broadcasted_iota(jnp.int32, (1, C), 1)
    n = len(chains)

    kks, fs, bs = [], [], []
    for q, z, v, lb_row, state, code, rev in chains:
        kk = (1.0 - lb_row) / (1.0 + jnp.exp(z))
        f = jnp.maximum(1.0 - kk, HG_MIN_GATE)
        g = jnp.log2(f)
        tri = jnp.where((col >= row) if rev else (col <= row), 1.0, 0.0).astype(BF16)
        g1, g2, g3 = _split3(g)
        kks.append(kk)
        fs.append(f)
        bs.append(_dot(tri, g1) + _dot(tri, g2) + _dot(tri, g3))

    outs, states = [], []
    for (q, z, v, lb_row, state, code, rev), kk, b in zip(chains, kks, bs):
        if isinstance(state, int):
            state = states[state]
        q_hat = (q * jnp.exp2(b)).astype(BF16)
        outs.append(lax.dot_general(q_hat, state.astype(BF16), NT_DIMS, preferred_element_type=F32))
        b_end = b[0:1, :] if rev else b[C - 1:C, :]
        k_hat = (kk * jnp.exp2(b_end - b)).astype(BF16)
        states.append(state * jnp.exp2(b_end) + lax.dot_general(v, k_hat, TN_DIMS, preferred_element_type=F32))

    attns = []
    for (q, z, v, lb_row, state, code, rev), kk, f in zip(chains, kks, fs):
        step = (C - 1) if rev else 1
        u = kk
        attn = jnp.where(code == 0, jnp.sum(q * u, axis=-1, keepdims=True), 0.0)
        for d in range(1, HG_BAND):
            u = f * pltpu.roll(u, step, 0)
            attn = jnp.where(code == d, jnp.sum(q * u, axis=-1, keepdims=True), attn)
        attns.append(attn)

    m, level = HG_BAND, HG_BAND
    while m < C:
        for idx in range(n):
            q, z, v, lb_row, state, code, rev = chains[idx]
            b3 = bs[idx].reshape(C // (2 * m), 2 * m, HG_DIM)
            ref = b3[:, m:m + 1, :] if rev else b3[:, m - 1:m, :]
            e = jnp.exp2(-jnp.abs(b3 - ref)).reshape(C, HG_DIM)
            a_m = lax.dot_general((q * e).astype(BF16), (kks[idx] * e).astype(BF16), NT_DIMS,
                                  preferred_element_type=F32)
            attns[idx] = jnp.where(code == level, a_m, attns[idx])
        m, level = 2 * m, level + 1

    return [(o + _dot(attn.astype(BF16), ch[2]), st) for o, attn, ch, st in zip(outs, attns, chains, states)]


def _hgrn_kernel(hq_ref, hff_ref, hfb_ref, hi_ref, hg_ref, lbl_ref, onorm_ref, out_ref,
                 q_scr, of_scr, ob_scr, code_scr):
    C = HG_CHUNK
    n_chunks = hq_ref.shape[0] // C
    lg = lbl_ref[...]
    mx = jnp.maximum(lg[0], lg[1])
    e0 = jnp.exp(lg[0] - mx)
    lb = e0 / (e0 + jnp.exp(lg[1] - mx))
    q_scr[...] = _silu(hq_ref[...].astype(F32))
    code_scr[0] = _hgrn_pair_codes(False)
    code_scr[1] = _hgrn_pair_codes(True)

    per_trip = HG_CHUNKS_PER_TRIP if n_chunks % HG_CHUNKS_PER_TRIP == 0 else 1

    def body(i, states):
        chains, dest = [], []
        for hd in range(HG_HEADS_PER_STEP):
            sl = slice(hd * HG_DIM, (hd + 1) * HG_DIM)
            for u in range(per_trip):
                sf = pl.multiple_of((i * per_trip + u) * C, C)
                sb = pl.multiple_of((n_chunks - 1 - (i * per_trip + u)) * C, C)
                state_f = states[hd][0] if u == 0 else len(chains) - 2
                chains.append((q_scr[pl.ds(sf, C), sl], hff_ref[pl.ds(sf, C), sl], hi_ref[pl.ds(sf, C), sl],
                               lb[0:1, sl], state_f, code_scr[0], False))
                dest.append((of_scr, sf, sl))
                state_b = states[hd][1] if u == 0 else len(chains) - 2
                chains.append((q_scr[pl.ds(sb, C), sl], hfb_ref[pl.ds(sb, C), sl], hi_ref[pl.ds(sb, C), sl],
                               lb[1:2, sl], state_b, code_scr[1], True))
                dest.append((ob_scr, sb, sl))
        results = _hgrn_chunks(chains)
        for (o, _), (scr, start, sl) in zip(results, dest):
            scr[pl.ds(start, C), sl] = o
        last = 2 * per_trip
        return tuple((results[hd * last + last - 2][1], results[hd * last + last - 1][1])
                     for hd in range(HG_HEADS_PER_STEP))

    zero = jnp.zeros((HG_DIM, HG_DIM), F32)
    lax.fori_loop(0, n_chunks // per_trip, body, ((zero, zero),) * HG_HEADS_PER_STEP)
    for hd in range(HG_HEADS_PER_STEP):
        sl = slice(hd * HG_DIM, (hd + 1) * HG_DIM)
        o = of_scr[:, sl] + ob_scr[:, sl]
        out_ref[:, sl] = ((_rms(o, HG_DIM) * onorm_ref[...]).astype(BF16)
                          * _silu(hg_ref[:, sl].astype(F32)).astype(BF16))


def _hgrn(hq, hff, hfb, hi, hg, lb_logits, o_norm, B, S):
    T = B * S
    width = HG_HEADS_PER_STEP * HG_DIM
    blk = pl.BlockSpec((S, width), lambda b, h: (b, h))
    n_layers = lb_logits.shape[0]
    return pl.pallas_call(
        _hgrn_kernel,
        grid=(B, HG_HEADS // HG_HEADS_PER_STEP),
        in_specs=[blk, blk, blk, blk, blk,
                  pl.BlockSpec((n_layers, 2, width), lambda b, h: (0, 0, h)),
                  pl.BlockSpec((1, HG_DIM), lambda b, h: (0, 0))],
        out_specs=blk,
        out_shape=jax.ShapeDtypeStruct((T, HG_WIDTH), BF16),
        scratch_shapes=[pltpu.VMEM((S, width), F32), pltpu.VMEM((S, width), F32), pltpu.VMEM((S, width), F32),
                        pltpu.VMEM((2, HG_CHUNK, HG_CHUNK), jnp.int32)],
        compiler_params=_params(("parallel", "parallel")),
        name="hgrn2",
    )(hq, hff, hfb, hi, hg, lb_logits, o_norm.reshape(1, HG_DIM))


def _memkv_kernel(mem_ref, nmem_ref, wkv_ref, kn_ref, k_out, v_out):
    hm = (_rms(mem_ref[0], D_MODEL) * nmem_ref[...]).astype(BF16)
    kv = _dot(hm, wkv_ref[...])
    for hd in range(X_HEADS):
        sl = slice(hd * X_HEAD_DIM, (hd + 1) * X_HEAD_DIM)
        k_out[0, :, sl] = (_rms(kv[:, sl], X_HEAD_DIM) * kn_ref[...]).astype(BF16)
    v_out[0] = kv[:, D_MODEL:].astype(BF16)


def _memkv(mem, norm_mem, w_kv, k_norm):
    B, M, _ = mem.shape
    wkv = w_kv.astype(BF16)
    blk = pl.BlockSpec((1, M, D_MODEL), lambda b: (b, 0, 0))
    return pl.pallas_call(
        _memkv_kernel,
        grid=(B,),
        in_specs=[blk, pl.BlockSpec((1, D_MODEL), lambda b: (0, 0)),
                  pl.BlockSpec(wkv.shape, lambda b: (0, 0)),
                  pl.BlockSpec((1, X_HEAD_DIM), lambda b: (0, 0))],
        out_specs=(blk, blk),
        out_shape=(jax.ShapeDtypeStruct((B, M, D_MODEL), BF16), jax.ShapeDtypeStruct((B, M, D_MODEL), BF16)),
        compiler_params=_params(("parallel",)),
        name="mem_kv",
    )(mem, norm_mem.reshape(1, -1), wkv, k_norm.reshape(1, -1))


def _router_logits(h, w_ref, bias):
    h_hi = h.astype(BF16)
    h_lo = (h - h_hi.astype(F32)).astype(BF16)
    both = _dot(h_hi, w_ref[...])
    return both[:, :LANES] + both[:, LANES:] + _dot(h_lo, w_ref[:, :LANES]) + bias


def _split_hi_lo(w):
    hi = w.astype(BF16)
    return jnp.concatenate([hi, (w - hi.astype(F32)).astype(BF16)], axis=-1)


def _first_max_lane(vals, lane_f):
    mx = jnp.max(vals, axis=-1, keepdims=True)
    idx = jnp.min(jnp.where(vals == mx, lane_f, float(LANES)), axis=-1, keepdims=True)
    return mx, idx


def _cross_kernel(x_ref, a_ref, r_ref, woa_ref, wor_ref, ncross_ref, wq_ref, qn_ref, kx_ref, vx_ref, wxo_ref,
                  nffn_ref, wg_ref, bg_ref, x2_out, key_out):
    tm = x_ref.shape[0]
    sub = min(CROSS_SUB_ROWS, tm)
    blocks = [slice(r0, r0 + sub) for r0 in range(0, tm, sub)]
    q_scale = X_HEAD_DIM ** -0.5 * LOG2E
    lane = lax.broadcasted_iota(jnp.int32, (1, LANES), 1)
    lane_f = lane.astype(F32)

    x1s = [x_ref[rows, :] + _dot(a_ref[rows, :], woa_ref[...]) + _dot(r_ref[rows, :], wor_ref[...])
           for rows in blocks]
    qxs = [_dot((_rms(x1, D_MODEL) * ncross_ref[...]).astype(BF16), wq_ref[...]) for x1 in x1s]
    heads = [[] for _ in blocks]
    for hd in range(X_HEADS):
        sl = slice(hd * X_HEAD_DIM, (hd + 1) * X_HEAD_DIM)
        for n, qx in enumerate(qxs):
            qh = (_rms(qx[:, sl], X_HEAD_DIM) * qn_ref[...] * q_scale).astype(BF16)
            s = lax.dot_general(qh, kx_ref[0, :, sl], NT_DIMS, preferred_element_type=F32)
            p = jnp.exp2(s - jnp.max(s, axis=-1, keepdims=True))
            o = _dot(p.astype(BF16), vx_ref[0, :, sl]) / jnp.sum(p, axis=-1, keepdims=True)
            heads[n].append(o.astype(BF16))
    x2s = [x1 + _dot(jnp.concatenate(hs, axis=-1), wxo_ref[...]) for x1, hs in zip(x1s, heads)]
    for rows, x2 in zip(blocks, x2s):
        x2_out[rows, 0:D_MODEL] = x2
    for rows, x2 in zip(blocks, x2s):
        h3 = _rms(x2, D_MODEL) * nffn_ref[...]
        logits = _router_logits(h3, wg_ref, bg_ref[...])
        g_logits = jnp.where(lane < MOE_GROUPS, logits, -jnp.inf)
        g_max, g_idx = _first_max_lane(g_logits, lane_f)
        g_weight = 1.0 / jnp.sum(jnp.exp(g_logits - g_max), axis=-1, keepdims=True)
        first = MOE_GROUPS + MOE_PER_GROUP * g_idx
        in_group = jnp.logical_and(lane_f >= first, lane_f < first + MOE_PER_GROUP)
        e_logits = jnp.where(in_group, logits, -jnp.inf)
        e1, i1 = _first_max_lane(e_logits, lane_f)
        e2, i2 = _first_max_lane(jnp.where(lane_f == i1, -jnp.inf, e_logits), lane_f)
        t = jnp.exp(e2 - e1)
        w1 = g_weight / (1.0 + t)
        w2 = g_weight * t / (1.0 + t)
        l1 = i1 - first
        l2 = i2 - first
        x2_out[rows, D_MODEL:D_MODEL + LANES] = jnp.where(lane_f == l1, w1, jnp.where(lane_f == l2, w2, 0.0))
        lower = jnp.minimum(l1, l2)
        upper = jnp.maximum(l1, l2)
        odd = lower - 2.0 * jnp.floor(lower * 0.5)
        pair = lower * MOE_PER_GROUP + jnp.where(odd == 0.0, upper, (MOE_PER_GROUP - 1) - upper)
        key_out[rows, :] = (g_idx * (MOE_PER_GROUP * MOE_PER_GROUP) + pair).astype(jnp.int32)


def _cross(x2d, a, r, w_out, norm_cross, w_q, q_norm, kx, vx, w_o, norm_ffn, w_group, b_group, w_expert, b_expert,
           B, S, tm):
    T = B * S
    per_b = S // tm
    M = kx.shape[1]
    woa = w_out[:MLA_HEADS * MLA_V].astype(BF16)
    wor = w_out[MLA_HEADS * MLA_V:].astype(BF16)
    wq = w_q.astype(BF16)
    wxo = w_o.astype(BF16)
    n_router = MOE_GROUPS + MOE_GROUPS * MOE_PER_GROUP
    wg = _split_hi_lo(jnp.pad(jnp.concatenate([w_group, w_expert], axis=1), ((0, 0), (0, LANES - n_router))))
    bg = jnp.pad(jnp.concatenate([b_group, b_expert]), (0, LANES - n_router)).reshape(1, LANES)

    def full(arr):
        return pl.BlockSpec(arr.shape, lambda i: (0,) * arr.ndim)

    def rows(width):
        return pl.BlockSpec((tm, width), lambda i: (i, 0))

    ncross = norm_cross.reshape(1, -1)
    qn = q_norm.reshape(1, -1)
    nffn = norm_ffn.reshape(1, -1)
    mem_blk = pl.BlockSpec((1, M, D_MODEL), lambda i: (i // per_b, 0, 0))
    return pl.pallas_call(
        _cross_kernel,
        grid=(T // tm,),
        in_specs=[rows(D_MODEL), rows(MLA_HEADS * MLA_V), rows(HG_WIDTH), full(woa), full(wor), full(ncross),
                  full(wq), full(qn), mem_blk, mem_blk, full(wxo), full(nffn), full(wg), full(bg)],
        out_specs=(rows(MOE_ROW_WIDTH), rows(1)),
        out_shape=(jax.ShapeDtypeStruct((T, MOE_ROW_WIDTH), F32), jax.ShapeDtypeStruct((T, 1), jnp.int32)),
        compiler_params=_params(("parallel",)),
        name="cross",
    )(x2d, a, r, woa, wor, ncross, wq, qn, kx, vx, wxo, nffn, wg, bg)


def _moe_kernel(grp_ref, nvalid_ref, nactive_ref, elist_ref,
                src_ref, src_next_ref, src_next2_ref, src_prev_ref,
                x_hbm, nffn_ref, wg_ref, wu_ref, wd_ref,
                out_hbm, xbuf, ybuf, hb_scr, y_scr, gsem, ssem):
    i = pl.program_id(0)
    n_blocks = pl.num_programs(0)
    tb = xbuf.shape[1]
    slot = i % MOE_SLOTS
    slot_next2 = (i + 2) % MOE_SLOTS
    slot_prev = (i + MOE_SLOTS - 1) % MOE_SLOTS

    def n_valid_at(j):
        inside = jnp.logical_and(j >= 0, j < n_blocks)
        return jnp.where(inside, nvalid_ref[jnp.clip(j, 0, n_blocks - 1)], 0)

    n_valid = nvalid_ref[i]
    n_prev = n_valid_at(i - 1)
    n_next2 = n_valid_at(i + 2)

    def gather_row(idx_ref, r, to_slot):
        tok = idx_ref[0, 0, r]
        pltpu.make_async_copy(x_hbm.at[pl.ds(tok, 1)], xbuf.at[to_slot, pl.ds(r, 1)], gsem.at[to_slot]).start()

    def scatter_row(idx_ref, r, from_slot):
        tok = idx_ref[0, 0, r]
        pltpu.make_async_copy(ybuf.at[from_slot, pl.ds(r, 1)], out_hbm.at[pl.ds(tok, 1)], ssem.at[from_slot]).start()

    n_active = nactive_ref[i]

    def gather_all(idx_ref, to_slot):
        def body(r, carry):
            gather_row(idx_ref, r, to_slot)
            return carry
        lax.fori_loop(0, tb, body, 0)

    def scatter_some(idx_ref, from_slot, count):
        def body(r, carry):
            scatter_row(idx_ref, r, from_slot)
            return carry
        lax.fori_loop(0, count, body, 0)

    def wait_scatter(of_slot, count):
        whole = pl.multiple_of((count // SUBLANES) * SUBLANES, SUBLANES)

        @pl.when(whole > 0)
        def _():
            pltpu.make_async_copy(ybuf.at[of_slot, pl.ds(0, whole)], out_hbm.at[pl.ds(0, whole)],
                                  ssem.at[of_slot]).wait()

        def one(r, carry):
            pltpu.make_async_copy(ybuf.at[of_slot, pl.ds(0, 1)], out_hbm.at[pl.ds(0, 1)], ssem.at[of_slot]).wait()
            return carry
        lax.fori_loop(0, count - whole, one, 0)

    def evaluate(side_work):
        pltpu.make_async_copy(x_hbm.at[pl.ds(0, tb)], xbuf.at[slot], gsem.at[slot]).wait()
        x = xbuf[slot, :, 0:D_MODEL]
        hb_scr[...] = (_rms(x, D_MODEL) * nffn_ref[...]).astype(BF16)
        lane = lax.broadcasted_iota(jnp.int32, (1, LANES), 1)

        def expert(k, first):
            e = elist_ref[i * MOE_PER_GROUP + k]
            hb = hb_scr[...]
            gate = _dot(hb, wg_ref[0, e])
            up = _dot(hb, wu_ref[0, e])
            slab = xbuf[slot, :, D_MODEL:MOE_ROW_WIDTH]
            w_e = jnp.sum(jnp.where(lane == e, slab, 0.0), axis=-1, keepdims=True)
            act = jnp.where(w_e != 0.0, _silu(gate) * up * w_e, 0.0).astype(BF16)
            rows = pl.ds(pl.multiple_of(e * MOE_HIDDEN, MOE_HIDDEN), MOE_HIDDEN)
            y = _dot(act, wd_ref[0, rows, :])
            y_scr[...] = y if first else y_scr[...] + y

        side_work(0)
        expert(0, True)
        side_work(1)
        expert(1, False)
        for k in range(2, MOE_PER_GROUP):
            @pl.when(k < n_active)
            def _(k=k):
                expert(k, False)
        ybuf[slot] = xbuf[slot, :, 0:D_MODEL] + y_scr[...]

    @pl.when(i == 0)
    def _():
        @pl.when(n_valid > 0)
        def _():
            gather_all(src_ref, 0)

        @pl.when(n_valid_at(1) > 0)
        def _():
            gather_all(src_next_ref, 1)

    wait_scatter(slot, n_valid_at(i - MOE_SLOTS))

    steady = jnp.logical_and(jnp.logical_and(n_valid > 0, n_next2 > 0), n_prev == tb)
    share = tb // 2

    @pl.when(steady)
    def _():
        def side_work(part):
            for r in range(part * share, (part + 1) * share):
                gather_row(src_next2_ref, r, slot_next2)
                scatter_row(src_prev_ref, r, slot_prev)
        evaluate(side_work)

    @pl.when(jnp.logical_not(steady))
    def _():
        @pl.when(n_next2 > 0)
        def _():
            gather_all(src_next2_ref, slot_next2)
        scatter_some(src_prev_ref, slot_prev, n_prev)

        @pl.when(n_valid > 0)
        def _():
            evaluate(lambda part: None)

    @pl.when(i == n_blocks - 1)
    def _():
        wait_scatter((i + MOE_SLOTS - 2) % MOE_SLOTS, n_valid_at(i - 2))
        wait_scatter(slot_prev, n_prev)
        scatter_some(src_ref, slot, n_valid)
        wait_scatter(slot, n_valid)


def _moe(x2, key, norm_ffn, w_gate, w_up, w_down, tb):
    T = x2.shape[0]
    G, E, H = MOE_GROUPS, MOE_PER_GROUP, MOE_HIDDEN
    n_blocks = T // tb + G

    key = key.reshape(T)
    g = key // (E * E)
    order = jnp.argsort(key, stable=True).astype(jnp.int32)
    counts = jnp.sum(g[:, None] == jnp.arange(G, dtype=jnp.int32)[None, :], axis=0).astype(jnp.int32)
    blocks_per_group = (counts + tb - 1) // tb
    block_end = jnp.cumsum(blocks_per_group)
    block_start = block_end - blocks_per_group
    token_start = jnp.cumsum(counts) - counts
    bi = jnp.arange(n_blocks, dtype=jnp.int32)
    block_group = jnp.minimum(jnp.sum(bi[:, None] >= block_end[None, :], axis=1), G - 1).astype(jnp.int32)
    first_row = (bi - block_start[block_group]) * tb
    n_valid = jnp.clip(counts[block_group] - first_row, 0, tb)
    n_valid = jnp.where(bi < block_end[G - 1], n_valid, 0).astype(jnp.int32)
    r = jnp.arange(tb, dtype=jnp.int32)[None, :]
    pos = token_start[block_group][:, None] + first_row[:, None] + r
    valid = r < n_valid[:, None]
    src = jnp.where(valid, order[jnp.clip(pos, 0, T - 1)], 0).astype(jnp.int32)
    pair = key[src] % (E * E)
    lower = pair // E
    upper = jnp.where(lower % 2 == 0, pair % E, (E - 1) - pair % E)
    used = jnp.left_shift(1, lower) | jnp.left_shift(1, upper)
    active = lax.reduce(jnp.where(valid, used, 0).astype(jnp.int32), jnp.int32(0), lax.bitwise_or, (1,))
    is_used = jnp.bitwise_and(jnp.right_shift(active[:, None], jnp.arange(E, dtype=jnp.int32)[None, :]), 1)
    n_active = jnp.sum(is_used, axis=1).astype(jnp.int32)
    expert_list = jnp.argsort(1 - is_used, axis=1, stable=True).astype(jnp.int32).reshape(n_blocks * E)
    src = src.reshape(n_blocks, 1, tb)

    wg = w_gate.reshape(G, E, D_MODEL, H).astype(BF16)
    wu = w_up.reshape(G, E, D_MODEL, H).astype(BF16)
    wd = w_down.reshape(G, E * H, D_MODEL).astype(BF16)
    nffn = norm_ffn.reshape(1, -1)

    smem_rows = lambda f: pl.BlockSpec((1, 1, tb), f, memory_space=pltpu.SMEM)
    grid_spec = pltpu.PrefetchScalarGridSpec(
        num_scalar_prefetch=4,
        grid=(n_blocks,),
        in_specs=[smem_rows(lambda i, grp, nv, nact, elist: (i, 0, 0)),
                  smem_rows(lambda i, grp, nv, nact, elist: (jnp.minimum(i + 1, n_blocks - 1), 0, 0)),
                  smem_rows(lambda i, grp, nv, nact, elist: (jnp.minimum(i + 2, n_blocks - 1), 0, 0)),
                  smem_rows(lambda i, grp, nv, nact, elist: (jnp.maximum(i - 1, 0), 0, 0)),
                  pl.BlockSpec(memory_space=pl.ANY),
                  pl.BlockSpec((1, D_MODEL), lambda i, grp, nv, nact, elist: (0, 0)),
                  pl.BlockSpec((1, E, D_MODEL, H), lambda i, grp, nv, nact, elist: (grp[i], 0, 0, 0)),
                  pl.BlockSpec((1, E, D_MODEL, H), lambda i, grp, nv, nact, elist: (grp[i], 0, 0, 0)),
                  pl.BlockSpec((1, E * H, D_MODEL), lambda i, grp, nv, nact, elist: (grp[i], 0, 0))],
        out_specs=pl.BlockSpec(memory_space=pl.ANY),
        scratch_shapes=[pltpu.VMEM((MOE_SLOTS, tb, MOE_ROW_WIDTH), F32), pltpu.VMEM((MOE_SLOTS, tb, D_MODEL), F32),
                        pltpu.VMEM((tb, D_MODEL), BF16), pltpu.VMEM((tb, D_MODEL), F32),
                        pltpu.SemaphoreType.DMA((MOE_SLOTS,)), pltpu.SemaphoreType.DMA((MOE_SLOTS,))])
    return pl.pallas_call(
        _moe_kernel,
        grid_spec=grid_spec,
        out_shape=jax.ShapeDtypeStruct((T, D_MODEL), F32),
        compiler_params=_params(("arbitrary",)),
        name="experts",
    )(block_group, n_valid, n_active, expert_list, src, src, src, src, x2, nffn, wg, wu, wd)


def _pick(n, pref):
    t = min(pref, n)
    while n % t:
        t //= 2
    return t


def _tiles(B, S):
    T = B * S
    return dict(
        inproj_rows=_pick(T, 512),
        mla_q_rows=_pick(S, 512),
        mla_k_rows=_pick(S, 1024),
        cross_rows=_pick(S, 4 * CROSS_SUB_ROWS),
        moe_rows=_pick(T, 256),
    )


def kernel(x, mem, positions, norm_mix, w_in, mla_q_a_norm, mla_w_q_up, mla_kv_a_norm, mla_w_kv_up, mla_q_norm, mla_k_norm, hg_lb_logits, hg_o_norm, w_out, norm_cross, norm_mem, x_w_q, x_w_kv, x_q_norm, x_k_norm, x_w_o, norm_ffn, moe_w_group, moe_b_group, moe_w_expert, moe_b_expert, moe_w_gate, moe_w_up, moe_w_down):
    B, S, D = x.shape
    assert D == D_MODEL and w_in.shape[0] == 1 and S % HG_CHUNK == 0
    T = B * S
    x2d = x.reshape(T, D)
    pos2d = positions.reshape(T, 1).astype(jnp.int32)
    tiles = _tiles(B, S)

    q, k, v, hq, hff, hfb, hi, hg = _inproj(
        x2d, pos2d, norm_mix[0], w_in[0], mla_q_a_norm[0], mla_w_q_up[0], mla_kv_a_norm[0], mla_w_kv_up[0],
        mla_q_norm[0], mla_k_norm[0], tm=tiles["inproj_rows"])
    a = _mla_attention(q, k, v, B, S, tq=tiles["mla_q_rows"], tk=tiles["mla_k_rows"])
    r = _hgrn(hq, hff, hfb, hi, hg, hg_lb_logits, hg_o_norm[0], B, S)
    kx, vx = _memkv(mem, norm_mem[0], x_w_kv[0], x_k_norm[0])
    x2, key = _cross(x2d, a, r, w_out[0], norm_cross[0], x_w_q[0], x_q_norm[0], kx, vx, x_w_o[0], norm_ffn[0],
                     moe_w_group[0], moe_b_group[0], moe_w_expert[0], moe_b_expert[0], B, S, tm=tiles["cross_rows"])
    out = _moe(x2, key, norm_ffn[0], moe_w_gate[0], moe_w_up[0], moe_w_down[0], tb=tiles["moe_rows"])
    return out.reshape(B, S, D)
```

```python
import functools
import math

import jax
import jax.numpy as jnp
from jax import lax
from jax.experimental import pallas as pl
from jax.experimental.pallas import tpu as pltpu

F32 = jnp.float32
BF16 = jnp.bfloat16

LANES = 128
SUBLANES = 8
VMEM_LIMIT_BYTES = 56 * 1024 * 1024

NORM_EPS = 1e-6
LOG2E = math.log2(math.e)

D_MODEL = 1024
MLA_HEADS = 8
MLA_NOPE = 64
MLA_ROPE = 32
MLA_QK = MLA_NOPE + MLA_ROPE
MLA_V = 64
MLA_Q_RANK = 192
MLA_Q_RANK_PAD = 256
MLA_KV_RANK = 128
MLA_HEADS_PER_STEP = 4
ROPE_BASE = 10000.0
CROSS_SUB_ROWS = 256
HG_HEADS = 4
HG_DIM = 128
HG_WIDTH = HG_HEADS * HG_DIM
HG_CHUNK = 128
HG_BAND = 4
HG_HEADS_PER_STEP = 2
HG_CHUNKS_PER_TRIP = 2
HG_MIN_GATE = 2.0 ** -100
X_HEADS = 4
X_HEAD_DIM = D_MODEL // X_HEADS
MOE_GROUPS = 8
MOE_PER_GROUP = 8
MOE_HIDDEN = 256
MOE_GROUP_HIDDEN = MOE_PER_GROUP * MOE_HIDDEN
MOE_SLOTS = 3
MOE_ROW_WIDTH = D_MODEL + LANES
IN_SIZES = (MLA_Q_RANK, MLA_KV_RANK, MLA_ROPE, HG_WIDTH, HG_WIDTH, HG_WIDTH, HG_WIDTH, HG_WIDTH)

NT_DIMS = (((1,), (1,)), ((), ()))
TN_DIMS = (((0,), (0,)), ((), ()))


def _rms(x, n):
    return x * lax.rsqrt(jnp.sum(x * x, axis=-1, keepdims=True) * (1.0 / n) + NORM_EPS)


def _silu(x):
    return x / (1.0 + jnp.exp(-x))


def _split3(x):
    a = x.astype(BF16)
    r = x - a.astype(F32)
    b = r.astype(BF16)
    c = (r - b.astype(F32)).astype(BF16)
    return a, b, c


def _dot(a, b):
    return jnp.dot(a, b, preferred_element_type=F32)


def _params(sem):
    return pltpu.CompilerParams(dimension_semantics=sem, vmem_limit_bytes=VMEM_LIMIT_BYTES)


def _inproj_kernel(x_ref, pos_ref, nmix_ref, w_ref, qan_ref, wq_ref, kvan_ref, wk_ref, wv_ref,
                   qn_ref, kn_ref, invf_ref,
                   q_out, k_out, v_out, hq_out, hff_out, hfb_out, hi_out, hg_out):
    h = (_rms(x_ref[...], D_MODEL) * nmix_ref[...]).astype(BF16)

    p = _dot(h, w_ref[:, 0:512])
    c_q = p[:, 0:MLA_Q_RANK_PAD]
    cqn = (_rms(c_q, MLA_Q_RANK) * qan_ref[...]).astype(BF16)
    q = _dot(cqn, wq_ref[...])
    c_kv = p[:, 256:384]
    ckvn = (_rms(c_kv, MLA_KV_RANK) * kvan_ref[...]).astype(BF16)
    k_nope = _dot(ckvn, wk_ref[...])
    v_out[...] = _dot(ckvn, wv_ref[...]).astype(BF16)
    k_rope = p[:, 384:512]

    ang = pos_ref[...].astype(F32) * invf_ref[...]
    cos = jnp.cos(ang)
    sin = jnp.sin(ang)
    lane = lax.broadcasted_iota(jnp.int32, (1, LANES), 1)
    half = MLA_ROPE // 2
    sin_lo = jnp.where(lane < MLA_NOPE + half, -sin, 0.0)
    sin_hi = jnp.where(lane >= MLA_NOPE + half, sin, 0.0)

    def rope(t):
        return t * cos + pltpu.roll(t, LANES - half, 1) * sin_lo + pltpu.roll(t, half, 1) * sin_hi

    q_scale = MLA_QK ** -0.5 * LOG2E
    mixer_outs = (hq_out, hff_out, hfb_out, hi_out, hg_out)
    for hd in range(MLA_HEADS):
        sl = slice(hd * LANES, (hd + 1) * LANES)
        qh = _rms(q[:, sl], MLA_QK) * qn_ref[...]
        q_out[:, sl] = (rope(qh) * q_scale).astype(BF16)
        kh = _rms(k_nope[:, sl] + k_rope, MLA_QK) * kn_ref[...]
        k_out[:, sl] = rope(kh).astype(BF16)
        if hd < len(mixer_outs):
            out = mixer_outs[hd]
            c0 = 512 + hd * HG_WIDTH
            out[...] = _dot(h, w_ref[:, c0:c0 + HG_WIDTH]).astype(out.dtype)


def _inproj(x2d, pos2d, norm_mix, w_in, q_a_norm, w_q_up, kv_a_norm, w_kv_up, q_norm, k_norm, tm):
    T = x2d.shape[0]
    c0 = 0
    cols = []
    for size in IN_SIZES:
        cols.append(w_in[:, c0:c0 + size])
        c0 += size
    w_cq, w_ckv, w_kr, w_hq, w_hff, w_hfb, w_hi, w_hg = cols
    w_cq = jnp.pad(w_cq, ((0, 0), (0, MLA_Q_RANK_PAD - MLA_Q_RANK)))
    w_kr = jnp.pad(w_kr, ((0, 0), (MLA_NOPE, LANES - MLA_QK)))
    w_big = jnp.concatenate([w_cq, w_ckv, w_kr, w_hq, w_hff, w_hfb, w_hi, w_hg], axis=1).astype(BF16)
    n_big = w_big.shape[1]

    qan = jnp.pad(q_a_norm, (0, MLA_Q_RANK_PAD - MLA_Q_RANK)).reshape(1, -1)
    wq = w_q_up.reshape(MLA_Q_RANK, MLA_HEADS, MLA_QK)
    wq = jnp.pad(wq, ((0, MLA_Q_RANK_PAD - MLA_Q_RANK), (0, 0), (0, LANES - MLA_QK)))
    wq = wq.reshape(MLA_Q_RANK_PAD, MLA_HEADS * LANES).astype(BF16)
    wkv = w_kv_up.reshape(MLA_KV_RANK, MLA_HEADS, MLA_NOPE + MLA_V)
    wk = jnp.pad(wkv[:, :, :MLA_NOPE], ((0, 0), (0, 0), (0, LANES - MLA_NOPE)))
    wk = wk.reshape(MLA_KV_RANK, MLA_HEADS * LANES).astype(BF16)
    wv = wkv[:, :, MLA_NOPE:].reshape(MLA_KV_RANK, MLA_HEADS * MLA_V).astype(BF16)
    qn = jnp.pad(q_norm, (0, LANES - MLA_QK)).reshape(1, LANES)
    kn = jnp.pad(k_norm, (0, LANES - MLA_QK)).reshape(1, LANES)
    half = MLA_ROPE // 2
    inv_freq = 1.0 / (ROPE_BASE ** (jnp.arange(half, dtype=F32) / half))
    invf = jnp.concatenate([jnp.zeros((MLA_NOPE,), F32), inv_freq, inv_freq,
                            jnp.zeros((LANES - MLA_QK,), F32)]).reshape(1, LANES)

    def full(a):
        return pl.BlockSpec(a.shape, lambda i: (0,) * a.ndim)

    def rows(width):
        return pl.BlockSpec((tm, width), lambda i: (i, 0))

    nmix = norm_mix.reshape(1, -1)
    kvan = kv_a_norm.reshape(1, -1)
    qk_w = MLA_HEADS * LANES
    v_w = MLA_HEADS * MLA_V
    out_shape = (
        jax.ShapeDtypeStruct((T, qk_w), BF16), jax.ShapeDtypeStruct((T, qk_w), BF16),
        jax.ShapeDtypeStruct((T, v_w), BF16),
        jax.ShapeDtypeStruct((T, HG_WIDTH), BF16), jax.ShapeDtypeStruct((T, HG_WIDTH), F32),
        jax.ShapeDtypeStruct((T, HG_WIDTH), F32), jax.ShapeDtypeStruct((T, HG_WIDTH), BF16),
        jax.ShapeDtypeStruct((T, HG_WIDTH), BF16))
    return pl.pallas_call(
        _inproj_kernel,
        grid=(T // tm,),
        in_specs=[rows(D_MODEL), rows(1), full(nmix), full(w_big), full(qan), full(wq), full(kvan),
                  full(wk), full(wv), full(qn), full(kn), full(invf)],
        out_specs=(rows(qk_w), rows(qk_w), rows(v_w), rows(HG_WIDTH), rows(HG_WIDTH), rows(HG_WIDTH),
                   rows(HG_WIDTH), rows(HG_WIDTH)),
        out_shape=out_shape,
        compiler_params=_params(("parallel",)),
        name="inproj",
    )(x2d, pos2d, nmix, w_big, qan, wq, kvan, wk, wv, qn, kn, invf)


def _mla_kernel(q_ref, k_ref, v_ref, o_ref, *, tk):
    heads = MLA_HEADS_PER_STEP
    n_chunks = k_ref.shape[0] // tk
    nsub = tk // LANES
    lane = lax.broadcasted_iota(jnp.int32, (1, LANES), 1)
    own = [lane < MLA_V, lane >= MLA_V]
    m = [None] * heads
    acc = [None] * heads
    for c in range(n_chunks):
        rows = slice(c * tk, (c + 1) * tk)
        scores = [lax.dot_general(q_ref[:, j * LANES:(j + 1) * LANES], k_ref[rows, j * LANES:(j + 1) * LANES],
                                  NT_DIMS, preferred_element_type=F32) for j in range(heads)]
        for j, s in enumerate(scores):
            vv = v_ref[rows, (j // 2) * LANES:(j // 2 + 1) * LANES]
            blk_max = s[:, 0:LANES]
            for i in range(1, nsub):
                blk_max = jnp.maximum(blk_max, s[:, i * LANES:(i + 1) * LANES])
            m_new = jnp.max(blk_max, axis=-1, keepdims=True)
            if c > 0:
                m_new = jnp.maximum(m[j], m_new)
            p = jnp.exp2((s - m_new).astype(BF16))
            pv = _dot(p, jnp.where(own[j % 2], vv, jnp.ones_like(vv)))
            acc[j] = pv if c == 0 else acc[j] * jnp.exp2(m[j] - m_new) + pv
            m[j] = m_new
    for pair in range(heads // 2):
        a0, a1 = acc[2 * pair], acc[2 * pair + 1]
        o0 = a0 / a0[:, MLA_V:MLA_V + 1]
        o1 = a1 / a1[:, 0:1]
        o_ref[:, pair * LANES:(pair + 1) * LANES] = jnp.where(own[0], o0, o1).astype(BF16)


def _mla_attention(q, k, v, B, S, tq, tk):
    T = B * S
    nq = S // tq
    heads = MLA_HEADS_PER_STEP
    return pl.pallas_call(
        functools.partial(_mla_kernel, tk=tk),
        grid=(B, MLA_HEADS // heads, nq),
        in_specs=[pl.BlockSpec((tq, heads * LANES), lambda b, h, i: (b * nq + i, h)),
                  pl.BlockSpec((S, heads * LANES), lambda b, h, i: (b, h)),
                  pl.BlockSpec((S, heads * MLA_V), lambda b, h, i: (b, h))],
        out_specs=pl.BlockSpec((tq, heads * MLA_V), lambda b, h, i: (b * nq + i, h)),
        out_shape=jax.ShapeDtypeStruct((T, MLA_HEADS * MLA_V), BF16),
        compiler_params=_params(("parallel", "parallel", "arbitrary")),
        name="mla_attention",
    )(q, k, v)


def _hgrn_pair_codes(rev):
    C = HG_CHUNK
    row = lax.broadcasted_iota(jnp.int32, (C, C), 0)
    col = lax.broadcasted_iota(jnp.int32, (C, C), 1)
    dist = (col - row) if rev else (row - col)
    code = jnp.full((C, C), -1, jnp.int32)
    m, level = C // 2, HG_BAND
    levels = []
    while m >= HG_BAND:
        levels.append(m)
        m //= 2
    for j, m in enumerate(levels):
        same = (row // (2 * m)) == (col // (2 * m))
        code = jnp.where(same, HG_BAND + len(levels) - 1 - j, code)
    code = jnp.where((row // HG_BAND) == (col // HG_BAND), dist, code)
    return jnp.where(dist < 0, -1, code)


def _hgrn_chunks(chains):
    C = HG_CHUNK
    row = lax.broadcasted_iota(jnp.int32, (C, 1), 0)
    col = lax.broadcasted_iota(jnp.int32, (1, C), 1)
    n = len(chains)

    kks, fs, bs = [], [], []
    for q, z, v, lb_row, state, code, rev in chains:
        kk = (1.0 - lb_row) / (1.0 + jnp.exp(z))
        f = jnp.maximum(1.0 - kk, HG_MIN_GATE)
        g = jnp.log2(f)
        tri = jnp.where((col >= row) if rev else (col <= row), 1.0, 0.0).astype(BF16)
        g1, g2, g3 = _split3(g)
        kks.append(kk)
        fs.append(f)
        bs.append(_dot(tri, g1) + _dot(tri, g2) + _dot(tri, g3))

    outs, states = [], []
    for (q, z, v, lb_row, state, code, rev), kk, b in zip(chains, kks, bs):
        if isinstance(state, int):
            state = states[state]
        q_hat = (q * jnp.exp2(b)).astype(BF16)
        outs.append(lax.dot_general(q_hat, state.astype(BF16), NT_DIMS, preferred_element_type=F32))
        b_end = b[0:1, :] if rev else b[C - 1:C, :]
        k_hat = (kk * jnp.exp2(b_end - b)).astype(BF16)
        states.append(state * jnp.exp2(b_end) + lax.dot_general(v, k_hat, TN_DIMS, preferred_element_type=F32))

    attns = []
    for (q, z, v, lb_row, state, code, rev), kk, f in zip(chains, kks, fs):
        step = (C - 1) if rev else 1
        u = kk
        attn = jnp.where(code == 0, jnp.sum(q * u, axis=-1, keepdims=True), 0.0)
        for d in range(1, HG_BAND):
            u = f * pltpu.roll(u, step, 0)
            attn = jnp.where(code == d, jnp.sum(q * u, axis=-1, keepdims=True), attn)
        attns.append(attn)

    m, level = HG_BAND, HG_BAND
    while m < C:
        for idx in range(n):
            q, z, v, lb_row, state, code, rev = chains[idx]
            b3 = bs[idx].reshape(C // (2 * m), 2 * m, HG_DIM)
            ref = b3[:, m:m + 1, :] if rev else b3[:, m - 1:m, :]
            e = jnp.exp2(-jnp.abs(b3 - ref)).reshape(C, HG_DIM)
            a_m = lax.dot_general((q * e).astype(BF16), (kks[idx] * e).astype(BF16), NT_DIMS,
                                  preferred_element_type=F32)
            attns[idx] = jnp.where(code == level, a_m, attns[idx])
        m, level = 2 * m, level + 1

    return [(o + _dot(attn.astype(BF16), ch[2]), st) for o, attn, ch, st in zip(outs, attns, chains, states)]


def _hgrn_kernel(hq_ref, hff_ref, hfb_ref, hi_ref, hg_ref, lbl_ref, onorm_ref, out_ref,
                 q_scr, of_scr, ob_scr, code_scr):
    C = HG_CHUNK
    n_chunks = hq_ref.shape[0] // C
    lg = lbl_ref[...]
    mx = jnp.maximum(lg[0], lg[1])
    e0 = jnp.exp(lg[0] - mx)
    lb = e0 / (e0 + jnp.exp(lg[1] - mx))
    q_scr[...] = _silu(hq_ref[...].astype(F32))
    code_scr[0] = _hgrn_pair_codes(False)
    code_scr[1] = _hgrn_pair_codes(True)

    per_trip = HG_CHUNKS_PER_TRIP if n_chunks % HG_CHUNKS_PER_TRIP == 0 else 1

    def body(i, states):
        chains, dest = [], []
        for hd in range(HG_HEADS_PER_STEP):
            sl = slice(hd * HG_DIM, (hd + 1) * HG_DIM)
            for u in range(per_trip):
                sf = pl.multiple_of((i * per_trip + u) * C, C)
                sb = pl.multiple_of((n_chunks - 1 - (i * per_trip + u)) * C, C)
                state_f = states[hd][0] if u == 0 else len(chains) - 2
                chains.append((q_scr[pl.ds(sf, C), sl], hff_ref[pl.ds(sf, C), sl], hi_ref[pl.ds(sf, C), sl],
                               lb[0:1, sl], state_f, code_scr[0], False))
                dest.append((of_scr, sf, sl))
                state_b = states[hd][1] if u == 0 else len(chains) - 2
                chains.append((q_scr[pl.ds(sb, C), sl], hfb_ref[pl.ds(sb, C), sl], hi_ref[pl.ds(sb, C), sl],
                               lb[1:2, sl], state_b, code_scr[1], True))
                dest.append((ob_scr, sb, sl))
        results = _hgrn_chunks(chains)
        for (o, _), (scr, start, sl) in zip(results, dest):
            scr[pl.ds(start, C), sl] = o
        last = 2 * per_trip
        return tuple((results[hd * last + last - 2][1], results[hd * last + last - 1][1])
                     for hd in range(HG_HEADS_PER_STEP))

    zero = jnp.zeros((HG_DIM, HG_DIM), F32)
    lax.fori_loop(0, n_chunks // per_trip, body, ((zero, zero),) * HG_HEADS_PER_STEP)
    for hd in range(HG_HEADS_PER_STEP):
        sl = slice(hd * HG_DIM, (hd + 1) * HG_DIM)
        o = of_scr[:, sl] + ob_scr[:, sl]
        out_ref[:, sl] = ((_rms(o, HG_DIM) * onorm_ref[...]).astype(BF16)
                          * _silu(hg_ref[:, sl].astype(F32)).astype(BF16))


def _hgrn(hq, hff, hfb, hi, hg, lb_logits, o_norm, B, S):
    T = B * S
    width = HG_HEADS_PER_STEP * HG_DIM
    blk = pl.BlockSpec((S, width), lambda b, h: (b, h))
    n_layers = lb_logits.shape[0]
    return pl.pallas_call(
        _hgrn_kernel,
        grid=(B, HG_HEADS // HG_HEADS_PER_STEP),
        in_specs=[blk, blk, blk, blk, blk,
                  pl.BlockSpec((n_layers, 2, width), lambda b, h: (0, 0, h)),
                  pl.BlockSpec((1, HG_DIM), lambda b, h: (0, 0))],
        out_specs=blk,
        out_shape=jax.ShapeDtypeStruct((T, HG_WIDTH), BF16),
        scratch_shapes=[pltpu.VMEM((S, width), F32), pltpu.VMEM((S, width), F32), pltpu.VMEM((S, width), F32),
                        pltpu.VMEM((2, HG_CHUNK, HG_CHUNK), jnp.int32)],
        compiler_params=_params(("parallel", "parallel")),
        name="hgrn2",
    )(hq, hff, hfb, hi, hg, lb_logits, o_norm.reshape(1, HG_DIM))


def _memkv_kernel(mem_ref, nmem_ref, wkv_ref, kn_ref, k_out, v_out):
    hm = (_rms(mem_ref[0], D_MODEL) * nmem_ref[...]).astype(BF16)
    kv = _dot(hm, wkv_ref[...])
    for hd in range(X_HEADS):
        sl = slice(hd * X_HEAD_DIM, (hd + 1) * X_HEAD_DIM)
        k_out[0, :, sl] = (_rms(kv[:, sl], X_HEAD_DIM) * kn_ref[...]).astype(BF16)
    v_out[0] = kv[:, D_MODEL:].astype(BF16)


def _memkv(mem, norm_mem, w_kv, k_norm):
    B, M, _ = mem.shape
    wkv = w_kv.astype(BF16)
    blk = pl.BlockSpec((1, M, D_MODEL), lambda b: (b, 0, 0))
    return pl.pallas_call(
        _memkv_kernel,
        grid=(B,),
        in_specs=[blk, pl.BlockSpec((1, D_MODEL), lambda b: (0, 0)),
                  pl.BlockSpec(wkv.shape, lambda b: (0, 0)),
                  pl.BlockSpec((1, X_HEAD_DIM), lambda b: (0, 0))],
        out_specs=(blk, blk),
        out_shape=(jax.ShapeDtypeStruct((B, M, D_MODEL), BF16), jax.ShapeDtypeStruct((B, M, D_MODEL), BF16)),
        compiler_params=_params(("parallel",)),
        name="mem_kv",
    )(mem, norm_mem.reshape(1, -1), wkv, k_norm.reshape(1, -1))


def _router_logits(h, w_ref, bias):
    h_hi = h.astype(BF16)
    h_lo = (h - h_hi.astype(F32)).astype(BF16)
    both = _dot(h_hi, w_ref[...])
    return both[:, :LANES] + both[:, LANES:] + _dot(h_lo, w_ref[:, :LANES]) + bias


def _split_hi_lo(w):
    hi = w.astype(BF16)
    return jnp.concatenate([hi, (w - hi.astype(F32)).astype(BF16)], axis=-1)


def _first_max_lane(vals, lane_f):
    mx = jnp.max(vals, axis=-1, keepdims=True)
    idx = jnp.min(jnp.where(vals == mx, lane_f, float(LANES)), axis=-1, keepdims=True)
    return mx, idx


def _cross_kernel(x_ref, a_ref, r_ref, woa_ref, wor_ref, ncross_ref, wq_ref, qn_ref, kx_ref, vx_ref, wxo_ref,
                  nffn_ref, wg_ref, bg_ref, x2_out, key_out):
    tm = x_ref.shape[0]
    sub = min(CROSS_SUB_ROWS, tm)
    blocks = [slice(r0, r0 + sub) for r0 in range(0, tm, sub)]
    q_scale = X_HEAD_DIM ** -0.5 * LOG2E
    lane = lax.broadcasted_iota(jnp.int32, (1, LANES), 1)
    lane_f = lane.astype(F32)

    x1s = [x_ref[rows, :] + _dot(a_ref[rows, :], woa_ref[...]) + _dot(r_ref[rows, :], wor_ref[...])
           for rows in blocks]
    qxs = [_dot((_rms(x1, D_MODEL) * ncross_ref[...]).astype(BF16), wq_ref[...]) for x1 in x1s]
    heads = [[] for _ in blocks]
    for hd in range(X_HEADS):
        sl = slice(hd * X_HEAD_DIM, (hd + 1) * X_HEAD_DIM)
        for n, qx in enumerate(qxs):
            qh = (_rms(qx[:, sl], X_HEAD_DIM) * qn_ref[...] * q_scale).astype(BF16)
            s = lax.dot_general(qh, kx_ref[0, :, sl], NT_DIMS, preferred_element_type=F32)
            p = jnp.exp2(s - jnp.max(s, axis=-1, keepdims=True))
            o = _dot(p.astype(BF16), vx_ref[0, :, sl]) / jnp.sum(p, axis=-1, keepdims=True)
            heads[n].append(o.astype(BF16))
    x2s = [x1 + _dot(jnp.concatenate(hs, axis=-1), wxo_ref[...]) for x1, hs in zip(x1s, heads)]
    for rows, x2 in zip(blocks, x2s):
        x2_out[rows, 0:D_MODEL] = x2
    for rows, x2 in zip(blocks, x2s):
        h3 = _rms(x2, D_MODEL) * nffn_ref[...]
        logits = _router_logits(h3, wg_ref, bg_ref[...])
        g_logits = jnp.where(lane < MOE_GROUPS, logits, -jnp.inf)
        g_max, g_idx = _first_max_lane(g_logits, lane_f)
        g_weight = 1.0 / jnp.sum(jnp.exp(g_logits - g_max), axis=-1, keepdims=True)
        first = MOE_GROUPS + MOE_PER_GROUP * g_idx
        in_group = jnp.logical_and(lane_f >= first, lane_f < first + MOE_PER_GROUP)
        e_logits = jnp.where(in_group, logits, -jnp.inf)
        e1, i1 = _first_max_lane(e_logits, lane_f)
        e2, i2 = _first_max_lane(jnp.where(lane_f == i1, -jnp.inf, e_logits), lane_f)
        t = jnp.exp(e2 - e1)
        w1 = g_weight / (1.0 + t)
        w2 = g_weight * t / (1.0 + t)
        l1 = i1 - first
        l2 = i2 - first
        x2_out[rows, D_MODEL:D_MODEL + LANES] = jnp.where(lane_f == l1, w1, jnp.where(lane_f == l2, w2, 0.0))
        lower = jnp.minimum(l1, l2)
        upper = jnp.maximum(l1, l2)
        odd = lower - 2.0 * jnp.floor(lower * 0.5)
        pair = lower * MOE_PER_GROUP + jnp.where(odd == 0.0, upper, (MOE_PER_GROUP - 1) - upper)
        key_out[rows, :] = (g_idx * (MOE_PER_GROUP * MOE_PER_GROUP) + pair).astype(jnp.int32)


def _cross(x2d, a, r, w_out, norm_cross, w_q, q_norm, kx, vx, w_o, norm_ffn, w_group, b_group, w_expert, b_expert,
           B, S, tm):
    T = B * S
    per_b = S // tm
    M = kx.shape[1]
    woa = w_out[:MLA_HEADS * MLA_V].astype(BF16)
    wor = w_out[MLA_HEADS * MLA_V:].astype(BF16)
    wq = w_q.astype(BF16)
    wxo = w_o.astype(BF16)
    n_router = MOE_GROUPS + MOE_GROUPS * MOE_PER_GROUP
    wg = _split_hi_lo(jnp.pad(jnp.concatenate([w_group, w_expert], axis=1), ((0, 0), (0, LANES - n_router))))
    bg = jnp.pad(jnp.concatenate([b_group, b_expert]), (0, LANES - n_router)).reshape(1, LANES)

    def full(arr):
        return pl.BlockSpec(arr.shape, lambda i: (0,) * arr.ndim)

    def rows(width):
        return pl.BlockSpec((tm, width), lambda i: (i, 0))

    ncross = norm_cross.reshape(1, -1)
    qn = q_norm.reshape(1, -1)
    nffn = norm_ffn.reshape(1, -1)
    mem_blk = pl.BlockSpec((1, M, D_MODEL), lambda i: (i // per_b, 0, 0))
    return pl.pallas_call(
        _cross_kernel,
        grid=(T // tm,),
        in_specs=[rows(D_MODEL), rows(MLA_HEADS * MLA_V), rows(HG_WIDTH), full(woa), full(wor), full(ncross),
                  full(wq), full(qn), mem_blk, mem_blk, full(wxo), full(nffn), full(wg), full(bg)],
        out_specs=(rows(MOE_ROW_WIDTH), rows(1)),
        out_shape=(jax.ShapeDtypeStruct((T, MOE_ROW_WIDTH), F32), jax.ShapeDtypeStruct((T, 1), jnp.int32)),
        compiler_params=_params(("parallel",)),
        name="cross",
    )(x2d, a, r, woa, wor, ncross, wq, qn, kx, vx, wxo, nffn, wg, bg)


def _moe_kernel(grp_ref, nvalid_ref, nactive_ref, elist_ref,
                src_ref, src_next_ref, src_next2_ref, src_prev_ref,
                x_hbm, nffn_ref, wg_ref, wu_ref, wd_ref,
                out_hbm, xbuf, ybuf, hb_scr, y_scr, gsem, ssem):
    i = pl.program_id(0)
    n_blocks = pl.num_programs(0)
    tb = xbuf.shape[1]
    slot = i % MOE_SLOTS
    slot_next2 = (i + 2) % MOE_SLOTS
    slot_prev = (i + MOE_SLOTS - 1) % MOE_SLOTS

    def n_valid_at(j):
        inside = jnp.logical_and(j >= 0, j < n_blocks)
        return jnp.where(inside, nvalid_ref[jnp.clip(j, 0, n_blocks - 1)], 0)

    n_valid = nvalid_ref[i]
    n_prev = n_valid_at(i - 1)
    n_next2 = n_valid_at(i + 2)

    def row_priority(r):
        return r % 2 if isinstance(r, int) else 0

    def gather_row(idx_ref, r, to_slot):
        tok = idx_ref[0, 0, r]
        pltpu.make_async_copy(x_hbm.at[pl.ds(tok, 1)], xbuf.at[to_slot, pl.ds(r, 1)],
                              gsem.at[to_slot]).start(priority=row_priority(r))

    def scatter_row(idx_ref, r, from_slot):
        tok = idx_ref[0, 0, r]
        pltpu.make_async_copy(ybuf.at[from_slot, pl.ds(r, 1)], out_hbm.at[pl.ds(tok, 1)],
                              ssem.at[from_slot]).start(priority=row_priority(r))

    n_active = nactive_ref[i]

    def gather_all(idx_ref, to_slot):
        def body(r, carry):
            gather_row(idx_ref, r, to_slot)
            return carry
        lax.fori_loop(0, tb, body, 0)

    def scatter_some(idx_ref, from_slot, count):
        def body(r, carry):
            scatter_row(idx_ref, r, from_slot)
            return carry
        lax.fori_loop(0, count, body, 0)

    def wait_scatter(of_slot, count):
        whole = pl.multiple_of((count // SUBLANES) * SUBLANES, SUBLANES)

        @pl.when(whole > 0)
        def _():
            pltpu.make_async_copy(ybuf.at[of_slot, pl.ds(0, whole)], out_hbm.at[pl.ds(0, whole)],
                                  ssem.at[of_slot]).wait()

        def one(r, carry):
            pltpu.make_async_copy(ybuf.at[of_slot, pl.ds(0, 1)], out_hbm.at[pl.ds(0, 1)], ssem.at[of_slot]).wait()
            return carry
        lax.fori_loop(0, count - whole, one, 0)

    def evaluate(side_work):
        pltpu.make_async_copy(x_hbm.at[pl.ds(0, tb)], xbuf.at[slot], gsem.at[slot]).wait()
        x = xbuf[slot, :, 0:D_MODEL]
        hb_scr[...] = (_rms(x, D_MODEL) * nffn_ref[...]).astype(BF16)
        lane = lax.broadcasted_iota(jnp.int32, (1, LANES), 1)

        def expert(k, first):
            e = elist_ref[i * MOE_PER_GROUP + k]
            hb = hb_scr[...]
            gate = _dot(hb, wg_ref[0, e])
            up = _dot(hb, wu_ref[0, e])
            slab = xbuf[slot, :, D_MODEL:MOE_ROW_WIDTH]
            w_e = jnp.sum(jnp.where(lane == e, slab, 0.0), axis=-1, keepdims=True)
            act = jnp.where(w_e != 0.0, _silu(gate) * up * w_e, 0.0).astype(BF16)
            rows = pl.ds(pl.multiple_of(e * MOE_HIDDEN, MOE_HIDDEN), MOE_HIDDEN)
            y = _dot(act, wd_ref[0, rows, :])
            y_scr[...] = y if first else y_scr[...] + y

        side_work(0)
        expert(0, True)
        side_work(1)
        expert(1, False)
        for k in range(2, MOE_PER_GROUP):
            @pl.when(k < n_active)
            def _(k=k):
                expert(k, False)
        ybuf[slot] = xbuf[slot, :, 0:D_MODEL] + y_scr[...]

    @pl.when(i == 0)
    def _():
        @pl.when(n_valid > 0)
        def _():
            gather_all(src_ref, 0)

        @pl.when(n_valid_at(1) > 0)
        def _():
            gather_all(src_next_ref, 1)

    wait_scatter(slot, n_valid_at(i - MOE_SLOTS))

    steady = jnp.logical_and(jnp.logical_and(n_valid > 0, n_next2 > 0), n_prev == tb)
    share = tb // 2

    @pl.when(steady)
    def _():
        def side_work(part):
            for r in range(part * share, (part + 1) * share):
                gather_row(src_next2_ref, r, slot_next2)
                scatter_row(src_prev_ref, r, slot_prev)
        evaluate(side_work)

    @pl.when(jnp.logical_not(steady))
    def _():
        @pl.when(n_next2 > 0)
        def _():
            gather_all(src_next2_ref, slot_next2)
        scatter_some(src_prev_ref, slot_prev, n_prev)

        @pl.when(n_valid > 0)
        def _():
            evaluate(lambda part: None)

    @pl.when(i == n_blocks - 1)
    def _():
        wait_scatter((i + MOE_SLOTS - 2) % MOE_SLOTS, n_valid_at(i - 2))
        wait_scatter(slot_prev, n_prev)
        scatter_some(src_ref, slot, n_valid)
        wait_scatter(slot, n_valid)


def _moe(x2, key, norm_ffn, w_gate, w_up, w_down, tb):
    T = x2.shape[0]
    G, E, H = MOE_GROUPS, MOE_PER_GROUP, MOE_HIDDEN
    n_blocks = T // tb + G

    key = key.reshape(T)
    g = key // (E * E)
    order = jnp.argsort(key, stable=True).astype(jnp.int32)
    counts = jnp.sum(g[:, None] == jnp.arange(G, dtype=jnp.int32)[None, :], axis=0).astype(jnp.int32)
    blocks_per_group = (counts + tb - 1) // tb
    block_end = jnp.cumsum(blocks_per_group)
    block_start = block_end - blocks_per_group
    token_start = jnp.cumsum(counts) - counts
    bi = jnp.arange(n_blocks, dtype=jnp.int32)
    block_group = jnp.minimum(jnp.sum(bi[:, None] >= block_end[None, :], axis=1), G - 1).astype(jnp.int32)
    first_row = (bi - block_start[block_group]) * tb
    n_valid = jnp.clip(counts[block_group] - first_row, 0, tb)
    n_valid = jnp.where(bi < block_end[G - 1], n_valid, 0).astype(jnp.int32)
    r = jnp.arange(tb, dtype=jnp.int32)[None, :]
    pos = token_start[block_group][:, None] + first_row[:, None] + r
    valid = r < n_valid[:, None]
    src = jnp.where(valid, order[jnp.clip(pos, 0, T - 1)], 0).astype(jnp.int32)
    pair = key[src] % (E * E)
    lower = pair // E
    upper = jnp.where(lower % 2 == 0, pair % E, (E - 1) - pair % E)
    used = jnp.left_shift(1, lower) | jnp.left_shift(1, upper)
    active = lax.reduce(jnp.where(valid, used, 0).astype(jnp.int32), jnp.int32(0), lax.bitwise_or, (1,))
    is_used = jnp.bitwise_and(jnp.right_shift(active[:, None], jnp.arange(E, dtype=jnp.int32)[None, :]), 1)
    n_active = jnp.sum(is_used, axis=1).astype(jnp.int32)
    expert_list = jnp.argsort(1 - is_used, axis=1, stable=True).astype(jnp.int32).reshape(n_blocks * E)
    src = src.reshape(n_blocks, 1, tb)

    wg = w_gate.reshape(G, E, D_MODEL, H).astype(BF16)
    wu = w_up.reshape(G, E, D_MODEL, H).astype(BF16)
    wd = w_down.reshape(G, E * H, D_MODEL).astype(BF16)
    nffn = norm_ffn.reshape(1, -1)

    smem_rows = lambda f: pl.BlockSpec((1, 1, tb), f, memory_space=pltpu.SMEM)
    grid_spec = pltpu.PrefetchScalarGridSpec(
        num_scalar_prefetch=4,
        grid=(n_blocks,),
        in_specs=[smem_rows(lambda i, grp, nv, nact, elist: (i, 0, 0)),
                  smem_rows(lambda i, grp, nv, nact, elist: (jnp.minimum(i + 1, n_blocks - 1), 0, 0)),
                  smem_rows(lambda i, grp, nv, nact, elist: (jnp.minimum(i + 2, n_blocks - 1), 0, 0)),
                  smem_rows(lambda i, grp, nv, nact, elist: (jnp.maximum(i - 1, 0), 0, 0)),
                  pl.BlockSpec(memory_space=pl.ANY),
                  pl.BlockSpec((1, D_MODEL), lambda i, grp, nv, nact, elist: (0, 0)),
                  pl.BlockSpec((1, E, D_MODEL, H), lambda i, grp, nv, nact, elist: (grp[i], 0, 0, 0)),
                  pl.BlockSpec((1, E, D_MODEL, H), lambda i, grp, nv, nact, elist: (grp[i], 0, 0, 0)),
                  pl.BlockSpec((1, E * H, D_MODEL), lambda i, grp, nv, nact, elist: (grp[i], 0, 0))],
        out_specs=pl.BlockSpec(memory_space=pl.ANY),
        scratch_shapes=[pltpu.VMEM((MOE_SLOTS, tb, MOE_ROW_WIDTH), F32), pltpu.VMEM((MOE_SLOTS, tb, D_MODEL), F32),
                        pltpu.VMEM((tb, D_MODEL), BF16), pltpu.VMEM((tb, D_MODEL), F32),
                        pltpu.SemaphoreType.DMA((MOE_SLOTS,)), pltpu.SemaphoreType.DMA((MOE_SLOTS,))])
    return pl.pallas_call(
        _moe_kernel,
        grid_spec=grid_spec,
        out_shape=jax.ShapeDtypeStruct((T, D_MODEL), F32),
        compiler_params=_params(("arbitrary",)),
        name="experts",
    )(block_group, n_valid, n_active, expert_list, src, src, src, src, x2, nffn, wg, wu, wd)


def _pick(n, pref):
    t = min(pref, n)
    while n % t:
        t //= 2
    return t


def _tiles(B, S):
    T = B * S
    return dict(
        inproj_rows=_pick(T, 512),
        mla_q_rows=_pick(S, 512),
        mla_k_rows=_pick(S, 1024),
        cross_rows=_pick(S, 4 * CROSS_SUB_ROWS),
        moe_rows=_pick(T, 256),
    )


def kernel(x, mem, positions, norm_mix, w_in, mla_q_a_norm, mla_w_q_up, mla_kv_a_norm, mla_w_kv_up, mla_q_norm, mla_k_norm, hg_lb_logits, hg_o_norm, w_out, norm_cross, norm_mem, x_w_q, x_w_kv, x_q_norm, x_k_norm, x_w_o, norm_ffn, moe_w_group, moe_b_group, moe_w_expert, moe_b_expert, moe_w_gate, moe_w_up, moe_w_down):
    B, S, D = x.shape
    assert D == D_MODEL and w_in.shape[0] == 1 and S % HG_CHUNK == 0
    T = B * S
    x2d = x.reshape(T, D)
    pos2d = positions.reshape(T, 1).astype(jnp.int32)
    tiles = _tiles(B, S)

    q, k, v, hq, hff, hfb, hi, hg = _inproj(
        x2d, pos2d, norm_mix[0], w_in[0], mla_q_a_norm[0], mla_w_q_up[0], mla_kv_a_norm[0], mla_w_kv_up[0],
        mla_q_norm[0], mla_k_norm[0], tm=tiles["inproj_rows"])
    a = _mla_attention(q, k, v, B, S, tq=tiles["mla_q_rows"], tk=tiles["mla_k_rows"])
    r = _hgrn(hq, hff, hfb, hi, hg, hg_lb_logits, hg_o_norm[0], B, S)
    kx, vx = _memkv(mem, norm_mem[0], x_w_kv[0], x_k_norm[0])
    x2, key = _cross(x2d, a, r, w_out[0], norm_cross[0], x_w_q[0], x_q_norm[0], kx, vx, x_w_o[0], norm_ffn[0],
                     moe_w_group[0], moe_b_group[0], moe_w_expert[0], moe_b_expert[0], B, S, tm=tiles["cross_rows"])
    out = _moe(x2, key, norm_ffn[0], moe_w_gate[0], moe_w_up[0], moe_w_down[0], tb=tiles["moe_rows"])
    return out.reshape(B, S, D)
```
